```python
import math
import jax
import jax.numpy as jnp
from jax import lax
import numpy as np

D_MODEL = 1024
BATCH = 8
SEQ = 4096
DEPTH = 1

GRID_W = 64
CTX_LEN = 256
MIX_W = D_MODEL
NA_W = MIX_W // 2
NA_DH = 64
NA_HEADS = NA_W // NA_DH
NA_KR = 8
NA_KC = 16
S5_W = MIX_W - NA_W
S5_CG = 16
S5_G = S5_W // S5_CG
S5_P = 64
D_FF = ((8 * D_MODEL // 3 + 255) // 256) * 256
IN_COLS = 3 * NA_W + S5_W
DN_ALPHA = (2.0 * DEPTH) ** 0.25
DN_BETA = (8.0 * DEPTH) ** -0.25
LN_EPS = 1e-6
DT_MIN = 1e-3
DT_MAX = 1e-1

kernel_name = 'hybrid_na_s5_macaron_dit_layer'


def layer_norm(x, g, b):
    xf = x.astype(jnp.float32)
    mu = jnp.mean(xf, axis=-1, keepdims=True)
    var = jnp.mean(jnp.square(xf - mu), axis=-1, keepdims=True)
    return ((xf - mu) * lax.rsqrt(var + LN_EPS)).astype(x.dtype) * g + b


def post_norm(stream, update, g, b):
    return layer_norm(DN_ALPHA * stream + update, g, b)


def modulate(h, shift, scale):
    return h * (1 + scale) + shift


def swiglu(h, w_up, w_down):
    a, g = jnp.split(h @ w_up, 2, axis=-1)
    return (jax.nn.silu(g) * a) @ w_down


def neighborhood_attention(q, k, v, k_ctx, v_ctx, rpb):
    B, S, H, Dh = q.shape
    rows = S // GRID_W
    kr = min(NA_KR, rows)
    kc = NA_KC
    n_win = kr * kc
    scale = Dh ** -0.5
    qg = q.reshape(B, rows, GRID_W, H, Dh)
    kg = k.reshape(B, rows, GRID_W, H, Dh)
    vg = v.reshape(B, rows, GRID_W, H, Dh)
    cols = jnp.arange(GRID_W)
    col_start = jnp.clip(cols - kc // 2, 0, GRID_W - kc)
    col_idx = col_start[:, None] + jnp.arange(kc)[None, :]
    col_off = col_idx - cols[:, None] + (NA_KC - 1)
    rpb_c = rpb[:, :, col_off]

    def row_block(r):
        r0 = jnp.clip(r - NA_KR // 2, 0, rows - kr)
        qr = lax.dynamic_index_in_dim(qg, r, axis=1, keepdims=False)
        kb = lax.dynamic_slice_in_dim(kg, r0, kr, axis=1)[:, :, col_idx]
        vb = lax.dynamic_slice_in_dim(vg, r0, kr, axis=1)[:, :, col_idx]
        row_off = r0 + jnp.arange(kr) - r + (NA_KR - 1)
        bias = jnp.transpose(rpb_c[:, row_off], (0, 2, 1, 3))
        s_win = jnp.einsum('bwhd,biwjhd->bhwij', qr, kb) * scale + bias
        s_ctx = jnp.einsum('bwhd,bchd->bhwc', qr, k_ctx) * scale
        s = jnp.concatenate([s_win.reshape(B, H, GRID_W, n_win), s_ctx], axis=-1)
        p = jax.nn.softmax(s.astype(jnp.float32), axis=-1).astype(q.dtype)
        p_win = p[..., :n_win].reshape(B, H, GRID_W, kr, kc)
        return (jnp.einsum('bhwij,biwjhd->bwhd', p_win, vb)
                + jnp.einsum('bhwc,bchd->bwhd', p[..., n_win:], v_ctx))

    o = lax.map(row_block, jnp.arange(rows))
    return jnp.moveaxis(o, 0, 1).reshape(B, S, H * Dh)


def context_attention(q, k, v):
    B, L, H, Dh = q.shape
    s = jnp.einsum('bqhd,bkhd->bhqk', q, k) * (Dh ** -0.5)
    p = jax.nn.softmax(s.astype(jnp.float32), axis=-1).astype(q.dtype)
    return jnp.einsum('bhqk,bkhd->bqhd', p, v).reshape(B, L, H * Dh)


def s5_discretise(a_re, a_im, log_dt, b_re, b_im):
    lam = lax.complex(a_re.astype(jnp.float32), a_im.astype(jnp.float32))
    dt = jnp.exp(log_dt.astype(jnp.float32))[:, None]
    a_bar = jnp.exp(lam * dt)
    b = lax.complex(b_re.astype(jnp.float32), b_im.astype(jnp.float32))
    b_bar = ((a_bar - 1.0) / lam)[..., None] * b
    return a_bar, b_bar


def _linear_combine(left, right):
    a_l, b_l = left
    a_r, b_r = right
    return a_r * a_l, a_r * b_l + b_r


def s5_scan(u, a_bar, b_bar, h0, reverse):
    bu = jnp.einsum('gpc,blgc->blgp', b_bar, u.astype(jnp.complex64))
    if h0 is not None:
        edge = -1 if reverse else 0
        bu = bu.at[:, edge].add(a_bar * h0)
    a_seq = jnp.broadcast_to(a_bar, (1, u.shape[1]) + a_bar.shape)
    _, h = lax.associative_scan(_linear_combine, (a_seq, bu), axis=1, reverse=reverse)
    return h


def s5_readout(c_mat, h):
    return jnp.einsum('gcp,blgp->blgc', c_mat, h).real


def s5_glu(y, w_glu, b_glu):
    g = jax.nn.gelu(y)
    return g * jax.nn.sigmoid(g @ w_glu + b_glu)


def s5_mixer(u, u_c, a_re, a_im, log_dt, b_re, b_im, c_re, c_im, d, w_glu, b_glu, with_ctx_out):
    B, S, _ = u.shape
    L = u_c.shape[1]
    ug = u.reshape(B, S, S5_G, S5_CG)
    ucg = u_c.reshape(B, L, S5_G, S5_CG)
    y = d * ug
    y_c = d * ucg if with_ctx_out else None
    for direction in range(2):
        rev = direction == 1
        a_bar, b_bar = s5_discretise(a_re[direction], a_im[direction], log_dt[direction],
                                     b_re[direction], b_im[direction])
        c_mat = lax.complex(c_re[direction].astype(jnp.float32), c_im[direction].astype(jnp.float32))
        h_ctx = s5_scan(ucg, a_bar, b_bar, None, rev)
        h0 = h_ctx[:, 0] if rev else h_ctx[:, -1]
        h_lat = s5_scan(ug, a_bar, b_bar, h0, rev)
        y = y + s5_readout(c_mat, h_lat).astype(y.dtype)
        if with_ctx_out:
            y_c = y_c + s5_readout(c_mat, h_ctx).astype(y_c.dtype)
    y = s5_glu(y.reshape(B, S, S5_W), w_glu, b_glu)
    y_c = s5_glu(y_c.reshape(B, L, S5_W), w_glu, b_glu) if with_ctx_out else None
    return y, y_c


def parallel_mixer(h, h_c, w_in, rpb, a_re, a_im, log_dt, b_re, b_im, c_re, c_im, d,
                   w_glu, b_glu, w_out, with_ctx_out):
    B, S, _ = h.shape
    L = h_c.shape[1]
    splits = [NA_W, 2 * NA_W, 3 * NA_W]
    q, k, v, u = jnp.split(h @ w_in, splits, axis=-1)
    q_c, k_c, v_c, u_c = jnp.split(h_c @ w_in, splits, axis=-1)

    def heads(t):
        return t.reshape(t.shape[0], t.shape[1], NA_HEADS, NA_DH)

    y_na = neighborhood_attention(heads(q), heads(k), heads(v), heads(k_c), heads(v_c), rpb)
    y_s5, y_s5_c = s5_mixer(u, u_c, a_re, a_im, log_dt, b_re, b_im, c_re, c_im, d,
                            w_glu, b_glu, with_ctx_out)
    y = jnp.concatenate([y_na, y_s5], axis=-1) @ w_out
    if with_ctx_out:
        y_na_c = context_attention(heads(q_c), heads(k_c), heads(v_c))
        y_c = jnp.concatenate([y_na_c, y_s5_c], axis=-1) @ w_out
    else:
        y_c = None
    return y, y_c


def setup_inputs(seed: int = 0) -> dict:
    key = jax.random.key(seed)
    ks = jax.random.split(key, 26)
    f32 = jnp.float32

    def nrm(k, shape, s):
        return jax.random.normal(k, shape, f32) * s

    D = D_MODEL
    return {
        'x': nrm(ks[0], (BATCH, SEQ, D), 1.0),
        'c': nrm(ks[1], (BATCH, D), 1.0),
        'ctx': nrm(ks[2], (BATCH, CTX_LEN, D), 1.0),
        'c_ctx': nrm(ks[3], (D,), 1.0),
        'w_ada': nrm(ks[4], (DEPTH, D, 9 * D), 0.5 * D ** -0.5),
        'b_ada': nrm(ks[5], (DEPTH, 9 * D), 0.02),
        'ln_g': 1.0 + nrm(ks[6], (DEPTH, 3, D), 0.02),
        'ln_b': nrm(ks[7], (DEPTH, 3, D), 0.02),
        'ffn1_w_up': nrm(ks[8], (DEPTH, D, 2 * D_FF), D ** -0.5),
        'ffn1_w_down': nrm(ks[9], (DEPTH, D_FF, D), DN_BETA * D_FF ** -0.5),
        'w_in': nrm(ks[10], (DEPTH, D, IN_COLS), D ** -0.5),
        'na_rpb': nrm(ks[11], (DEPTH, NA_HEADS, 2 * NA_KR - 1, 2 * NA_KC - 1), 0.02),
        's5_a_re': -0.5 + nrm(ks[12], (DEPTH, 2, S5_G, S5_P), 0.01),
        's5_a_im': math.pi * jnp.arange(S5_P, dtype=f32) + nrm(ks[13], (DEPTH, 2, S5_G, S5_P), 0.01),
        's5_log_dt': jax.random.uniform(ks[14], (DEPTH, 2, S5_G), f32, math.log(DT_MIN), math.log(DT_MAX)),
        's5_b_re': nrm(ks[15], (DEPTH, 2, S5_G, S5_P, S5_CG), (2 * S5_CG) ** -0.5),
        's5_b_im': nrm(ks[16], (DEPTH, 2, S5_G, S5_P, S5_CG), (2 * S5_CG) ** -0.5),
        's5_c_re': nrm(ks[17], (DEPTH, 2, S5_G, S5_CG, S5_P), (2 * S5_P) ** -0.5),
        's5_c_im': nrm(ks[18], (DEPTH, 2, S5_G, S5_CG, S5_P), (2 * S5_P) ** -0.5),
        's5_d': nrm(ks[19], (DEPTH, S5_G, S5_CG), 1.0),
        's5_w_glu': nrm(ks[20], (DEPTH, S5_W, S5_W), S5_W ** -0.5),
        's5_b_glu': nrm(ks[21], (DEPTH, S5_W), 0.02),
        'w_out': nrm(ks[22], (DEPTH, MIX_W, D), DN_BETA * MIX_W ** -0.5),
        'ffn2_w_up': nrm(ks[23], (DEPTH, D, 2 * D_FF), D ** -0.5),
        'ffn2_w_down': nrm(ks[24], (DEPTH, D_FF, D), DN_BETA * D_FF ** -0.5),
    }


def reference(x, c, ctx, c_ctx, w_ada, b_ada, ln_g, ln_b, ffn1_w_up, ffn1_w_down, w_in, na_rpb,
              s5_a_re, s5_a_im, s5_log_dt, s5_b_re, s5_b_im, s5_c_re, s5_c_im, s5_d,
              s5_w_glu, s5_b_glu, w_out, ffn2_w_up, ffn2_w_down):
    h_c = ctx
    for layer in range(DEPTH):
        last = layer == DEPTH - 1
        mod = jnp.split((jax.nn.silu(c) @ w_ada[layer] + b_ada[layer])[:, None, :], 9, axis=-1)
        mod_c = jnp.split(jax.nn.silu(c_ctx) @ w_ada[layer] + b_ada[layer], 9, axis=-1)

        x = post_norm(x, 0.5 * mod[2] * swiglu(modulate(x, mod[0], mod[1]), ffn1_w_up[layer], ffn1_w_down[layer]),
                      ln_g[layer, 0], ln_b[layer, 0])
        h_c = post_norm(h_c, 0.5 * mod_c[2] * swiglu(modulate(h_c, mod_c[0], mod_c[1]),
                                                     ffn1_w_up[layer], ffn1_w_down[layer]),
                        ln_g[layer, 0], ln_b[layer, 0])

        y, y_c = parallel_mixer(modulate(x, mod[3], mod[4]), modulate(h_c, mod_c[3], mod_c[4]),
                                w_in[layer], na_rpb[layer], s5_a_re[layer], s5_a_im[layer], s5_log_dt[layer],
                                s5_b_re[layer], s5_b_im[layer], s5_c_re[layer], s5_c_im[layer], s5_d[layer],
                                s5_w_glu[layer], s5_b_glu[layer], w_out[layer], not last)
        x = post_norm(x, mod[5] * y, ln_g[layer, 1], ln_b[layer, 1])

        x = post_norm(x, 0.5 * mod[8] * swiglu(modulate(x, mod[6], mod[7]), ffn2_w_up[layer], ffn2_w_down[layer]),
                      ln_g[layer, 2], ln_b[layer, 2])
        if not last:
            h_c = post_norm(h_c, mod_c[5] * y_c, ln_g[layer, 1], ln_b[layer, 1])
            h_c = post_norm(h_c, 0.5 * mod_c[8] * swiglu(modulate(h_c, mod_c[6], mod_c[7]),
                                                         ffn2_w_up[layer], ffn2_w_down[layer]),
                            ln_g[layer, 2], ln_b[layer, 2])
    return x
```

```python
import functools

import jax
import jax.numpy as jnp
import numpy as np
from jax import lax
from jax.experimental import pallas as pl
from jax.experimental.pallas import tpu as pltpu

GRID_W = 64
NA_DH = 64
NA_KR = 8
NA_KC = 16
S5_CG = 16
S5_P = 64
LN_EPS = 1e-6
N_MOD = 9

CHUNK = 16
NA_QROWS = 4
NA_KROWS = 12
MASK_VALUE = -1e30

MXU_TILE = 256
VMEM_LIMIT_BYTES = 56 * 1024 * 1024

_MXU_DTYPE = jnp.bfloat16
_HI = lax.Precision.HIGHEST


def _mm(a, b):
    return jnp.dot(a, b, preferred_element_type=jnp.float32)


def _mm_nt(a, b):
    return lax.dot_general(a, b, (((1,), (1,)), ((), ())), preferred_element_type=jnp.float32)


def _const_spec(shape):
    nd = len(shape)
    return pl.BlockSpec(shape, lambda *_: (0,) * nd, pipeline_mode=pl.Buffered(1))


def _layer_norm(r, g, b):
    mu = jnp.mean(r, axis=-1, keepdims=True)
    d = r - mu
    var = jnp.mean(d * d, axis=-1, keepdims=True)
    return d * lax.rsqrt(var + LN_EPS) * g + b


def _swiglu(h, wup_ref, wdn_ref, p_s):
    ff = wdn_ref.shape[0]
    for j in range(ff // MXU_TILE):
        lo = j * MXU_TILE
        a = _mm(h, wup_ref[:, lo:lo + MXU_TILE])
        g = _mm(h, wup_ref[:, ff + lo:ff + lo + MXU_TILE])
        p_s[:, lo:lo + MXU_TILE] = (g * jax.nn.sigmoid(g) * a).astype(p_s.dtype)
    return _mm(p_s[...], wdn_ref[...])


def _ada_kernel(c_ref, w_ref, b_ref, o_ref):
    cc = c_ref[...]
    s = cc * jax.nn.sigmoid(cc)
    o_ref[...] = jnp.dot(s, w_ref[...], precision=_HI,
                         preferred_element_type=jnp.float32) + b_ref[...]


def _ada(cc, w, b, bn):
    rows, d = cc.shape
    n = w.shape[1]
    return pl.pallas_call(
        _ada_kernel,
        grid=(n // bn,),
        in_specs=[pl.BlockSpec((rows, d), lambda j: (0, 0)),
                  pl.BlockSpec((d, bn), lambda j: (0, j)),
                  pl.BlockSpec((1, bn), lambda j: (0, j))],
        out_specs=pl.BlockSpec((rows, bn), lambda j: (0, j)),
        out_shape=jax.ShapeDtypeStruct((rows, n), jnp.float32),
        compiler_params=pltpu.CompilerParams(dimension_semantics=("arbitrary",),
                                             vmem_limit_bytes=VMEM_LIMIT_BYTES),
        name="ada",
    )(cc, w, b)


def _ffn_in_kernel(alpha, na_w, x_ref, mod_ref, lng_ref, lnb_ref, wup_ref, wdn_ref, win_ref,
                   x1_ref, qkv_ref, u_ref, p_s):
    x = x_ref[...]
    mod = mod_ref[0]
    h = (x * (1.0 + mod[1:2]) + mod[0:1]).astype(p_s.dtype)
    y = _swiglu(h, wup_ref, wdn_ref, p_s)
    x1 = _layer_norm(alpha * x + (0.5 * mod[2:3]) * y, lng_ref[0:1], lnb_ref[0:1])
    x1_ref[...] = x1
    h2 = (x1 * (1.0 + mod[4:5]) + mod[3:4]).astype(p_s.dtype)
    pr = _mm(h2, win_ref[...])
    qkv_ref[:, :na_w] = (pr[:, :na_w] * (NA_DH ** -0.5)).astype(qkv_ref.dtype)
    qkv_ref[:, na_w:] = pr[:, na_w:3 * na_w].astype(qkv_ref.dtype)
    u_ref[...] = pr[:, 3 * na_w:]


def _ffn_in(x2d, mod, tiles_per_mod, lng, lnb, wup, wdn, win, alpha, na_w, tm):
    t, d = x2d.shape
    ff = wdn.shape[0]
    ncol = win.shape[1]
    s5_w = ncol - 3 * na_w
    if tiles_per_mod is None:
        mod_map = lambda i: (0, 0, 0)
    else:
        mod_map = lambda i: (i // tiles_per_mod, 0, 0)
    return pl.pallas_call(
        functools.partial(_ffn_in_kernel, alpha, na_w),
        grid=(t // tm,),
        in_specs=[pl.BlockSpec((tm, d), lambda i: (i, 0)),
                  pl.BlockSpec((1, N_MOD, d), mod_map),
                  _const_spec(lng.shape), _const_spec(lnb.shape),
                  _const_spec(wup.shape), _const_spec(wdn.shape), _const_spec(win.shape)],
        out_specs=[pl.BlockSpec((tm, d), lambda i: (i, 0)),
                   pl.BlockSpec((tm, 3 * na_w), lambda i: (i, 0)),
                   pl.BlockSpec((tm, s5_w), lambda i: (i, 0))],
        out_shape=[jax.ShapeDtypeStruct((t, d), jnp.float32),
                   jax.ShapeDtypeStruct((t, 3 * na_w), _MXU_DTYPE),
                   jax.ShapeDtypeStruct((t, s5_w), jnp.float32)],
        scratch_shapes=[pltpu.VMEM((tm, ff), _MXU_DTYPE)],
        compiler_params=pltpu.CompilerParams(dimension_semantics=("arbitrary",),
                                             vmem_limit_bytes=VMEM_LIMIT_BYTES),
        name="ffn_in",
    )(x2d, mod, lng, lnb, wup, wdn, win)


def _s5_tables(a_re, a_im, log_dt, b_re, b_im, c_re, c_im, d_skip):
    g, p = a_re.shape[1], a_re.shape[2]
    f32 = jnp.float32
    dt = jnp.exp(log_dt.astype(f32))[..., None]
    lr, li = a_re.astype(f32) * dt, a_im.astype(f32) * dt
    m = jnp.arange(CHUNK + 1, dtype=f32)[:, None, None, None]
    mag = jnp.exp(m * lr[None])
    pw_r, pw_i = mag * jnp.cos(m * li[None]), mag * jnp.sin(m * li[None])
    ab_r, ab_i = pw_r[1], pw_i[1]
    den = a_re.astype(f32) ** 2 + a_im.astype(f32) ** 2
    nr, ni = ab_r - 1.0, ab_i
    sr = (nr * a_re + ni * a_im) / den
    si = (ni * a_re - nr * a_im) / den
    bb_r = sr[..., None] * b_re - si[..., None] * b_im
    bb_i = sr[..., None] * b_im + si[..., None] * b_re

    pb_r = pw_r[..., None] * bb_r[None] - pw_i[..., None] * bb_i[None]
    pb_i = pw_r[..., None] * bb_i[None] + pw_i[..., None] * bb_r[None]
    pwc_r, pwc_i = pw_r[:, :, :, None, :], pw_i[:, :, :, None, :]
    pc_r = c_re[None] * pwc_r - c_im[None] * pwc_i
    pc_i = c_re[None] * pwc_i + c_im[None] * pwc_r

    kern = (jnp.einsum('mdgkp,dgpc->mdgkc', pc_r[:CHUNK], bb_r, precision=_HI)
            - jnp.einsum('mdgkp,dgpc->mdgkc', pc_i[:CHUNK], bb_i, precision=_HI))

    ii = np.arange(CHUNK)[:, None]
    jj = np.arange(CHUNK)[None, :]
    toep = jnp.zeros((g, CHUNK, S5_CG, CHUNK, S5_CG), f32)
    for direction, lag in ((0, jj - ii), (1, ii - jj)):
        valid = jnp.asarray(lag >= 0)
        k_ij = kern[np.clip(lag, 0, CHUNK - 1), direction]
        k_ij = jnp.where(valid[:, :, None, None, None], k_ij, 0.0)
        toep = toep + jnp.transpose(k_ij, (2, 0, 4, 1, 3))
    toep = toep.reshape(g, CHUNK * S5_CG, CHUNK * S5_CG)

    pow_in = (np.arange(CHUNK)[::-1].copy(), np.arange(CHUNK))
    pow_out = (np.arange(1, CHUNK + 1), np.arange(CHUNK, 0, -1))
    half = 2 * p
    bs = jnp.zeros((2, g, CHUNK * S5_CG, 2 * half), f32)
    cs = jnp.zeros((2, g, 2 * half, CHUNK * S5_CG), f32)
    slot = (np.arange(g) % 2) * p
    for direction in range(2):
        b_r = jnp.transpose(pb_r[pow_in[direction], direction], (1, 0, 3, 2))
        b_i = jnp.transpose(pb_i[pow_in[direction], direction], (1, 0, 3, 2))
        b_r = b_r.reshape(g, CHUNK * S5_CG, p)
        b_i = b_i.reshape(g, CHUNK * S5_CG, p)
        c_r = jnp.transpose(pc_r[pow_out[direction], direction], (1, 3, 0, 2))
        c_i = jnp.transpose(pc_i[pow_out[direction], direction], (1, 3, 0, 2))
        c_r = c_r.reshape(g, p, CHUNK * S5_CG)
        c_i = c_i.reshape(g, p, CHUNK * S5_CG)
        for par in range(2):
            sel = jnp.asarray((np.arange(g) % 2) == par)[:, None, None]
            lo = par * p
            bs = bs.at[direction, :, :, lo:lo + p].add(jnp.where(sel, b_r, 0.0))
            bs = bs.at[direction, :, :, half + lo:half + lo + p].add(jnp.where(sel, b_i, 0.0))
            cs = cs.at[direction, :, lo:lo + p, :].add(jnp.where(sel, c_r, 0.0))
            cs = cs.at[direction, :, half + lo:half + lo + p, :].add(jnp.where(sel, -c_i, 0.0))
    del slot

    a_pow = jnp.stack([pw_r[CHUNK].reshape(2, g // 2, half), pw_i[CHUNK].reshape(2, g // 2, half)], axis=2)
    dvec = jnp.tile(d_skip.astype(f32)[:, None, :], (1, CHUNK, 1)).reshape(g, 1, CHUNK * S5_CG)
    return toep.astype(_MXU_DTYPE), bs.astype(_MXU_DTYPE), cs.astype(_MXU_DTYPE), a_pow, dvec


def _s5_kernel(nb, xl_ref, xc_ref, toep_ref, bs_ref, cs_ref, apow_ref, dvec_ref, y_ref, z_s, h_s):
    rows_l = xl_ref.shape[1]
    rows_c = xc_ref.shape[1]
    n_l, n_c = rows_l // nb, rows_c // nb
    half = z_s.shape[2] // 2
    xl = [xl_ref[g] for g in range(2)]
    xlb = [v.astype(_MXU_DTYPE) for v in xl]
    xcb = [xc_ref[g].astype(_MXU_DTYPE) for g in range(2)]

    for d in range(2):
        z_s[d, :rows_c, :] = _mm(xcb[0], bs_ref[d, 0]) + _mm(xcb[1], bs_ref[d, 1])
        z_s[d, rows_c:, :] = _mm(xlb[0], bs_ref[d, 0]) + _mm(xlb[1], bs_ref[d, 1])

    a_r = [jnp.broadcast_to(apow_ref[d, 0, 0:1, :], (nb, half)) for d in range(2)]
    a_i = [jnp.broadcast_to(apow_ref[d, 0, 1:2, :], (nb, half)) for d in range(2)]

    def step(d, row, h_r, h_i, store):
        if store:
            h_s[d, pl.ds(row - rows_c, nb), :half] = h_r
            h_s[d, pl.ds(row - rows_c, nb), half:] = h_i
        z = z_s[d, pl.ds(row, nb), :]
        return (a_r[d] * h_r - a_i[d] * h_i + z[:, :half],
                a_r[d] * h_i + a_i[d] * h_r + z[:, half:])

    def ctx_body(n, carry):
        f_r, f_i, b_r, b_i = carry
        f_r, f_i = step(0, pl.multiple_of(n * nb, nb), f_r, f_i, False)
        b_r, b_i = step(1, pl.multiple_of((n_c - 1 - n) * nb, nb), b_r, b_i, False)
        return f_r, f_i, b_r, b_i

    def lat_body(n, carry):
        f_r, f_i, b_r, b_i = carry
        f_r, f_i = step(0, pl.multiple_of(rows_c + n * nb, nb), f_r, f_i, True)
        b_r, b_i = step(1, pl.multiple_of(rows_c + (n_l - 1 - n) * nb, nb), b_r, b_i, True)
        return f_r, f_i, b_r, b_i

    zero = jnp.zeros((nb, half), jnp.float32)
    carry = lax.fori_loop(0, n_c, ctx_body, (zero, zero, zero, zero))
    lax.fori_loop(0, n_l, lat_body, carry)

    hb = [h_s[d].astype(_MXU_DTYPE) for d in range(2)]
    for g in range(2):
        y = _mm(xlb[g], toep_ref[g]) + _mm(hb[0], cs_ref[0, g]) + _mm(hb[1], cs_ref[1, g])
        y_ref[g] = y + xl[g] * dvec_ref[g]


def _s5(xl, xc, toep, bs, cs, a_pow, dvec, nb):
    g, rows_l, w = xl.shape
    rows_c = xc.shape[1]
    sw = bs.shape[3]
    return pl.pallas_call(
        functools.partial(_s5_kernel, nb),
        grid=(g // 2,),
        in_specs=[pl.BlockSpec((2, rows_l, w), lambda k: (k, 0, 0)),
                  pl.BlockSpec((2, rows_c, w), lambda k: (k, 0, 0)),
                  pl.BlockSpec((2, w, w), lambda k: (k, 0, 0)),
                  pl.BlockSpec((2, 2, w, sw), lambda k: (0, k, 0, 0)),
                  pl.BlockSpec((2, 2, sw, w), lambda k: (0, k, 0, 0)),
                  pl.BlockSpec((2, 1, 2, sw // 2), lambda k: (0, k, 0, 0)),
                  pl.BlockSpec((2, 1, w), lambda k: (k, 0, 0))],
        out_specs=pl.BlockSpec((2, rows_l, w), lambda k: (k, 0, 0)),
        out_shape=jax.ShapeDtypeStruct((g, rows_l, w), jnp.float32),
        scratch_shapes=[pltpu.VMEM((2, rows_c + rows_l, sw), jnp.float32),
                        pltpu.VMEM((2, rows_l, sw), jnp.float32)],
        compiler_params=pltpu.CompilerParams(dimension_semantics=("arbitrary",),
                                             vmem_limit_bytes=VMEM_LIMIT_BYTES),
        name="s5",
    )(xl, xc, toep, bs, cs, a_pow, dvec)


def _na_bias(rpb, rows):
    lr = np.arange(NA_QROWS)[:, None, None, None]
    qc = np.arange(GRID_W)[None, :, None, None]
    kl = np.arange(NA_KROWS)[None, None, :, None]
    kc = np.arange(GRID_W)[None, None, None, :]
    c0 = np.clip(qc - NA_KC // 2, 0, GRID_W - NA_KC)
    col_ok = (kc >= c0) & (kc < c0 + NA_KC)
    col_off = np.clip(kc - qc + (NA_KC - 1), 0, 2 * NA_KC - 2)
    tables = []
    last_q = NA_KROWS - NA_QROWS
    for r0s, qrs in ((0 * lr, lr), (lr, NA_KR // 2 + lr), (0 * lr + last_q - NA_KR // 2, last_q + lr)):
        row_ok = (kl >= r0s) & (kl < r0s + NA_KR)
        row_off = np.clip(kl - qrs + (NA_KR - 1), 0, 2 * NA_KR - 2)
        ok = np.broadcast_to(row_ok & col_ok, (NA_QROWS, GRID_W, NA_KROWS, GRID_W))
        ro = np.broadcast_to(row_off, ok.shape).reshape(NA_QROWS * GRID_W, NA_KROWS * GRID_W)
        co = np.broadcast_to(col_off, ok.shape).reshape(NA_QROWS * GRID_W, NA_KROWS * GRID_W)
        vals = rpb[:, ro, co]
        tables.append(jnp.where(jnp.asarray(ok.reshape(ro.shape))[None], vals, MASK_VALUE))
    del rows
    return jnp.stack(tables).astype(jnp.float32)


def _na_kernel(rows, q_ref, k_ref, v_ref, kc_ref, vc_ref, bias_ref, o_ref):
    rb = pl.program_id(2)
    start = jnp.clip(rb * NA_QROWS - NA_KR // 2, 0, rows - NA_KROWS) * GRID_W
    start = pl.multiple_of(start, GRID_W)
    nk = NA_KROWS * GRID_W
    q2 = q_ref[0]
    kw = k_ref[0, pl.ds(start, nk), :]
    vw = v_ref[0, pl.ds(start, nk), :]
    kcx = kc_ref[0]
    vcx = vc_ref[0]
    lane = lax.broadcasted_iota(jnp.int32, (1, q2.shape[1]), 1)
    outs = []
    for hh in range(2):
        qm = jnp.where(lane // NA_DH == hh, q2, jnp.zeros_like(q2))
        s_w = _mm_nt(qm, kw) + bias_ref[0, hh]
        s_c = _mm_nt(qm, kcx)
        mx = jnp.maximum(jnp.max(s_w, axis=-1, keepdims=True), jnp.max(s_c, axis=-1, keepdims=True))
        p_w = jnp.exp(s_w - mx)
        p_c = jnp.exp(s_c - mx)
        denom = jnp.sum(p_w, axis=-1, keepdims=True) + jnp.sum(p_c, axis=-1, keepdims=True)
        o = _mm(p_w.astype(vw.dtype), vw) + _mm(p_c.astype(vcx.dtype), vcx)
        outs.append(o * (1.0 / denom))
    o_ref[0] = jnp.where(lane // NA_DH == 0, outs[0], outs[1]).astype(o_ref.dtype)


def _na(qkv, qkv_c, bias, na_w):
    b, s, _ = qkv.shape
    l = qkv_c.shape[1]
    rows = s // GRID_W
    n_rb = rows // NA_QROWS
    n_hp = na_w // (2 * NA_DH)
    tq = NA_QROWS * GRID_W
    lanes = 2 * NA_DH

    def bias_map(bi, hp, rb):
        kind = jnp.where(rb == 0, 0, jnp.where(rb == n_rb - 1, 2, 1))
        return (kind, hp, 0, 0)

    return pl.pallas_call(
        functools.partial(_na_kernel, rows),
        grid=(b, n_hp, n_rb),
        in_specs=[pl.BlockSpec((1, tq, lanes), lambda bi, hp, rb: (bi, rb, hp)),
                  pl.BlockSpec((1, s, lanes), lambda bi, hp, rb: (bi, 0, n_hp + hp)),
                  pl.BlockSpec((1, s, lanes), lambda bi, hp, rb: (bi, 0, 2 * n_hp + hp)),
                  pl.BlockSpec((1, l, lanes), lambda bi, hp, rb: (bi, 0, n_hp + hp)),
                  pl.BlockSpec((1, l, lanes), lambda bi, hp, rb: (bi, 0, 2 * n_hp + hp)),
                  pl.BlockSpec((1, 2, tq, NA_KROWS * GRID_W), bias_map)],
        out_specs=pl.BlockSpec((1, tq, lanes), lambda bi, hp, rb: (bi, rb, hp)),
        out_shape=jax.ShapeDtypeStruct((b, s, na_w), _MXU_DTYPE),
        compiler_params=pltpu.CompilerParams(
            dimension_semantics=("arbitrary", "arbitrary", "arbitrary"),
            vmem_limit_bytes=VMEM_LIMIT_BYTES),
        name="na",
    )(qkv, qkv, qkv, qkv_c, qkv_c, bias)


def _out_ffn_kernel(alpha, x1_ref, yna_ref, ys5_ref, mod_ref, lng_ref, lnb_ref, wglu_ref, bglu_ref,
                    wout_ref, wup_ref, wdn_ref, o_ref, p_s):
    mod = mod_ref[0]
    na_w = yna_ref.shape[1]
    gl = jax.nn.gelu(ys5_ref[...])
    gate = jax.nn.sigmoid(_mm(gl.astype(p_s.dtype), wglu_ref[...]) + bglu_ref[...])
    glu = (gl * gate).astype(p_s.dtype)
    y = _mm(yna_ref[...], wout_ref[:na_w, :]) + _mm(glu, wout_ref[na_w:, :])
    x2 = _layer_norm(alpha * x1_ref[...] + mod[5:6] * y, lng_ref[1:2], lnb_ref[1:2])
    h = (x2 * (1.0 + mod[7:8]) + mod[6:7]).astype(p_s.dtype)
    f = _swiglu(h, wup_ref, wdn_ref, p_s)
    o_ref[...] = _layer_norm(alpha * x2 + (0.5 * mod[8:9]) * f, lng_ref[2:3], lnb_ref[2:3])


def _out_ffn(x1, yna, ys5, mod, tiles_per_mod, lng, lnb, wglu, bglu, wout, wup, wdn, alpha, tm):
    t, d = x1.shape
    ff = wdn.shape[0]
    na_w = yna.shape[1]
    s5_w = ys5.shape[1]
    return pl.pallas_call(
        functools.partial(_out_ffn_kernel, alpha),
        grid=(t // tm,),
        in_specs=[pl.BlockSpec((tm, d), lambda i: (i, 0)),
                  pl.BlockSpec((tm, na_w), lambda i: (i, 0)),
                  pl.BlockSpec((tm, s5_w), lambda i: (i, 0)),
                  pl.BlockSpec((1, N_MOD, d), lambda i: (i // tiles_per_mod, 0, 0)),
                  _const_spec(lng.shape), _const_spec(lnb.shape),
                  _const_spec(wglu.shape), _const_spec(bglu.shape), _const_spec(wout.shape),
                  _const_spec(wup.shape), _const_spec(wdn.shape)],
        out_specs=pl.BlockSpec((tm, d), lambda i: (i, 0)),
        out_shape=jax.ShapeDtypeStruct((t, d), jnp.float32),
        scratch_shapes=[pltpu.VMEM((tm, ff), _MXU_DTYPE)],
        compiler_params=pltpu.CompilerParams(dimension_semantics=("arbitrary",),
                                             vmem_limit_bytes=VMEM_LIMIT_BYTES),
        name="out_ffn",
    )(x1, yna, ys5, mod, lng, lnb, wglu, bglu, wout, wup, wdn)


def _pick_tile(n, target):
    t = min(n, target)
    while n % t:
        t //= 2
    return t


def _to_chunks(u, g):
    b, t, _ = u.shape
    v = u.reshape(b, t // CHUNK, CHUNK, g, S5_CG)
    return jnp.transpose(v, (3, 1, 0, 2, 4)).reshape(g, (t // CHUNK) * b, CHUNK * S5_CG)


def _from_chunks(y, b):
    g, r, _ = y.shape
    n = r // b
    v = y.reshape(g, n, b, CHUNK, S5_CG)
    return jnp.transpose(v, (2, 1, 3, 0, 4)).reshape(b, n * CHUNK, g * S5_CG)


def kernel(x, c, ctx, c_ctx, w_ada, b_ada, ln_g, ln_b, ffn1_w_up, ffn1_w_down, w_in, na_rpb, s5_a_re, s5_a_im, s5_log_dt, s5_b_re, s5_b_im, s5_c_re, s5_c_im, s5_d, s5_w_glu, s5_b_glu, w_out, ffn2_w_up, ffn2_w_down):
    depth = w_ada.shape[0]
    assert depth == 1, "single-layer stack only"
    bsz, seq, d = x.shape
    l_ctx = ctx.shape[1]
    s5_w = s5_w_glu.shape[1]
    na_w = w_out.shape[1] - s5_w
    groups = s5_w // S5_CG
    rows = seq // GRID_W
    assert bsz % 8 == 0 and seq % (GRID_W * NA_QROWS) == 0 and rows >= NA_KROWS
    assert seq % CHUNK == 0 and l_ctx % CHUNK == 0 and groups % 2 == 0
    assert na_w % (2 * NA_DH) == 0 and ffn1_w_down.shape[1] % MXU_TILE == 0
    alpha = (2.0 * depth) ** 0.25
    cd = _MXU_DTYPE

    n_rows = -(-(bsz + 1) // 8) * 8
    cc = jnp.zeros((n_rows, d), jnp.float32).at[:bsz].set(c).at[bsz].set(c_ctx)
    n_ada = w_ada.shape[2]
    mod = _ada(cc, w_ada[0], b_ada[0][None, :], _pick_tile(n_ada, 1152)).reshape(n_rows, N_MOD, d)

    lng, lnb = ln_g[0], ln_b[0]
    wup1, wdn1 = ffn1_w_up[0].astype(cd), ffn1_w_down[0].astype(cd)
    wup2, wdn2 = ffn2_w_up[0].astype(cd), ffn2_w_down[0].astype(cd)
    win = w_in[0].astype(cd)

    tm = _pick_tile(seq, 512)
    x1, qkv, u = _ffn_in(x.reshape(bsz * seq, d), mod, seq // tm, lng, lnb, wup1, wdn1, win, alpha, na_w, tm)
    tmc = _pick_tile(bsz * l_ctx, 512)
    _, qkv_c, u_c = _ffn_in(ctx.reshape(bsz * l_ctx, d), mod[bsz:bsz + 1], None, lng, lnb, wup1, wdn1, win,
                            alpha, na_w, tmc)

    tables = _s5_tables(s5_a_re[0], s5_a_im[0], s5_log_dt[0], s5_b_re[0], s5_b_im[0], s5_c_re[0], s5_c_im[0],
                        s5_d[0])
    y_chunks = _s5(_to_chunks(u.reshape(bsz, seq, s5_w), groups),
                   _to_chunks(u_c.reshape(bsz, l_ctx, s5_w), groups), *tables, bsz)
    y_s5 = _from_chunks(y_chunks, bsz).reshape(bsz * seq, s5_w)

    y_na = _na(qkv.reshape(bsz, seq, 3 * na_w), qkv_c.reshape(bsz, l_ctx, 3 * na_w), _na_bias(na_rpb[0], rows),
               na_w).reshape(bsz * seq, na_w)

    out = _out_ffn(x1, y_na, y_s5, mod, seq // tm, lng, lnb, s5_w_glu[0].astype(cd), s5_b_glu[0][None, :],
                   w_out[0].astype(cd), wup2, wdn2, alpha, tm)
    return out.reshape(bsz, seq, d)
```

```python
import functools

import jax
import jax.numpy as jnp
import numpy as np
from jax import lax
from jax.experimental import pallas as pl
from jax.experimental.pallas import tpu as pltpu

GRID_W = 64
NA_DH = 64
NA_KR = 8
NA_KC = 16
S5_CG = 16
S5_P = 64
LN_EPS = 1e-6
N_MOD = 9

CHUNK = 16
NA_QROWS = 4
NA_KROWS = 12
MASK_VALUE = -1e30

MXU_TILE = 256
LANES = 128
SUBLANES = 8
VMEM_LIMIT_BYTES = 56 * 1024 * 1024

_MXU_DTYPE = jnp.bfloat16
_HI = lax.Precision.HIGHEST


def _mm(a, b):
    return jnp.dot(a, b, preferred_element_type=jnp.float32)


def _mm_nt(a, b):
    return lax.dot_general(a, b, (((1,), (1,)), ((), ())), preferred_element_type=jnp.float32)


def _const_spec(shape):
    nd = len(shape)
    return pl.BlockSpec(shape, lambda *_: (0,) * nd, pipeline_mode=pl.Buffered(1))


def _layer_norm(r, g, b):
    mu = jnp.mean(r, axis=-1, keepdims=True)
    d = r - mu
    var = jnp.mean(d * d, axis=-1, keepdims=True)
    return d * lax.rsqrt(var + LN_EPS) * g + b


def _swiglu(h, wup_ref, wdn_ref, p_s):
    ff = wdn_ref.shape[0]
    for j in range(ff // MXU_TILE):
        lo = j * MXU_TILE
        a = _mm(h, wup_ref[:, lo:lo + MXU_TILE])
        g = _mm(h, wup_ref[:, ff + lo:ff + lo + MXU_TILE])
        p_s[:, lo:lo + MXU_TILE] = (g * jax.nn.sigmoid(g) * a).astype(p_s.dtype)
    return _mm(p_s[...], wdn_ref[...])


def _tokens_to_chunks(u_s, xs_ref):
    n_ch = u_s.shape[1] // CHUNK
    per = LANES // S5_CG
    blk = lax.broadcasted_iota(jnp.int32, (1, LANES), 1) // S5_CG
    for v in range(u_s.shape[0]):
        for w in range(CHUNK // per):
            srcs = [u_s[v, pl.ds(per * w + jj, n_ch, stride=CHUNK), :] for jj in range(per)]
            for gg in range(per):
                acc = None
                for jj in range(per):
                    k = (jj - gg) % per
                    r = srcs[jj] if k == 0 else pltpu.roll(srcs[jj], S5_CG * k, axis=1)
                    acc = r if acc is None else jnp.where(blk == jj, r, acc)
                xs_ref[per * v + gg, :, w * LANES:(w + 1) * LANES] = acc.astype(xs_ref.dtype)


def _chunks_to_tokens(ys_ref, y_s):
    n_ch = y_s.shape[1] // CHUNK
    per = LANES // S5_CG
    blk = lax.broadcasted_iota(jnp.int32, (1, LANES), 1) // S5_CG
    for v in range(y_s.shape[0]):
        for w in range(CHUNK // per):
            srcs = [ys_ref[per * v + gg, :, w * LANES:(w + 1) * LANES] for gg in range(per)]
            for jj in range(per):
                acc = None
                for gg in range(per):
                    k = (gg - jj) % per
                    r = srcs[gg] if k == 0 else pltpu.roll(srcs[gg], S5_CG * k, axis=1)
                    acc = r if acc is None else jnp.where(blk == gg, r, acc)
                y_s[v, pl.ds(per * w + jj, n_ch, stride=CHUNK), :] = acc


def _ada_kernel(c_ref, w_ref, b_ref, o_ref):
    cc = c_ref[...]
    s = cc * jax.nn.sigmoid(cc)
    o_ref[...] = jnp.dot(s, w_ref[...], precision=_HI,
                         preferred_element_type=jnp.float32) + b_ref[...]


def _ada(cc, w, b, bn):
    rows, d = cc.shape
    n = w.shape[1]
    return pl.pallas_call(
        _ada_kernel,
        grid=(n // bn,),
        in_specs=[pl.BlockSpec((rows, d), lambda j: (0, 0)),
                  pl.BlockSpec((d, bn), lambda j: (0, j)),
                  pl.BlockSpec((1, bn), lambda j: (0, j))],
        out_specs=pl.BlockSpec((rows, bn), lambda j: (0, j)),
        out_shape=jax.ShapeDtypeStruct((rows, n), jnp.float32),
        compiler_params=pltpu.CompilerParams(dimension_semantics=("arbitrary",),
                                             vmem_limit_bytes=VMEM_LIMIT_BYTES),
        name="ada",
    )(cc, w, b)


def _ffn_in_kernel(alpha, na_w, x_ref, mod_ref, lng_ref, lnb_ref, wup_ref, wdn_ref, win_ref,
                   x1_ref, qkv_ref, xs_ref, p_s, u_s):
    x = x_ref[...]
    mod = mod_ref[0]
    h = (x * (1.0 + mod[1:2]) + mod[0:1]).astype(p_s.dtype)
    y = _swiglu(h, wup_ref, wdn_ref, p_s)
    x1 = _layer_norm(alpha * x + (0.5 * mod[2:3]) * y, lng_ref[0:1], lnb_ref[0:1])
    x1_ref[...] = x1
    h2 = (x1 * (1.0 + mod[4:5]) + mod[3:4]).astype(p_s.dtype)
    pr = _mm(h2, win_ref[...])
    qkv_ref[:, :na_w] = (pr[:, :na_w] * (NA_DH ** -0.5)).astype(qkv_ref.dtype)
    qkv_ref[:, na_w:] = pr[:, na_w:3 * na_w].astype(qkv_ref.dtype)
    for v in range(u_s.shape[0]):
        u_s[v] = pr[:, 3 * na_w + v * LANES:3 * na_w + (v + 1) * LANES]
    _tokens_to_chunks(u_s, xs_ref)


def _ffn_in(x2d, mod, tiles_per_mod, lng, lnb, wup, wdn, win, alpha, na_w, tm):
    t, d = x2d.shape
    ff = wdn.shape[0]
    ncol = win.shape[1]
    s5_w = ncol - 3 * na_w
    groups, cw = s5_w // S5_CG, CHUNK * S5_CG
    if tiles_per_mod is None:
        mod_map = lambda i: (0, 0, 0)
    else:
        mod_map = lambda i: (i // tiles_per_mod, 0, 0)
    return pl.pallas_call(
        functools.partial(_ffn_in_kernel, alpha, na_w),
        grid=(t // tm,),
        in_specs=[pl.BlockSpec((tm, d), lambda i: (i, 0)),
                  pl.BlockSpec((1, N_MOD, d), mod_map),
                  _const_spec(lng.shape), _const_spec(lnb.shape),
                  _const_spec(wup.shape), _const_spec(wdn.shape), _const_spec(win.shape)],
        out_specs=[pl.BlockSpec((tm, d), lambda i: (i, 0)),
                   pl.BlockSpec((tm, 3 * na_w), lambda i: (i, 0)),
                   pl.BlockSpec((groups, tm // CHUNK, cw), lambda i: (0, i, 0))],
        out_shape=[jax.ShapeDtypeStruct((t, d), jnp.float32),
                   jax.ShapeDtypeStruct((t, 3 * na_w), _MXU_DTYPE),
                   jax.ShapeDtypeStruct((groups, t // CHUNK, cw), jnp.float32)],
        scratch_shapes=[pltpu.VMEM((tm, ff), _MXU_DTYPE), pltpu.VMEM((s5_w // LANES, tm, LANES), jnp.float32)],
        compiler_params=pltpu.CompilerParams(dimension_semantics=("arbitrary",),
                                             vmem_limit_bytes=VMEM_LIMIT_BYTES),
        name="ffn_in",
    )(x2d, mod, lng, lnb, wup, wdn, win)


def _s5_tables(a_re, a_im, log_dt, b_re, b_im, c_re, c_im, d_skip):
    g, p = a_re.shape[1], a_re.shape[2]
    f32 = jnp.float32
    dt = jnp.exp(log_dt.astype(f32))[..., None]
    lr, li = a_re.astype(f32) * dt, a_im.astype(f32) * dt
    m = jnp.arange(CHUNK + 1, dtype=f32)[:, None, None, None]
    mag = jnp.exp(m * lr[None])
    pw_r, pw_i = mag * jnp.cos(m * li[None]), mag * jnp.sin(m * li[None])
    ab_r, ab_i = pw_r[1], pw_i[1]
    den = a_re.astype(f32) ** 2 + a_im.astype(f32) ** 2
    nr, ni = ab_r - 1.0, ab_i
    sr = (nr * a_re + ni * a_im) / den
    si = (ni * a_re - nr * a_im) / den
    bb_r = sr[..., None] * b_re - si[..., None] * b_im
    bb_i = sr[..., None] * b_im + si[..., None] * b_re

    pb_r = pw_r[..., None] * bb_r[None] - pw_i[..., None] * bb_i[None]
    pb_i = pw_r[..., None] * bb_i[None] + pw_i[..., None] * bb_r[None]
    pwc_r, pwc_i = pw_r[:, :, :, None, :], pw_i[:, :, :, None, :]
    pc_r = c_re[None] * pwc_r - c_im[None] * pwc_i
    pc_i = c_re[None] * pwc_i + c_im[None] * pwc_r

    kern = (jnp.einsum('mdgkp,dgpc->mdgkc', pc_r[:CHUNK], bb_r, precision=_HI)
            - jnp.einsum('mdgkp,dgpc->mdgkc', pc_i[:CHUNK], bb_i, precision=_HI))

    ii = np.arange(CHUNK)[:, None]
    jj = np.arange(CHUNK)[None, :]
    toep = jnp.zeros((g, CHUNK, S5_CG, CHUNK, S5_CG), f32)
    for direction, lag in ((0, jj - ii), (1, ii - jj)):
        valid = jnp.asarray(lag >= 0)
        k_ij = kern[np.clip(lag, 0, CHUNK - 1), direction]
        k_ij = jnp.where(valid[:, :, None, None, None], k_ij, 0.0)
        toep = toep + jnp.transpose(k_ij, (2, 0, 4, 1, 3))
    toep = toep.reshape(g, CHUNK * S5_CG, CHUNK * S5_CG)

    pow_in = (np.arange(CHUNK)[::-1].copy(), np.arange(CHUNK))
    pow_out = (np.arange(1, CHUNK + 1), np.arange(CHUNK, 0, -1))
    half = 2 * p
    even = jnp.asarray((np.arange(g) % 2) == 0)[:, None, None]
    bs, cs = [], []
    for direction in range(2):
        b_r = jnp.transpose(pb_r[pow_in[direction], direction], (1, 0, 3, 2))
        b_i = jnp.transpose(pb_i[pow_in[direction], direction], (1, 0, 3, 2))
        b_r = b_r.reshape(g, CHUNK * S5_CG, p)
        b_i = b_i.reshape(g, CHUNK * S5_CG, p)
        c_r = jnp.transpose(pc_r[pow_out[direction], direction], (1, 3, 0, 2))
        c_i = jnp.transpose(pc_i[pow_out[direction], direction], (1, 3, 0, 2))
        c_r = c_r.reshape(g, p, CHUNK * S5_CG)
        c_i = -c_i.reshape(g, p, CHUNK * S5_CG)
        bs.append(jnp.concatenate([jnp.where(even, b_r, 0.0), jnp.where(even, 0.0, b_r),
                                   jnp.where(even, b_i, 0.0), jnp.where(even, 0.0, b_i)], axis=2))
        cs.append(jnp.concatenate([jnp.where(even, c_r, 0.0), jnp.where(even, 0.0, c_r),
                                   jnp.where(even, c_i, 0.0), jnp.where(even, 0.0, c_i)], axis=1))
    bs, cs = jnp.stack(bs), jnp.stack(cs)

    a_pow = jnp.stack([pw_r[CHUNK].reshape(2, g // 2, half), pw_i[CHUNK].reshape(2, g // 2, half)], axis=2)
    dvec = jnp.tile(d_skip.astype(f32)[:, None, :], (1, CHUNK, 1)).reshape(g, 1, CHUNK * S5_CG)
    return toep.astype(_MXU_DTYPE), bs.astype(_MXU_DTYPE), cs.astype(_MXU_DTYPE), a_pow, dvec


def _s5_kernel(nb, xl_ref, xc_ref, toep_ref, bs_ref, cs_ref, apow_ref, dvec_ref, y_ref, z_s, h_s):
    rows_l = xl_ref.shape[1]
    rows_c = xc_ref.shape[1]
    n_l, n_c = rows_l // nb, rows_c // nb
    half = z_s.shape[3]
    xl = [xl_ref[g] for g in range(2)]
    xlb = [v.astype(_MXU_DTYPE) for v in xl]
    xcb = [xc_ref[g].astype(_MXU_DTYPE) for g in range(2)]

    for d in range(2):
        zc = _mm(xcb[0], bs_ref[d, 0]) + _mm(xcb[1], bs_ref[d, 1])
        zl = _mm(xlb[0], bs_ref[d, 0]) + _mm(xlb[1], bs_ref[d, 1])
        for ri in range(2):
            z_s[d, ri, :rows_c, :] = zc[:, ri * half:(ri + 1) * half]
            z_s[d, ri, rows_c:, :] = zl[:, ri * half:(ri + 1) * half]

    a_r = [jnp.broadcast_to(apow_ref[d, 0, 0:1, :], (nb, half)) for d in range(2)]
    a_i = [jnp.broadcast_to(apow_ref[d, 0, 1:2, :], (nb, half)) for d in range(2)]

    def step(d, n, latent, h_r, h_i):
        if latent:
            h_s[d, 0, pl.ds(n, nb, stride=n_l), :] = h_r
            h_s[d, 1, pl.ds(n, nb, stride=n_l), :] = h_i
            z_r = z_s[d, 0, pl.ds(rows_c + n, nb, stride=n_l), :]
            z_i = z_s[d, 1, pl.ds(rows_c + n, nb, stride=n_l), :]
        else:
            z_r = z_s[d, 0, pl.ds(n, nb, stride=n_c), :]
            z_i = z_s[d, 1, pl.ds(n, nb, stride=n_c), :]
        return (a_r[d] * h_r - a_i[d] * h_i + z_r,
                a_r[d] * h_i + a_i[d] * h_r + z_i)

    def ctx_body(n, carry):
        f_r, f_i, b_r, b_i = carry
        f_r, f_i = step(0, n, False, f_r, f_i)
        b_r, b_i = step(1, n_c - 1 - n, False, b_r, b_i)
        return f_r, f_i, b_r, b_i

    def lat_body(n, carry):
        f_r, f_i, b_r, b_i = carry
        f_r, f_i = step(0, n, True, f_r, f_i)
        b_r, b_i = step(1, n_l - 1 - n, True, b_r, b_i)
        return f_r, f_i, b_r, b_i

    zero = jnp.zeros((nb, half), jnp.float32)
    carry = lax.fori_loop(0, n_c, ctx_body, (zero, zero, zero, zero))
    lax.fori_loop(0, n_l, lat_body, carry)

    hb = [jnp.concatenate([h_s[d, 0], h_s[d, 1]], axis=1).astype(_MXU_DTYPE) for d in range(2)]
    for g in range(2):
        y = _mm(xlb[g], toep_ref[g]) + _mm(hb[0], cs_ref[0, g]) + _mm(hb[1], cs_ref[1, g])
        y_ref[g] = y + xl[g] * dvec_ref[g]


def _s5(xl, xc, toep, bs, cs, a_pow, dvec, nb):
    g, rows_l, w = xl.shape
    rows_c = xc.shape[1]
    sw = bs.shape[3]
    return pl.pallas_call(
        functools.partial(_s5_kernel, nb),
        grid=(g // 2,),
        in_specs=[pl.BlockSpec((2, rows_l, w), lambda k: (k, 0, 0)),
                  pl.BlockSpec((2, rows_c, w), lambda k: (k, 0, 0)),
                  pl.BlockSpec((2, w, w), lambda k: (k, 0, 0)),
                  pl.BlockSpec((2, 2, w, sw), lambda k: (0, k, 0, 0)),
                  pl.BlockSpec((2, 2, sw, w), lambda k: (0, k, 0, 0)),
                  pl.BlockSpec((2, 1, 2, sw // 2), lambda k: (0, k, 0, 0)),
                  pl.BlockSpec((2, 1, w), lambda k: (k, 0, 0))],
        out_specs=pl.BlockSpec((2, rows_l, w), lambda k: (k, 0, 0)),
        out_shape=jax.ShapeDtypeStruct((g, rows_l, w), jnp.float32),
        scratch_shapes=[pltpu.VMEM((2, 2, rows_c + rows_l, sw // 2), jnp.float32),
                        pltpu.VMEM((2, 2, rows_l, sw // 2), jnp.float32)],
        compiler_params=pltpu.CompilerParams(dimension_semantics=("arbitrary",),
                                             vmem_limit_bytes=VMEM_LIMIT_BYTES),
        name="s5",
    )(xl, xc, toep, bs, cs, a_pow, dvec)


def _na_bias(rpb, rows):
    lr = np.arange(NA_QROWS)[:, None, None, None]
    qc = np.arange(GRID_W)[None, :, None, None]
    kl = np.arange(NA_KROWS)[None, None, :, None]
    kc = np.arange(GRID_W)[None, None, None, :]
    c0 = np.clip(qc - NA_KC // 2, 0, GRID_W - NA_KC)
    col_ok = (kc >= c0) & (kc < c0 + NA_KC)
    col_off = kc - qc + (NA_KC - 1)
    sel_c = (col_off[0, :, 0, :, None] == np.arange(2 * NA_KC - 1)) & col_ok[0, :, 0, :, None]
    sel_r, ok = [], []
    last_q = NA_KROWS - NA_QROWS
    for r0s, qrs in ((0 * lr, lr), (lr, NA_KR // 2 + lr), (0 * lr + last_q - NA_KR // 2, last_q + lr)):
        row_ok = (kl >= r0s) & (kl < r0s + NA_KR)
        row_off = kl - qrs + (NA_KR - 1)
        sel_r.append((row_off[:, 0, :, 0, None] == np.arange(2 * NA_KR - 1)) & row_ok[:, 0, :, 0, None])
        ok.append(np.broadcast_to(row_ok & col_ok, (NA_QROWS, GRID_W, NA_KROWS, GRID_W)))
    del rows
    sel_r = jnp.asarray(np.stack(sel_r), jnp.float32)
    by_col = jnp.einsum('hab,qkb->haqk', rpb.astype(jnp.float32), jnp.asarray(sel_c, jnp.float32), precision=_HI)
    bias = jnp.einsum('tlma,haqk->thlqmk', sel_r, by_col, precision=_HI)
    ok = jnp.asarray(np.stack(ok))[:, None]
    nq, nk = NA_QROWS * GRID_W, NA_KROWS * GRID_W
    return jnp.where(ok, bias, MASK_VALUE).reshape(3, rpb.shape[0], nq, nk)


def _na_kernel(rows, q_ref, k_ref, v_ref, kc_ref, vc_ref, bias_ref, o_ref):
    rb = pl.program_id(2)
    start = jnp.clip(rb * NA_QROWS - NA_KR // 2, 0, rows - NA_KROWS) * GRID_W
    start = pl.multiple_of(start, GRID_W)
    nk = NA_KROWS * GRID_W
    q2 = q_ref[0]
    kw = k_ref[0, pl.ds(start, nk), :]
    vw = v_ref[0, pl.ds(start, nk), :]
    kcx = kc_ref[0]
    vcx = vc_ref[0]
    lane = lax.broadcasted_iota(jnp.int32, (1, q2.shape[1]), 1)
    outs = []
    for hh in range(2):
        qm = jnp.where(lane // NA_DH == hh, q2, jnp.zeros_like(q2))
        s_w = _mm_nt(qm, kw) + bias_ref[0, hh]
        s_c = _mm_nt(qm, kcx)
        mx = jnp.maximum(jnp.max(s_w, axis=-1, keepdims=True), jnp.max(s_c, axis=-1, keepdims=True))
        p_w = jnp.exp(s_w - mx)
        p_c = jnp.exp(s_c - mx)
        denom = jnp.sum(p_w, axis=-1, keepdims=True) + jnp.sum(p_c, axis=-1, keepdims=True)
        o = _mm(p_w.astype(vw.dtype), vw) + _mm(p_c.astype(vcx.dtype), vcx)
        outs.append(o * (1.0 / denom))
    o_ref[0] = jnp.where(lane // NA_DH == 0, outs[0], outs[1]).astype(o_ref.dtype)


def _na(qkv, qkv_c, bias, na_w):
    b, s, _ = qkv.shape
    l = qkv_c.shape[1]
    rows = s // GRID_W
    n_rb = rows // NA_QROWS
    n_hp = na_w // (2 * NA_DH)
    tq = NA_QROWS * GRID_W
    lanes = 2 * NA_DH

    def bias_map(bi, hp, rb):
        kind = jnp.where(rb == 0, 0, jnp.where(rb == n_rb - 1, 2, 1))
        return (kind, hp, 0, 0)

    return pl.pallas_call(
        functools.partial(_na_kernel, rows),
        grid=(b, n_hp, n_rb),
        in_specs=[pl.BlockSpec((1, tq, lanes), lambda bi, hp, rb: (bi, rb, hp)),
                  pl.BlockSpec((1, s, lanes), lambda bi, hp, rb: (bi, 0, n_hp + hp)),
                  pl.BlockSpec((1, s, lanes), lambda bi, hp, rb: (bi, 0, 2 * n_hp + hp)),
                  pl.BlockSpec((1, l, lanes), lambda bi, hp, rb: (bi, 0, n_hp + hp)),
                  pl.BlockSpec((1, l, lanes), lambda bi, hp, rb: (bi, 0, 2 * n_hp + hp)),
                  pl.BlockSpec((1, 2, tq, NA_KROWS * GRID_W), bias_map)],
        out_specs=pl.BlockSpec((1, tq, lanes), lambda bi, hp, rb: (bi, rb, hp)),
        out_shape=jax.ShapeDtypeStruct((b, s, na_w), _MXU_DTYPE),
        compiler_params=pltpu.CompilerParams(
            dimension_semantics=("arbitrary", "arbitrary", "arbitrary"),
            vmem_limit_bytes=VMEM_LIMIT_BYTES),
        name="na",
    )(qkv, qkv, qkv, qkv_c, qkv_c, bias)


def _out_ffn_kernel(alpha, x1_ref, yna_ref, ys5_ref, mod_ref, lng_ref, lnb_ref, wglu_ref, bglu_ref,
                    wout_ref, wup_ref, wdn_ref, o_ref, p_s, y_s):
    mod = mod_ref[0]
    na_w = yna_ref.shape[1]
    _chunks_to_tokens(ys5_ref, y_s)
    gl = jax.nn.gelu(jnp.concatenate([y_s[v] for v in range(y_s.shape[0])], axis=1))
    gate = jax.nn.sigmoid(_mm(gl.astype(p_s.dtype), wglu_ref[...]) + bglu_ref[...])
    glu = (gl * gate).astype(p_s.dtype)
    y = _mm(yna_ref[...], wout_ref[:na_w, :]) + _mm(glu, wout_ref[na_w:, :])
    x2 = _layer_norm(alpha * x1_ref[...] + mod[5:6] * y, lng_ref[1:2], lnb_ref[1:2])
    h = (x2 * (1.0 + mod[7:8]) + mod[6:7]).astype(p_s.dtype)
    f = _swiglu(h, wup_ref, wdn_ref, p_s)
    o_ref[...] = _layer_norm(alpha * x2 + (0.5 * mod[8:9]) * f, lng_ref[2:3], lnb_ref[2:3])


def _out_ffn(x1, yna, ys5, mod, tiles_per_mod, lng, lnb, wglu, bglu, wout, wup, wdn, alpha, tm):
    t, d = x1.shape
    ff = wdn.shape[0]
    na_w = yna.shape[1]
    groups, _, cw = ys5.shape
    s5_w = groups * S5_CG
    return pl.pallas_call(
        functools.partial(_out_ffn_kernel, alpha),
        grid=(t // tm,),
        in_specs=[pl.BlockSpec((tm, d), lambda i: (i, 0)),
                  pl.BlockSpec((tm, na_w), lambda i: (i, 0)),
                  pl.BlockSpec((groups, tm // CHUNK, cw), lambda i: (0, i, 0)),
                  pl.BlockSpec((1, N_MOD, d), lambda i: (i // tiles_per_mod, 0, 0)),
                  _const_spec(lng.shape), _const_spec(lnb.shape),
                  _const_spec(wglu.shape), _const_spec(bglu.shape), _const_spec(wout.shape),
                  _const_spec(wup.shape), _const_spec(wdn.shape)],
        out_specs=pl.BlockSpec((tm, d), lambda i: (i, 0)),
        out_shape=jax.ShapeDtypeStruct((t, d), jnp.float32),
        scratch_shapes=[pltpu.VMEM((tm, ff), _MXU_DTYPE), pltpu.VMEM((s5_w // LANES, tm, LANES), jnp.float32)],
        compiler_params=pltpu.CompilerParams(dimension_semantics=("arbitrary",),
                                             vmem_limit_bytes=VMEM_LIMIT_BYTES),
        name="out_ffn",
    )(x1, yna, ys5, mod, lng, lnb, wglu, bglu, wout, wup, wdn)


def _pick_tile(n, target):
    t = min(n, target)
    while n % t:
        t //= 2
    return t


def kernel(x, c, ctx, c_ctx, w_ada, b_ada, ln_g, ln_b, ffn1_w_up, ffn1_w_down, w_in, na_rpb, s5_a_re, s5_a_im, s5_log_dt, s5_b_re, s5_b_im, s5_c_re, s5_c_im, s5_d, s5_w_glu, s5_b_glu, w_out, ffn2_w_up, ffn2_w_down):
    depth = w_ada.shape[0]
    assert depth == 1, "single-layer stack only"
    bsz, seq, d = x.shape
    l_ctx = ctx.shape[1]
    s5_w = s5_w_glu.shape[1]
    na_w = w_out.shape[1] - s5_w
    groups = s5_w // S5_CG
    rows = seq // GRID_W
    assert bsz % 8 == 0 and seq % (GRID_W * NA_QROWS) == 0 and rows >= NA_KROWS
    assert seq % CHUNK == 0 and l_ctx % CHUNK == 0 and groups % 2 == 0
    assert na_w % (2 * NA_DH) == 0 and ffn1_w_down.shape[1] % MXU_TILE == 0
    alpha = (2.0 * depth) ** 0.25
    cd = _MXU_DTYPE

    n_rows = -(-(bsz + 1) // 8) * 8
    cc = jnp.zeros((n_rows, d), jnp.float32).at[:bsz].set(c).at[bsz].set(c_ctx)
    n_ada = w_ada.shape[2]
    mod = _ada(cc, w_ada[0], b_ada[0][None, :], _pick_tile(n_ada, 1152)).reshape(n_rows, N_MOD, d)

    lng, lnb = ln_g[0], ln_b[0]
    wup1, wdn1 = ffn1_w_up[0].astype(cd), ffn1_w_down[0].astype(cd)
    wup2, wdn2 = ffn2_w_up[0].astype(cd), ffn2_w_down[0].astype(cd)
    win = w_in[0].astype(cd)

    tm = _pick_tile(seq, 512)
    x1, qkv, xs = _ffn_in(x.reshape(bsz * seq, d), mod, seq // tm, lng, lnb, wup1, wdn1, win, alpha, na_w, tm)
    tmc = _pick_tile(bsz * l_ctx, 512)
    _, qkv_c, xs_c = _ffn_in(ctx.reshape(bsz * l_ctx, d), mod[bsz:bsz + 1], None, lng, lnb, wup1, wdn1, win,
                             alpha, na_w, tmc)

    tables = _s5_tables(s5_a_re[0], s5_a_im[0], s5_log_dt[0], s5_b_re[0], s5_b_im[0], s5_c_re[0], s5_c_im[0],
                        s5_d[0])
    y_s5 = _s5(xs, xs_c, *tables, bsz)

    y_na = _na(qkv.reshape(bsz, seq, 3 * na_w), qkv_c.reshape(bsz, l_ctx, 3 * na_w), _na_bias(na_rpb[0], rows),
               na_w).reshape(bsz * seq, na_w)

    out = _out_ffn(x1, y_na, y_s5, mod, seq // tm, lng, lnb, s5_w_glu[0].astype(cd), s5_b_glu[0][None, :],
                   w_out[0].astype(cd), wup2, wdn2, alpha, tm)
    return out.reshape(bsz, seq, d)
```

```python
import functools

import jax
import jax.numpy as jnp
import numpy as np
from jax import lax
from jax.experimental import pallas as pl
from jax.experimental.pallas import tpu as pltpu

GRID_W = 64
NA_DH = 64
NA_KR = 8
NA_KC = 16
S5_CG = 16
S5_P = 64
LN_EPS = 1e-6
N_MOD = 9

CHUNK = 16
NA_QROWS = 4
NA_KROWS = 12
MASK_VALUE = -1e30
SCAN_UNROLL = 4

MXU_TILE = 256
LANES = 128
SUBLANES = 8
VMEM_LIMIT_BYTES = 56 * 1024 * 1024

_MXU_DTYPE = jnp.bfloat16
_HI = lax.Precision.HIGHEST


def _mm(a, b):
    return jnp.dot(a, b, preferred_element_type=jnp.float32)


def _mm_nt(a, b):
    return lax.dot_general(a, b, (((1,), (1,)), ((), ())), preferred_element_type=jnp.float32)


def _const_spec(shape):
    nd = len(shape)
    return pl.BlockSpec(shape, lambda *_: (0,) * nd, pipeline_mode=pl.Buffered(1))


def _layer_norm(r, g, b):
    mu = jnp.mean(r, axis=-1, keepdims=True)
    d = r - mu
    var = jnp.mean(d * d, axis=-1, keepdims=True)
    return d * lax.rsqrt(var + LN_EPS) * g + b


def _swiglu(h, wup_ref, wdn_ref, p_s):
    ff = wdn_ref.shape[0]
    for j in range(ff // MXU_TILE):
        lo = j * MXU_TILE
        a = _mm(h, wup_ref[:, lo:lo + MXU_TILE])
        g = _mm(h, wup_ref[:, ff + lo:ff + lo + MXU_TILE])
        p_s[:, lo:lo + MXU_TILE] = (g * jax.nn.sigmoid(g) * a).astype(p_s.dtype)
    return _mm(p_s[...], wdn_ref[...])


def _tokens_to_chunks(u_s, xs_ref):
    n_ch = u_s.shape[1] // CHUNK
    per = LANES // S5_CG
    blk = lax.broadcasted_iota(jnp.int32, (1, LANES), 1) // S5_CG
    for v in range(u_s.shape[0]):
        for w in range(CHUNK // per):
            srcs = [u_s[v, pl.ds(per * w + jj, n_ch, stride=CHUNK), :] for jj in range(per)]
            for gg in range(per):
                acc = None
                for jj in range(per):
                    k = (jj - gg) % per
                    r = srcs[jj] if k == 0 else pltpu.roll(srcs[jj], S5_CG * k, axis=1)
                    acc = r if acc is None else jnp.where(blk == jj, r, acc)
                xs_ref[per * v + gg, :, w * LANES:(w + 1) * LANES] = acc.astype(xs_ref.dtype)


def _chunks_to_tokens(ys_ref, y_s):
    n_ch = y_s.shape[1] // CHUNK
    per = LANES // S5_CG
    blk = lax.broadcasted_iota(jnp.int32, (1, LANES), 1) // S5_CG
    for v in range(y_s.shape[0]):
        for w in range(CHUNK // per):
            srcs = [ys_ref[per * v + gg, :, w * LANES:(w + 1) * LANES] for gg in range(per)]
            for jj in range(per):
                acc = None
                for gg in range(per):
                    k = (gg - jj) % per
                    r = srcs[gg] if k == 0 else pltpu.roll(srcs[gg], S5_CG * k, axis=1)
                    acc = r if acc is None else jnp.where(blk == gg, r, acc)
                y_s[v, pl.ds(per * w + jj, n_ch, stride=CHUNK), :] = acc


def _ada_kernel(c_ref, w_ref, b_ref, o_ref):
    cc = c_ref[...]
    s = cc * jax.nn.sigmoid(cc)
    o_ref[...] = jnp.dot(s, w_ref[...], precision=_HI,
                         preferred_element_type=jnp.float32) + b_ref[...]


def _ada(cc, w, b, bn):
    rows, d = cc.shape
    n = w.shape[1]
    return pl.pallas_call(
        _ada_kernel,
        grid=(n // bn,),
        in_specs=[pl.BlockSpec((rows, d), lambda j: (0, 0)),
                  pl.BlockSpec((d, bn), lambda j: (0, j)),
                  pl.BlockSpec((1, bn), lambda j: (0, j))],
        out_specs=pl.BlockSpec((rows, bn), lambda j: (0, j)),
        out_shape=jax.ShapeDtypeStruct((rows, n), jnp.float32),
        compiler_params=pltpu.CompilerParams(dimension_semantics=("arbitrary",),
                                             vmem_limit_bytes=VMEM_LIMIT_BYTES),
        name="ada",
    )(cc, w, b)


def _ffn_in_kernel(alpha, na_w, x_ref, mod_ref, lng_ref, lnb_ref, wup_ref, wdn_ref, win_ref,
                   x1_ref, qkv_ref, xs_ref, p_s, u_s):
    x = x_ref[...]
    mod = mod_ref[0]
    h = (x * (1.0 + mod[1:2]) + mod[0:1]).astype(p_s.dtype)
    y = _swiglu(h, wup_ref, wdn_ref, p_s)
    x1 = _layer_norm(alpha * x + (0.5 * mod[2:3]) * y, lng_ref[0:1], lnb_ref[0:1])
    x1_ref[...] = x1
    h2 = (x1 * (1.0 + mod[4:5]) + mod[3:4]).astype(p_s.dtype)
    pr = _mm(h2, win_ref[...])
    qkv_ref[:, :na_w] = (pr[:, :na_w] * (NA_DH ** -0.5)).astype(qkv_ref.dtype)
    qkv_ref[:, na_w:] = pr[:, na_w:3 * na_w].astype(qkv_ref.dtype)
    for v in range(u_s.shape[0]):
        u_s[v] = pr[:, 3 * na_w + v * LANES:3 * na_w + (v + 1) * LANES]
    _tokens_to_chunks(u_s, xs_ref)


def _ffn_in(x2d, mod, tiles_per_mod, lng, lnb, wup, wdn, win, alpha, na_w, tm):
    t, d = x2d.shape
    ff = wdn.shape[0]
    ncol = win.shape[1]
    s5_w = ncol - 3 * na_w
    groups, cw = s5_w // S5_CG, CHUNK * S5_CG
    if tiles_per_mod is None:
        mod_map = lambda i: (0, 0, 0)
    else:
        mod_map = lambda i: (i // tiles_per_mod, 0, 0)
    return pl.pallas_call(
        functools.partial(_ffn_in_kernel, alpha, na_w),
        grid=(t // tm,),
        in_specs=[pl.BlockSpec((tm, d), lambda i: (i, 0)),
                  pl.BlockSpec((1, N_MOD, d), mod_map),
                  _const_spec(lng.shape), _const_spec(lnb.shape),
                  _const_spec(wup.shape), _const_spec(wdn.shape), _const_spec(win.shape)],
        out_specs=[pl.BlockSpec((tm, d), lambda i: (i, 0)),
                   pl.BlockSpec((tm, 3 * na_w), lambda i: (i, 0)),
                   pl.BlockSpec((groups, tm // CHUNK, cw), lambda i: (0, i, 0))],
        out_shape=[jax.ShapeDtypeStruct((t, d), jnp.float32),
                   jax.ShapeDtypeStruct((t, 3 * na_w), _MXU_DTYPE),
                   jax.ShapeDtypeStruct((groups, t // CHUNK, cw), jnp.float32)],
        scratch_shapes=[pltpu.VMEM((tm, ff), _MXU_DTYPE), pltpu.VMEM((s5_w // LANES, tm, LANES), jnp.float32)],
        compiler_params=pltpu.CompilerParams(dimension_semantics=("arbitrary",),
                                             vmem_limit_bytes=VMEM_LIMIT_BYTES),
        name="ffn_in",
    )(x2d, mod, lng, lnb, wup, wdn, win)


def _toeplitz_kernel(lhs_ref, rhs_ref, o_ref):
    width = o_ref.shape[2]
    lane = lax.broadcasted_iota(jnp.int32, (1, width), 1)
    k_f = jnp.dot(lhs_ref[0, 0], rhs_ref[0, 0], precision=_HI, preferred_element_type=jnp.float32)
    k_b = jnp.dot(lhs_ref[1, 0], rhs_ref[1, 0], precision=_HI, preferred_element_type=jnp.float32)
    for i in range(CHUNK):
        lo, hi = i * S5_CG, (i + 1) * S5_CG
        fwd = k_f if lo == 0 else jnp.where(lane >= lo, pltpu.roll(k_f, lo, axis=1), 0.0)
        bwd = k_b if hi == width else jnp.where(lane < hi, pltpu.roll(k_b, hi, axis=1), 0.0)
        o_ref[0, lo:hi, :] = (fwd + bwd).astype(o_ref.dtype)


def _toeplitz(lhs, rhs):
    _, g, cg, kdim = lhs.shape
    width = rhs.shape[3]
    return pl.pallas_call(
        _toeplitz_kernel,
        grid=(g,),
        in_specs=[pl.BlockSpec((2, 1, cg, kdim), lambda k: (0, k, 0, 0)),
                  pl.BlockSpec((2, 1, kdim, width), lambda k: (0, k, 0, 0))],
        out_specs=pl.BlockSpec((1, width, width), lambda k: (k, 0, 0)),
        out_shape=jax.ShapeDtypeStruct((g, width, width), _MXU_DTYPE),
        compiler_params=pltpu.CompilerParams(dimension_semantics=("arbitrary",)),
        name="s5_toeplitz",
    )(lhs, rhs)


def _s5_tables(a_re, a_im, log_dt, b_re, b_im, c_re, c_im, d_skip):
    g, p = a_re.shape[1], a_re.shape[2]
    f32 = jnp.float32
    a_re, a_im = a_re.astype(f32), a_im.astype(f32)
    dt = jnp.exp(log_dt.astype(f32))[..., None]
    lr, li = a_re * dt, a_im * dt

    def powers(expo, p_last):
        e = jnp.asarray(expo, f32)
        e = e[:, None, :, None] if p_last else e[:, None, None, :]
        x_r = lr[:, :, None, :] if p_last else lr[..., None]
        x_i = li[:, :, None, :] if p_last else li[..., None]
        mag = jnp.exp(e * x_r)
        return mag * jnp.cos(e * x_i), mag * jnp.sin(e * x_i)

    ab_r, ab_i = jnp.exp(lr) * jnp.cos(li), jnp.exp(lr) * jnp.sin(li)
    den = a_re ** 2 + a_im ** 2
    nr, ni = ab_r - 1.0, ab_i
    sr = (nr * a_re + ni * a_im) / den
    si = (ni * a_re - nr * a_im) / den
    bb_r = sr[..., None] * b_re - si[..., None] * b_im
    bb_i = sr[..., None] * b_im + si[..., None] * b_re
    bt_r, bt_i = jnp.swapaxes(bb_r, 2, 3), jnp.swapaxes(bb_i, 2, 3)
    ct_r, ct_i = jnp.swapaxes(c_re, 2, 3).astype(f32), jnp.swapaxes(c_im, 2, 3).astype(f32)

    asc = np.arange(CHUNK)
    desc = asc[::-1].copy()

    q_r, q_i = powers(np.stack([asc, desc]), False)
    rhs_r = ct_r[:, :, :, None, :] * q_r[..., None] - ct_i[:, :, :, None, :] * q_i[..., None]
    rhs_i = ct_r[:, :, :, None, :] * q_i[..., None] + ct_i[:, :, :, None, :] * q_r[..., None]
    rhs = jnp.concatenate([rhs_r, rhs_i], axis=2).reshape(2, g, 2 * p, CHUNK * S5_CG)
    toep = _toeplitz(jnp.concatenate([bt_r, -bt_i], axis=3), rhs)

    w_r, w_i = powers(np.stack([desc, asc]), True)
    bs_r = (w_r[:, :, :, None, :] * bt_r[:, :, None] - w_i[:, :, :, None, :] * bt_i[:, :, None])
    bs_i = (w_r[:, :, :, None, :] * bt_i[:, :, None] + w_i[:, :, :, None, :] * bt_r[:, :, None])
    bs_r = bs_r.reshape(2, g, CHUNK * S5_CG, p)
    bs_i = bs_i.reshape(2, g, CHUNK * S5_CG, p)
    v_r, v_i = powers(np.stack([asc + 1, desc + 1]), False)
    cs_r = (ct_r[:, :, :, None, :] * v_r[..., None] - ct_i[:, :, :, None, :] * v_i[..., None])
    cs_i = -(ct_r[:, :, :, None, :] * v_i[..., None] + ct_i[:, :, :, None, :] * v_r[..., None])
    cs_r = cs_r.reshape(2, g, p, CHUNK * S5_CG)
    cs_i = cs_i.reshape(2, g, p, CHUNK * S5_CG)
    even = jnp.asarray((np.arange(g) % 2) == 0)[None, :, None, None]
    bs = jnp.concatenate([jnp.where(even, bs_r, 0.0), jnp.where(even, 0.0, bs_r),
                          jnp.where(even, bs_i, 0.0), jnp.where(even, 0.0, bs_i)], axis=3)
    cs = jnp.concatenate([jnp.where(even, cs_r, 0.0), jnp.where(even, 0.0, cs_r),
                          jnp.where(even, cs_i, 0.0), jnp.where(even, 0.0, cs_i)], axis=2)

    e_r, e_i = jnp.exp(CHUNK * lr), CHUNK * li
    half = 2 * p
    a_pow = jnp.stack([(e_r * jnp.cos(e_i)).reshape(2, g // 2, half),
                       (e_r * jnp.sin(e_i)).reshape(2, g // 2, half)], axis=2)
    dvec = jnp.tile(d_skip.astype(f32)[:, None, :], (1, CHUNK, 1)).reshape(g, 1, CHUNK * S5_CG)
    return toep, bs.astype(_MXU_DTYPE), cs.astype(_MXU_DTYPE), a_pow, dvec


def _row_pitch(n):
    return n + (4 - n) % 8


def _s5_kernel(nb, xl_ref, xc_ref, toep_ref, bs_ref, cs_ref, apow_ref, dvec_ref, y_ref, zc_s, zl_s, h_s):
    rows_l = xl_ref.shape[1]
    rows_c = xc_ref.shape[1]
    n_l, n_c = rows_l // nb, rows_c // nb
    pitch_l, pitch_c = _row_pitch(n_l), _row_pitch(n_c)
    half = zl_s.shape[3]
    xl = [xl_ref[g] for g in range(2)]
    xlb = [v.astype(_MXU_DTYPE) for v in xl]
    xcb = [xc_ref[g].astype(_MXU_DTYPE) for g in range(2)]

    for d in range(2):
        zc = _mm(xcb[0], bs_ref[d, 0]) + _mm(xcb[1], bs_ref[d, 1])
        zl = _mm(xlb[0], bs_ref[d, 0]) + _mm(xlb[1], bs_ref[d, 1])
        for ri in range(2):
            for b in range(nb):
                zc_s[d, ri, b * pitch_c:b * pitch_c + n_c, :] = zc[b * n_c:(b + 1) * n_c, ri * half:(ri + 1) * half]
                zl_s[d, ri, b * pitch_l:b * pitch_l + n_l, :] = zl[b * n_l:(b + 1) * n_l, ri * half:(ri + 1) * half]

    a_r = [jnp.broadcast_to(apow_ref[d, 0, 0:1, :], (nb, half)) for d in range(2)]
    a_i = [jnp.broadcast_to(apow_ref[d, 0, 1:2, :], (nb, half)) for d in range(2)]

    def step(d, n, latent, h_r, h_i):
        if latent:
            h_s[d, 0, pl.ds(n, nb, stride=pitch_l), :] = h_r
            h_s[d, 1, pl.ds(n, nb, stride=pitch_l), :] = h_i
            z_r = zl_s[d, 0, pl.ds(n, nb, stride=pitch_l), :]
            z_i = zl_s[d, 1, pl.ds(n, nb, stride=pitch_l), :]
        else:
            z_r = zc_s[d, 0, pl.ds(n, nb, stride=pitch_c), :]
            z_i = zc_s[d, 1, pl.ds(n, nb, stride=pitch_c), :]
        return (a_r[d] * h_r - a_i[d] * h_i + z_r,
                a_r[d] * h_i + a_i[d] * h_r + z_i)

    def ctx_body(n, carry):
        f_r, f_i, b_r, b_i = carry
        f_r, f_i = step(0, n, False, f_r, f_i)
        b_r, b_i = step(1, n_c - 1 - n, False, b_r, b_i)
        return f_r, f_i, b_r, b_i

    def lat_body(n, carry):
        f_r, f_i, b_r, b_i = carry
        f_r, f_i = step(0, n, True, f_r, f_i)
        b_r, b_i = step(1, n_l - 1 - n, True, b_r, b_i)
        return f_r, f_i, b_r, b_i

    zero = jnp.zeros((nb, half), jnp.float32)
    carry = lax.fori_loop(0, n_c, ctx_body, (zero, zero, zero, zero), unroll=SCAN_UNROLL)
    lax.fori_loop(0, n_l, lat_body, carry, unroll=SCAN_UNROLL)

    def plane(d, ri):
        return jnp.concatenate([h_s[d, ri, b * pitch_l:b * pitch_l + n_l, :] for b in range(nb)], axis=0)

    hb = [jnp.concatenate([plane(d, 0), plane(d, 1)], axis=1).astype(_MXU_DTYPE) for d in range(2)]
    for g in range(2):
        y = _mm(xlb[g], toep_ref[g]) + _mm(hb[0], cs_ref[0, g]) + _mm(hb[1], cs_ref[1, g])
        y_ref[g] = y + xl[g] * dvec_ref[g]


def _s5(xl, xc, toep, bs, cs, a_pow, dvec, nb):
    g, rows_l, w = xl.shape
    rows_c = xc.shape[1]
    sw = bs.shape[3]
    return pl.pallas_call(
        functools.partial(_s5_kernel, nb),
        grid=(g // 2,),
        in_specs=[pl.BlockSpec((2, rows_l, w), lambda k: (k, 0, 0)),
                  pl.BlockSpec((2, rows_c, w), lambda k: (k, 0, 0)),
                  pl.BlockSpec((2, w, w), lambda k: (k, 0, 0)),
                  pl.BlockSpec((2, 2, w, sw), lambda k: (0, k, 0, 0)),
                  pl.BlockSpec((2, 2, sw, w), lambda k: (0, k, 0, 0)),
                  pl.BlockSpec((2, 1, 2, sw // 2), lambda k: (0, k, 0, 0)),
                  pl.BlockSpec((2, 1, w), lambda k: (k, 0, 0))],
        out_specs=pl.BlockSpec((2, rows_l, w), lambda k: (k, 0, 0)),
        out_shape=jax.ShapeDtypeStruct((g, rows_l, w), jnp.float32),
        scratch_shapes=[pltpu.VMEM((2, 2, nb * _row_pitch(rows_c // nb), sw // 2), jnp.float32),
                        pltpu.VMEM((2, 2, nb * _row_pitch(rows_l // nb), sw // 2), jnp.float32),
                        pltpu.VMEM((2, 2, nb * _row_pitch(rows_l // nb), sw // 2), jnp.float32)],
        compiler_params=pltpu.CompilerParams(dimension_semantics=("arbitrary",),
                                             vmem_limit_bytes=VMEM_LIMIT_BYTES),
        name="s5",
    )(xl, xc, toep, bs, cs, a_pow, dvec)


def _na_row_plan():
    last_q = NA_KROWS - NA_QROWS
    plan = []
    for r0s, qrs in ((lambda lr: 0, lambda lr: lr),
                     (lambda lr: lr, lambda lr: NA_KR // 2 + lr),
                     (lambda lr: last_q - NA_KR // 2, lambda lr: last_q + lr)):
        plan.append([[kl - qrs(lr) + (NA_KR - 1) if r0s(lr) <= kl < r0s(lr) + NA_KR else None
                      for kl in range(NA_KROWS)] for lr in range(NA_QROWS)])
    return plan


def _na_bias_kernel(plan, col_ref, o_ref):
    lane = lax.broadcasted_iota(jnp.int32, (1, 2 * GRID_W), 1)
    masked = jnp.full((GRID_W, 2 * GRID_W), MASK_VALUE, jnp.float32)
    for kind in range(len(plan)):
        for lr in range(NA_QROWS):
            for kp in range(NA_KROWS // 2):
                a0, a1 = plan[kind][lr][2 * kp], plan[kind][lr][2 * kp + 1]
                t0 = masked if a0 is None else col_ref[0, a0]
                t1 = masked if a1 is None else col_ref[0, a1]
                tile = t0 if a0 == a1 else jnp.where(lane < GRID_W, t0, t1)
                o_ref[kind, 0, lr * GRID_W:(lr + 1) * GRID_W, kp * 2 * GRID_W:(kp + 1) * 2 * GRID_W] = tile


def _na_bias(rpb):
    heads = rpb.shape[0]
    qc = np.arange(GRID_W)[:, None]
    kc = np.arange(GRID_W)[None, :]
    c0 = np.clip(qc - NA_KC // 2, 0, GRID_W - NA_KC)
    col_ok = (kc >= c0) & (kc < c0 + NA_KC)
    sel_c = ((kc - qc + (NA_KC - 1))[:, :, None] == np.arange(2 * NA_KC - 1)) & col_ok[:, :, None]
    by_col = jnp.einsum('hab,qkb->haqk', rpb.astype(jnp.float32), jnp.asarray(sel_c, jnp.float32), precision=_HI)
    by_col = jnp.where(jnp.asarray(col_ok), by_col, MASK_VALUE)
    by_col = jnp.concatenate([by_col, by_col], axis=3)
    plan = _na_row_plan()
    nq, nk = NA_QROWS * GRID_W, NA_KROWS * GRID_W
    return pl.pallas_call(
        functools.partial(_na_bias_kernel, plan),
        grid=(heads,),
        in_specs=[pl.BlockSpec((1,) + by_col.shape[1:], lambda h: (h, 0, 0, 0))],
        out_specs=pl.BlockSpec((len(plan), 1, nq, nk), lambda h: (0, h, 0, 0)),
        out_shape=jax.ShapeDtypeStruct((len(plan), heads, nq, nk), jnp.float32),
        compiler_params=pltpu.CompilerParams(dimension_semantics=("arbitrary",)),
        name="na_bias",
    )(by_col)


def _na_kernel(rows, q_ref, k_ref, v_ref, kc_ref, vc_ref, bias_ref, o_ref):
    rb = pl.program_id(2)
    start = jnp.clip(rb * NA_QROWS - NA_KR // 2, 0, rows - NA_KROWS) * GRID_W
    start = pl.multiple_of(start, GRID_W)
    nk = NA_KROWS * GRID_W
    q2 = q_ref[0]
    kw = k_ref[0, pl.ds(start, nk), :]
    vw = v_ref[0, pl.ds(start, nk), :]
    kcx = kc_ref[0]
    vcx = vc_ref[0]
    lane = lax.broadcasted_iota(jnp.int32, (1, q2.shape[1]), 1)
    outs = []
    for hh in range(2):
        qm = jnp.where(lane // NA_DH == hh, q2, jnp.zeros_like(q2))
        s_w = _mm_nt(qm, kw) + bias_ref[0, hh]
        s_c = _mm_nt(qm, kcx)
        mx = jnp.maximum(jnp.max(s_w, axis=-1, keepdims=True), jnp.max(s_c, axis=-1, keepdims=True))
        p_w = jnp.exp(s_w - mx)
        p_c = jnp.exp(s_c - mx)
        denom = jnp.sum(p_w, axis=-1, keepdims=True) + jnp.sum(p_c, axis=-1, keepdims=True)
        o = _mm(p_w.astype(vw.dtype), vw) + _mm(p_c.astype(vcx.dtype), vcx)
        outs.append(o * (1.0 / denom))
    o_ref[0] = jnp.where(lane // NA_DH == 0, outs[0], outs[1]).astype(o_ref.dtype)


def _na(qkv, qkv_c, bias, na_w):
    b, s, _ = qkv.shape
    l = qkv_c.shape[1]
    rows = s // GRID_W
    n_rb = rows // NA_QROWS
    n_hp = na_w // (2 * NA_DH)
    tq = NA_QROWS * GRID_W
    lanes = 2 * NA_DH

    def bias_map(bi, hp, rb):
        kind = jnp.where(rb == 0, 0, jnp.where(rb == n_rb - 1, 2, 1))
        return (kind, hp, 0, 0)

    return pl.pallas_call(
        functools.partial(_na_kernel, rows),
        grid=(b, n_hp, n_rb),
        in_specs=[pl.BlockSpec((1, tq, lanes), lambda bi, hp, rb: (bi, rb, hp)),
                  pl.BlockSpec((1, s, lanes), lambda bi, hp, rb: (bi, 0, n_hp + hp)),
                  pl.BlockSpec((1, s, lanes), lambda bi, hp, rb: (bi, 0, 2 * n_hp + hp)),
                  pl.BlockSpec((1, l, lanes), lambda bi, hp, rb: (bi, 0, n_hp + hp)),
                  pl.BlockSpec((1, l, lanes), lambda bi, hp, rb: (bi, 0, 2 * n_hp + hp)),
                  pl.BlockSpec((1, 2, tq, NA_KROWS * GRID_W), bias_map)],
        out_specs=pl.BlockSpec((1, tq, lanes), lambda bi, hp, rb: (bi, rb, hp)),
        out_shape=jax.ShapeDtypeStruct((b, s, na_w), _MXU_DTYPE),
        compiler_params=pltpu.CompilerParams(
            dimension_semantics=("arbitrary", "arbitrary", "arbitrary"),
            vmem_limit_bytes=VMEM_LIMIT_BYTES),
        name="na",
    )(qkv, qkv, qkv, qkv_c, qkv_c, bias)


def _out_ffn_kernel(alpha, x1_ref, yna_ref, ys5_ref, mod_ref, lng_ref, lnb_ref, wglu_ref, bglu_ref,
                    wout_ref, wup_ref, wdn_ref, o_ref, p_s, y_s):
    mod = mod_ref[0]
    na_w = yna_ref.shape[1]
    _chunks_to_tokens(ys5_ref, y_s)
    gl = jax.nn.gelu(jnp.concatenate([y_s[v] for v in range(y_s.shape[0])], axis=1))
    gate = jax.nn.sigmoid(_mm(gl.astype(p_s.dtype), wglu_ref[...]) + bglu_ref[...])
    glu = (gl * gate).astype(p_s.dtype)
    y = _mm(yna_ref[...], wout_ref[:na_w, :]) + _mm(glu, wout_ref[na_w:, :])
    x2 = _layer_norm(alpha * x1_ref[...] + mod[5:6] * y, lng_ref[1:2], lnb_ref[1:2])
    h = (x2 * (1.0 + mod[7:8]) + mod[6:7]).astype(p_s.dtype)
    f = _swiglu(h, wup_ref, wdn_ref, p_s)
    o_ref[...] = _layer_norm(alpha * x2 + (0.5 * mod[8:9]) * f, lng_ref[2:3], lnb_ref[2:3])


def _out_ffn(x1, yna, ys5, mod, tiles_per_mod, lng, lnb, wglu, bglu, wout, wup, wdn, alpha, tm):
    t, d = x1.shape
    ff = wdn.shape[0]
    na_w = yna.shape[1]
    groups, _, cw = ys5.shape
    s5_w = groups * S5_CG
    return pl.pallas_call(
        functools.partial(_out_ffn_kernel, alpha),
        grid=(t // tm,),
        in_specs=[pl.BlockSpec((tm, d), lambda i: (i, 0)),
                  pl.BlockSpec((tm, na_w), lambda i: (i, 0)),
                  pl.BlockSpec((groups, tm // CHUNK, cw), lambda i: (0, i, 0)),
                  pl.BlockSpec((1, N_MOD, d), lambda i: (i // tiles_per_mod, 0, 0)),
                  _const_spec(lng.shape), _const_spec(lnb.shape),
                  _const_spec(wglu.shape), _const_spec(bglu.shape), _const_spec(wout.shape),
                  _const_spec(wup.shape), _const_spec(wdn.shape)],
        out_specs=pl.BlockSpec((tm, d), lambda i: (i, 0)),
        out_shape=jax.ShapeDtypeStruct((t, d), jnp.float32),
        scratch_shapes=[pltpu.VMEM((tm, ff), _MXU_DTYPE), pltpu.VMEM((s5_w // LANES, tm, LANES), jnp.float32)],
        compiler_params=pltpu.CompilerParams(dimension_semantics=("arbitrary",),
                                             vmem_limit_bytes=VMEM_LIMIT_BYTES),
        name="out_ffn",
    )(x1, yna, ys5, mod, lng, lnb, wglu, bglu, wout, wup, wdn)


def _pick_tile(n, target):
    t = min(n, target)
    while n % t:
        t //= 2
    return t


def kernel(x, c, ctx, c_ctx, w_ada, b_ada, ln_g, ln_b, ffn1_w_up, ffn1_w_down, w_in, na_rpb, s5_a_re, s5_a_im, s5_log_dt, s5_b_re, s5_b_im, s5_c_re, s5_c_im, s5_d, s5_w_glu, s5_b_glu, w_out, ffn2_w_up, ffn2_w_down):
    depth = w_ada.shape[0]
    assert depth == 1, "single-layer stack only"
    bsz, seq, d = x.shape
    l_ctx = ctx.shape[1]
    s5_w = s5_w_glu.shape[1]
    na_w = w_out.shape[1] - s5_w
    groups = s5_w // S5_CG
    rows = seq // GRID_W
    assert bsz % 8 == 0 and seq % (GRID_W * NA_QROWS) == 0 and rows >= NA_KROWS
    assert seq % CHUNK == 0 and l_ctx % CHUNK == 0 and groups % 2 == 0
    assert na_w % (2 * NA_DH) == 0 and ffn1_w_down.shape[1] % MXU_TILE == 0
    alpha = (2.0 * depth) ** 0.25
    cd = _MXU_DTYPE

    n_rows = -(-(bsz + 1) // 8) * 8
    cc = jnp.zeros((n_rows, d), jnp.float32).at[:bsz].set(c).at[bsz].set(c_ctx)
    n_ada = w_ada.shape[2]
    mod = _ada(cc, w_ada[0], b_ada[0][None, :], _pick_tile(n_ada, 1152)).reshape(n_rows, N_MOD, d)

    lng, lnb = ln_g[0], ln_b[0]
    wup1, wdn1 = ffn1_w_up[0].astype(cd), ffn1_w_down[0].astype(cd)
    wup2, wdn2 = ffn2_w_up[0].astype(cd), ffn2_w_down[0].astype(cd)
    win = w_in[0].astype(cd)

    tm = _pick_tile(seq, 512)
    x1, qkv, xs = _ffn_in(x.reshape(bsz * seq, d), mod, seq // tm, lng, lnb, wup1, wdn1, win, alpha, na_w, tm)
    tmc = _pick_tile(bsz * l_ctx, 512)
    _, qkv_c, xs_c = _ffn_in(ctx.reshape(bsz * l_ctx, d), mod[bsz:bsz + 1], None, lng, lnb, wup1, wdn1, win,
                             alpha, na_w, tmc)

    tables = _s5_tables(s5_a_re[0], s5_a_im[0], s5_log_dt[0], s5_b_re[0], s5_b_im[0], s5_c_re[0], s5_c_im[0],
                        s5_d[0])
    y_s5 = _s5(xs, xs_c, *tables, bsz)

    y_na = _na(qkv.reshape(bsz, seq, 3 * na_w), qkv_c.reshape(bsz, l_ctx, 3 * na_w), _na_bias(na_rpb[0]),
               na_w).reshape(bsz * seq, na_w)

    out = _out_ffn(x1, y_na, y_s5, mod, seq // tm, lng, lnb, s5_w_glu[0].astype(cd), s5_b_glu[0][None, :],
                   w_out[0].astype(cd), wup2, wdn2, alpha, tm)
    return out.reshape(bsz, seq, d)
```

```python
import functools

import jax
import jax.numpy as jnp
import numpy as np
from jax import lax
from jax.experimental import pallas as pl
from jax.experimental.pallas import tpu as pltpu

GRID_W = 64
NA_DH = 64
NA_KR = 8
NA_KC = 16
S5_CG = 16
S5_P = 64
LN_EPS = 1e-6
N_MOD = 9

CHUNK = 16
NA_QROWS = 4
NA_KROWS = 12
MASK_VALUE = -1e30
LOG2_E = 1.4426950408889634
SCAN_UNROLL = 4

MXU_TILE = 256
LANES = 128
SUBLANES = 8
VMEM_LIMIT_BYTES = 56 * 1024 * 1024

_MXU_DTYPE = jnp.bfloat16
_HI = lax.Precision.HIGHEST


def _mm(a, b):
    return jnp.dot(a, b, preferred_element_type=jnp.float32)


def _mm_nt(a, b):
    return lax.dot_general(a, b, (((1,), (1,)), ((), ())), preferred_element_type=jnp.float32)


def _const_spec(shape):
    nd = len(shape)
    return pl.BlockSpec(shape, lambda *_: (0,) * nd, pipeline_mode=pl.Buffered(1))


def _layer_norm(r, g, b):
    mu = jnp.mean(r, axis=-1, keepdims=True)
    d = r - mu
    var = jnp.mean(d * d, axis=-1, keepdims=True)
    return d * lax.rsqrt(var + LN_EPS) * g + b


def _swiglu(h, wup_ref, wdn_ref, p_s):
    ff = wdn_ref.shape[0]
    for j in range(ff // MXU_TILE):
        lo = j * MXU_TILE
        a = _mm(h, wup_ref[:, lo:lo + MXU_TILE])
        g = _mm(h, wup_ref[:, ff + lo:ff + lo + MXU_TILE])
        p_s[:, lo:lo + MXU_TILE] = (g * jax.nn.sigmoid(g) * a).astype(p_s.dtype)
    return _mm(p_s[...], wdn_ref[...])


def _block_transpose(tiles):
    per = len(tiles)
    blk = lax.broadcasted_iota(jnp.int32, (1, LANES), 1) // S5_CG
    tiles = list(tiles)
    d = per // 2
    while d:
        upper = (blk & d) != 0
        for a in range(per):
            if a & d:
                continue
            lo, hi = tiles[a], tiles[a + d]
            tiles[a] = jnp.where(upper, pltpu.roll(hi, S5_CG * d, axis=1), lo)
            tiles[a + d] = jnp.where(upper, hi, pltpu.roll(lo, LANES - S5_CG * d, axis=1))
        d //= 2
    return tiles


def _tokens_to_chunks(u_s, xs_ref):
    n_ch = u_s.shape[1] // CHUNK
    per = LANES // S5_CG
    for v in range(u_s.shape[0]):
        for w in range(CHUNK // per):
            by_group = _block_transpose([u_s[v, pl.ds(per * w + jj, n_ch, stride=CHUNK), :] for jj in range(per)])
            for gg in range(per):
                xs_ref[per * v + gg, :, w * LANES:(w + 1) * LANES] = by_group[gg].astype(xs_ref.dtype)


def _chunks_to_tokens(ys_ref, y_s):
    n_ch = y_s.shape[1] // CHUNK
    per = LANES // S5_CG
    for v in range(y_s.shape[0]):
        for w in range(CHUNK // per):
            by_phase = _block_transpose([ys_ref[per * v + gg, :, w * LANES:(w + 1) * LANES] for gg in range(per)])
            for jj in range(per):
                y_s[v, pl.ds(per * w + jj, n_ch, stride=CHUNK), :] = by_phase[jj]


def _ada_kernel(c_ref, w_ref, b_ref, o_ref):
    cc = c_ref[...]
    s = cc * jax.nn.sigmoid(cc)
    o_ref[...] = jnp.dot(s, w_ref[...], precision=_HI,
                         preferred_element_type=jnp.float32) + b_ref[...]


def _ada(cc, w, b, bn):
    rows, d = cc.shape
    n = w.shape[1]
    return pl.pallas_call(
        _ada_kernel,
        grid=(n // bn,),
        in_specs=[pl.BlockSpec((rows, d), lambda j: (0, 0)),
                  pl.BlockSpec((d, bn), lambda j: (0, j)),
                  pl.BlockSpec((1, bn), lambda j: (0, j))],
        out_specs=pl.BlockSpec((rows, bn), lambda j: (0, j)),
        out_shape=jax.ShapeDtypeStruct((rows, n), jnp.float32),
        compiler_params=pltpu.CompilerParams(dimension_semantics=("arbitrary",),
                                             vmem_limit_bytes=VMEM_LIMIT_BYTES),
        name="ada",
    )(cc, w, b)


def _ffn_in_kernel(alpha, na_w, x_ref, mod_ref, lng_ref, lnb_ref, wup_ref, wdn_ref, win_ref,
                   x1_ref, qkv_ref, xs_ref, p_s, u_s):
    x = x_ref[...]
    mod = mod_ref[0]
    h = (x * (1.0 + mod[1:2]) + mod[0:1]).astype(p_s.dtype)
    y = _swiglu(h, wup_ref, wdn_ref, p_s)
    x1 = _layer_norm(alpha * x + (0.5 * mod[2:3]) * y, lng_ref[0:1], lnb_ref[0:1])
    x1_ref[...] = x1
    h2 = (x1 * (1.0 + mod[4:5]) + mod[3:4]).astype(p_s.dtype)
    pu = _mm(h2, win_ref[:, 3 * na_w:])
    for v in range(u_s.shape[0]):
        u_s[v] = pu[:, v * LANES:(v + 1) * LANES]
    _tokens_to_chunks(u_s, xs_ref)
    pr = _mm(h2, win_ref[:, :3 * na_w])
    qkv_ref[:, :na_w] = (pr[:, :na_w] * (NA_DH ** -0.5 * LOG2_E)).astype(qkv_ref.dtype)
    qkv_ref[:, na_w:] = pr[:, na_w:].astype(qkv_ref.dtype)


def _ffn_in(x2d, mod, tiles_per_mod, lng, lnb, wup, wdn, win, alpha, na_w, tm):
    t, d = x2d.shape
    ff = wdn.shape[0]
    ncol = win.shape[1]
    s5_w = ncol - 3 * na_w
    groups, cw = s5_w // S5_CG, CHUNK * S5_CG
    if tiles_per_mod is None:
        mod_map = lambda i: (0, 0, 0)
    else:
        mod_map = lambda i: (i // tiles_per_mod, 0, 0)
    return pl.pallas_call(
        functools.partial(_ffn_in_kernel, alpha, na_w),
        grid=(t // tm,),
        in_specs=[pl.BlockSpec((tm, d), lambda i: (i, 0)),
                  pl.BlockSpec((1, N_MOD, d), mod_map),
                  _const_spec(lng.shape), _const_spec(lnb.shape),
                  _const_spec(wup.shape), _const_spec(wdn.shape), _const_spec(win.shape)],
        out_specs=[pl.BlockSpec((tm, d), lambda i: (i, 0)),
                   pl.BlockSpec((tm, 3 * na_w), lambda i: (i, 0)),
                   pl.BlockSpec((groups, tm // CHUNK, cw), lambda i: (0, i, 0))],
        out_shape=[jax.ShapeDtypeStruct((t, d), jnp.float32),
                   jax.ShapeDtypeStruct((t, 3 * na_w), _MXU_DTYPE),
                   jax.ShapeDtypeStruct((groups, t // CHUNK, cw), jnp.float32)],
        scratch_shapes=[pltpu.VMEM((tm, ff), _MXU_DTYPE), pltpu.VMEM((s5_w // LANES, tm, LANES), jnp.float32)],
        compiler_params=pltpu.CompilerParams(dimension_semantics=("arbitrary",),
                                             vmem_limit_bytes=VMEM_LIMIT_BYTES),
        name="ffn_in",
    )(x2d, mod, lng, lnb, wup, wdn, win)


def _toeplitz_kernel(lhs_ref, rhs_ref, o_ref):
    width = o_ref.shape[2]
    lane = lax.broadcasted_iota(jnp.int32, (1, width), 1)
    k_f = jnp.dot(lhs_ref[0, 0], rhs_ref[0, 0], precision=_HI, preferred_element_type=jnp.float32)
    k_b = jnp.dot(lhs_ref[1, 0], rhs_ref[1, 0], precision=_HI, preferred_element_type=jnp.float32)
    for i in range(CHUNK):
        lo, hi = i * S5_CG, (i + 1) * S5_CG
        fwd = k_f if lo == 0 else jnp.where(lane >= lo, pltpu.roll(k_f, lo, axis=1), 0.0)
        bwd = k_b if hi == width else jnp.where(lane < hi, pltpu.roll(k_b, hi, axis=1), 0.0)
        o_ref[0, lo:hi, :] = (fwd + bwd).astype(o_ref.dtype)


def _toeplitz(lhs, rhs):
    _, g, cg, kdim = lhs.shape
    width = rhs.shape[3]
    return pl.pallas_call(
        _toeplitz_kernel,
        grid=(g,),
        in_specs=[pl.BlockSpec((2, 1, cg, kdim), lambda k: (0, k, 0, 0)),
                  pl.BlockSpec((2, 1, kdim, width), lambda k: (0, k, 0, 0))],
        out_specs=pl.BlockSpec((1, width, width), lambda k: (k, 0, 0)),
        out_shape=jax.ShapeDtypeStruct((g, width, width), _MXU_DTYPE),
        compiler_params=pltpu.CompilerParams(dimension_semantics=("arbitrary",)),
        name="s5_toeplitz",
    )(lhs, rhs)


def _s5_tables(a_re, a_im, log_dt, b_re, b_im, c_re, c_im, d_skip):
    g, p = a_re.shape[1], a_re.shape[2]
    f32 = jnp.float32
    a_re, a_im = a_re.astype(f32), a_im.astype(f32)
    dt = jnp.exp(log_dt.astype(f32))[..., None]
    lr, li = a_re * dt, a_im * dt

    def powers(expo, p_last):
        e = jnp.asarray(expo, f32)
        e = e[:, None, :, None] if p_last else e[:, None, None, :]
        x_r = lr[:, :, None, :] if p_last else lr[..., None]
        x_i = li[:, :, None, :] if p_last else li[..., None]
        mag = jnp.exp(e * x_r)
        return mag * jnp.cos(e * x_i), mag * jnp.sin(e * x_i)

    ab_r, ab_i = jnp.exp(lr) * jnp.cos(li), jnp.exp(lr) * jnp.sin(li)
    den = a_re ** 2 + a_im ** 2
    nr, ni = ab_r - 1.0, ab_i
    sr = (nr * a_re + ni * a_im) / den
    si = (ni * a_re - nr * a_im) / den
    bb_r = sr[..., None] * b_re - si[..., None] * b_im
    bb_i = sr[..., None] * b_im + si[..., None] * b_re
    bt_r, bt_i = jnp.swapaxes(bb_r, 2, 3), jnp.swapaxes(bb_i, 2, 3)
    ct_r, ct_i = jnp.swapaxes(c_re, 2, 3).astype(f32), jnp.swapaxes(c_im, 2, 3).astype(f32)

    asc = np.arange(CHUNK)
    desc = asc[::-1].copy()

    q_r, q_i = powers(np.stack([asc, desc]), False)
    rhs_r = ct_r[:, :, :, None, :] * q_r[..., None] - ct_i[:, :, :, None, :] * q_i[..., None]
    rhs_i = ct_r[:, :, :, None, :] * q_i[..., None] + ct_i[:, :, :, None, :] * q_r[..., None]
    rhs = jnp.concatenate([rhs_r, rhs_i], axis=2).reshape(2, g, 2 * p, CHUNK * S5_CG)
    toep = _toeplitz(jnp.concatenate([bt_r, -bt_i], axis=3), rhs)

    w_r, w_i = powers(np.stack([desc, asc]), True)
    bs_r = (w_r[:, :, :, None, :] * bt_r[:, :, None] - w_i[:, :, :, None, :] * bt_i[:, :, None])
    bs_i = (w_r[:, :, :, None, :] * bt_i[:, :, None] + w_i[:, :, :, None, :] * bt_r[:, :, None])
    bs_r = bs_r.reshape(2, g, CHUNK * S5_CG, p)
    bs_i = bs_i.reshape(2, g, CHUNK * S5_CG, p)
    v_r, v_i = powers(np.stack([asc + 1, desc + 1]), False)
    cs_r = (ct_r[:, :, :, None, :] * v_r[..., None] - ct_i[:, :, :, None, :] * v_i[..., None])
    cs_i = -(ct_r[:, :, :, None, :] * v_i[..., None] + ct_i[:, :, :, None, :] * v_r[..., None])
    cs_r = cs_r.reshape(2, g, p, CHUNK * S5_CG)
    cs_i = cs_i.reshape(2, g, p, CHUNK * S5_CG)
    even = jnp.asarray((np.arange(g) % 2) == 0)[None, :, None, None]
    bs = jnp.concatenate([jnp.where(even, bs_r, 0.0), jnp.where(even, 0.0, bs_r),
                          jnp.where(even, bs_i, 0.0), jnp.where(even, 0.0, bs_i)], axis=3)
    cs = jnp.concatenate([jnp.where(even, cs_r, 0.0), jnp.where(even, 0.0, cs_r),
                          jnp.where(even, cs_i, 0.0), jnp.where(even, 0.0, cs_i)], axis=2)

    e_r, e_i = jnp.exp(CHUNK * lr), CHUNK * li
    half = 2 * p
    a_pow = jnp.stack([(e_r * jnp.cos(e_i)).reshape(2, g // 2, half),
                       (e_r * jnp.sin(e_i)).reshape(2, g // 2, half)], axis=2)
    dvec = jnp.tile(d_skip.astype(f32)[:, None, :], (1, CHUNK, 1)).reshape(g, 1, CHUNK * S5_CG)
    return toep, bs.astype(_MXU_DTYPE), cs.astype(_MXU_DTYPE), a_pow, dvec


def _row_pitch(n):
    return n + (4 - n) % 8


def _s5_kernel(nb, xl_ref, xc_ref, toep_ref, bs_ref, cs_ref, apow_ref, dvec_ref, y_ref, zc_s, zl_s, h_s):
    rows_l = xl_ref.shape[1]
    rows_c = xc_ref.shape[1]
    n_l, n_c = rows_l // nb, rows_c // nb
    pitch_l, pitch_c = _row_pitch(n_l), _row_pitch(n_c)
    half = zl_s.shape[3]
    xl = [xl_ref[g] for g in range(2)]
    xlb = [v.astype(_MXU_DTYPE) for v in xl]
    xcb = [xc_ref[g].astype(_MXU_DTYPE) for g in range(2)]

    for d in range(2):
        zc = _mm(xcb[0], bs_ref[d, 0]) + _mm(xcb[1], bs_ref[d, 1])
        zl = _mm(xlb[0], bs_ref[d, 0]) + _mm(xlb[1], bs_ref[d, 1])
        for ri in range(2):
            for b in range(nb):
                zc_s[d, ri, b * pitch_c:b * pitch_c + n_c, :] = zc[b * n_c:(b + 1) * n_c, ri * half:(ri + 1) * half]
                zl_s[d, ri, b * pitch_l:b * pitch_l + n_l, :] = zl[b * n_l:(b + 1) * n_l, ri * half:(ri + 1) * half]

    a_r = [jnp.broadcast_to(apow_ref[d, 0, 0:1, :], (nb, half)) for d in range(2)]
    a_i = [jnp.broadcast_to(apow_ref[d, 0, 1:2, :], (nb, half)) for d in range(2)]

    def step(d, n, latent, h_r, h_i):
        if latent:
            h_s[d, 0, pl.ds(n, nb, stride=pitch_l), :] = h_r
            h_s[d, 1, pl.ds(n, nb, stride=pitch_l), :] = h_i
            z_r = zl_s[d, 0, pl.ds(n, nb, stride=pitch_l), :]
            z_i = zl_s[d, 1, pl.ds(n, nb, stride=pitch_l), :]
        else:
            z_r = zc_s[d, 0, pl.ds(n, nb, stride=pitch_c), :]
            z_i = zc_s[d, 1, pl.ds(n, nb, stride=pitch_c), :]
        return (a_r[d] * h_r - a_i[d] * h_i + z_r,
                a_r[d] * h_i + a_i[d] * h_r + z_i)

    def ctx_body(n, carry):
        f_r, f_i, b_r, b_i = carry
        f_r, f_i = step(0, n, False, f_r, f_i)
        b_r, b_i = step(1, n_c - 1 - n, False, b_r, b_i)
        return f_r, f_i, b_r, b_i

    def lat_body(n, carry):
        f_r, f_i, b_r, b_i = carry
        f_r, f_i = step(0, n, True, f_r, f_i)
        b_r, b_i = step(1, n_l - 1 - n, True, b_r, b_i)
        return f_r, f_i, b_r, b_i

    zero = jnp.zeros((nb, half), jnp.float32)
    carry = lax.fori_loop(0, n_c, ctx_body, (zero, zero, zero, zero), unroll=SCAN_UNROLL)
    lax.fori_loop(0, n_l, lat_body, carry, unroll=SCAN_UNROLL)

    def plane(d, ri):
        return jnp.concatenate([h_s[d, ri, b * pitch_l:b * pitch_l + n_l, :] for b in range(nb)], axis=0)

    hb = [jnp.concatenate([plane(d, 0), plane(d, 1)], axis=1).astype(_MXU_DTYPE) for d in range(2)]
    for g in range(2):
        y = _mm(xlb[g], toep_ref[g]) + _mm(hb[0], cs_ref[0, g]) + _mm(hb[1], cs_ref[1, g])
        y_ref[g] = y + xl[g] * dvec_ref[g]


def _s5(xl, xc, toep, bs, cs, a_pow, dvec, nb):
    g, rows_l, w = xl.shape
    rows_c = xc.shape[1]
    sw = bs.shape[3]
    return pl.pallas_call(
        functools.partial(_s5_kernel, nb),
        grid=(g // 2,),
        in_specs=[pl.BlockSpec((2, rows_l, w), lambda k: (k, 0, 0)),
                  pl.BlockSpec((2, rows_c, w), lambda k: (k, 0, 0)),
                  pl.BlockSpec((2, w, w), lambda k: (k, 0, 0)),
                  pl.BlockSpec((2, 2, w, sw), lambda k: (0, k, 0, 0)),
                  pl.BlockSpec((2, 2, sw, w), lambda k: (0, k, 0, 0)),
                  pl.BlockSpec((2, 1, 2, sw // 2), lambda k: (0, k, 0, 0)),
                  pl.BlockSpec((2, 1, w), lambda k: (k, 0, 0))],
        out_specs=pl.BlockSpec((2, rows_l, w), lambda k: (k, 0, 0)),
        out_shape=jax.ShapeDtypeStruct((g, rows_l, w), jnp.float32),
        scratch_shapes=[pltpu.VMEM((2, 2, nb * _row_pitch(rows_c // nb), sw // 2), jnp.float32),
                        pltpu.VMEM((2, 2, nb * _row_pitch(rows_l // nb), sw // 2), jnp.float32),
                        pltpu.VMEM((2, 2, nb * _row_pitch(rows_l // nb), sw // 2), jnp.float32)],
        compiler_params=pltpu.CompilerParams(dimension_semantics=("arbitrary",),
                                             vmem_limit_bytes=VMEM_LIMIT_BYTES),
        name="s5",
    )(xl, xc, toep, bs, cs, a_pow, dvec)


def _na_row_plan():
    last_q = NA_KROWS - NA_QROWS
    plan = []
    for r0s, qrs in ((lambda lr: 0, lambda lr: lr),
                     (lambda lr: lr, lambda lr: NA_KR // 2 + lr),
                     (lambda lr: last_q - NA_KR // 2, lambda lr: last_q + lr)):
        plan.append([[kl - qrs(lr) + (NA_KR - 1) if r0s(lr) <= kl < r0s(lr) + NA_KR else None
                      for kl in range(NA_KROWS)] for lr in range(NA_QROWS)])
    return plan


def _na_bias_kernel(plan, col_ref, o_ref):
    lane = lax.broadcasted_iota(jnp.int32, (1, 2 * GRID_W), 1)
    masked = jnp.full((GRID_W, 2 * GRID_W), MASK_VALUE, jnp.float32)
    for kind in range(len(plan)):
        for lr in range(NA_QROWS):
            for kp in range(NA_KROWS // 2):
                a0, a1 = plan[kind][lr][2 * kp], plan[kind][lr][2 * kp + 1]
                t0 = masked if a0 is None else col_ref[0, a0]
                t1 = masked if a1 is None else col_ref[0, a1]
                tile = t0 if a0 == a1 else jnp.where(lane < GRID_W, t0, t1)
                o_ref[kind, 0, lr * GRID_W:(lr + 1) * GRID_W, kp * 2 * GRID_W:(kp + 1) * 2 * GRID_W] = tile


def _na_bias(rpb):
    heads = rpb.shape[0]
    qc = np.arange(GRID_W)[:, None]
    kc = np.arange(GRID_W)[None, :]
    c0 = np.clip(qc - NA_KC // 2, 0, GRID_W - NA_KC)
    col_ok = (kc >= c0) & (kc < c0 + NA_KC)
    sel_c = ((kc - qc + (NA_KC - 1))[:, :, None] == np.arange(2 * NA_KC - 1)) & col_ok[:, :, None]
    by_col = jnp.einsum('hab,qkb->haqk', rpb.astype(jnp.float32), jnp.asarray(sel_c, jnp.float32), precision=_HI)
    by_col = jnp.where(jnp.asarray(col_ok), by_col * LOG2_E, MASK_VALUE)
    by_col = jnp.concatenate([by_col, by_col], axis=3)
    plan = _na_row_plan()
    nq, nk = NA_QROWS * GRID_W, NA_KROWS * GRID_W
    return pl.pallas_call(
        functools.partial(_na_bias_kernel, plan),
        grid=(heads,),
        in_specs=[pl.BlockSpec((1,) + by_col.shape[1:], lambda h: (h, 0, 0, 0))],
        out_specs=pl.BlockSpec((len(plan), 1, nq, nk), lambda h: (0, h, 0, 0)),
        out_shape=jax.ShapeDtypeStruct((len(plan), heads, nq, nk), jnp.float32),
        compiler_params=pltpu.CompilerParams(dimension_semantics=("arbitrary",)),
        name="na_bias",
    )(by_col)


def _na_kernel(rows, q_ref, k_ref, v_ref, kc_ref, vc_ref, bias_ref, o_ref):
    rb = pl.program_id(1)
    start = jnp.clip(rb * NA_QROWS - NA_KR // 2, 0, rows - NA_KROWS) * GRID_W
    start = pl.multiple_of(start, GRID_W)
    nk = NA_KROWS * GRID_W
    pair = 2 * NA_DH
    lane = lax.broadcasted_iota(jnp.int32, (1, pair), 1)
    for hp in range(q_ref.shape[2] // pair):
        cols = slice(hp * pair, (hp + 1) * pair)
        q2 = q_ref[0, :, cols]
        kw = k_ref[0, pl.ds(start, nk), cols]
        vw = v_ref[0, pl.ds(start, nk), cols]
        kcx = kc_ref[0, :, cols]
        vcx = vc_ref[0, :, cols]
        outs = []
        for hh in range(2):
            qm = jnp.where(lane // NA_DH == hh, q2, jnp.zeros_like(q2))
            s_w = _mm_nt(qm, kw) + bias_ref[0, 2 * hp + hh]
            s_c = _mm_nt(qm, kcx)
            mx = jnp.maximum(jnp.max(s_w, axis=-1, keepdims=True), jnp.max(s_c, axis=-1, keepdims=True))
            p_w = jnp.exp2(s_w - mx)
            p_c = jnp.exp2(s_c - mx)
            denom = jnp.sum(p_w, axis=-1, keepdims=True) + jnp.sum(p_c, axis=-1, keepdims=True)
            o = _mm(p_w.astype(vw.dtype), vw) + _mm(p_c.astype(vcx.dtype), vcx)
            outs.append(o * (1.0 / denom))
        o_ref[0, :, cols] = jnp.where(lane // NA_DH == 0, outs[0], outs[1]).astype(o_ref.dtype)


def _na(qkv, qkv_c, bias, na_w):
    b, s, _ = qkv.shape
    l = qkv_c.shape[1]
    rows = s // GRID_W
    n_rb = rows // NA_QROWS
    tq = NA_QROWS * GRID_W

    def bias_map(bi, rb):
        kind = jnp.where(rb == 0, 0, jnp.where(rb == n_rb - 1, 2, 1))
        return (kind, 0, 0, 0)

    return pl.pallas_call(
        functools.partial(_na_kernel, rows),
        grid=(b, n_rb),
        in_specs=[pl.BlockSpec((1, tq, na_w), lambda bi, rb: (bi, rb, 0)),
                  pl.BlockSpec((1, s, na_w), lambda bi, rb: (bi, 0, 1)),
                  pl.BlockSpec((1, s, na_w), lambda bi, rb: (bi, 0, 2)),
                  pl.BlockSpec((1, l, na_w), lambda bi, rb: (bi, 0, 1)),
                  pl.BlockSpec((1, l, na_w), lambda bi, rb: (bi, 0, 2)),
                  pl.BlockSpec((1,) + bias.shape[1:], bias_map)],
        out_specs=pl.BlockSpec((1, tq, na_w), lambda bi, rb: (bi, rb, 0)),
        out_shape=jax.ShapeDtypeStruct((b, s, na_w), _MXU_DTYPE),
        compiler_params=pltpu.CompilerParams(
            dimension_semantics=("arbitrary", "arbitrary"),
            vmem_limit_bytes=VMEM_LIMIT_BYTES),
        name="na",
    )(qkv, qkv, qkv, qkv_c, qkv_c, bias)


def _out_ffn_kernel(alpha, x1_ref, yna_ref, ys5_ref, mod_a_ref, mod_b_ref, lng_ref, lnb_ref, wglu_ref, bglu_ref,
                    wout_ref, wup_ref, wdn_ref, o_ref, p_s, y_s, h_s, x2_s):
    @pl.when(pl.program_id(0) == 0)
    def _():
        h_s[...] = jnp.zeros_like(h_s)
        x2_s[...] = jnp.zeros_like(x2_s)

    mod_b = mod_b_ref[0]
    f = _swiglu(h_s[...], wup_ref, wdn_ref, p_s)
    o_ref[...] = _layer_norm(alpha * x2_s[...] + (0.5 * mod_b[8:9]) * f, lng_ref[2:3], lnb_ref[2:3])

    mod_a = mod_a_ref[0]
    na_w = yna_ref.shape[1]
    _chunks_to_tokens(ys5_ref, y_s)
    gl = jax.nn.gelu(jnp.concatenate([y_s[v] for v in range(y_s.shape[0])], axis=1))
    gate = jax.nn.sigmoid(_mm(gl.astype(p_s.dtype), wglu_ref[...]) + bglu_ref[...])
    glu = (gl * gate).astype(p_s.dtype)
    y = _mm(yna_ref[...], wout_ref[:na_w, :]) + _mm(glu, wout_ref[na_w:, :])
    x2 = _layer_norm(alpha * x1_ref[...] + mod_a[5:6] * y, lng_ref[1:2], lnb_ref[1:2])
    x2_s[...] = x2
    h_s[...] = (x2 * (1.0 + mod_a[7:8]) + mod_a[6:7]).astype(h_s.dtype)


def _out_ffn(x1, yna, ys5, mod, tiles_per_mod, lng, lnb, wglu, bglu, wout, wup, wdn, alpha, tm):
    t, d = x1.shape
    ff = wdn.shape[0]
    na_w = yna.shape[1]
    groups, _, cw = ys5.shape
    s5_w = groups * S5_CG
    n = t // tm

    def cur(i):
        return jnp.minimum(i, n - 1)

    def prev(i):
        return jnp.maximum(i - 1, 0)

    return pl.pallas_call(
        functools.partial(_out_ffn_kernel, alpha),
        grid=(n + 1,),
        in_specs=[pl.BlockSpec((tm, d), lambda i: (cur(i), 0)),
                  pl.BlockSpec((tm, na_w), lambda i: (cur(i), 0)),
                  pl.BlockSpec((groups, tm // CHUNK, cw), lambda i: (0, cur(i), 0)),
                  pl.BlockSpec((1, N_MOD, d), lambda i: (cur(i) // tiles_per_mod, 0, 0)),
                  pl.BlockSpec((1, N_MOD, d), lambda i: (prev(i) // tiles_per_mod, 0, 0)),
                  _const_spec(lng.shape), _const_spec(lnb.shape),
                  _const_spec(wglu.shape), _const_spec(bglu.shape), _const_spec(wout.shape),
                  _const_spec(wup.shape), _const_spec(wdn.shape)],
        out_specs=pl.BlockSpec((tm, d), lambda i: (prev(i), 0)),
        out_shape=jax.ShapeDtypeStruct((t, d), jnp.float32),
        scratch_shapes=[pltpu.VMEM((tm, ff), _MXU_DTYPE), pltpu.VMEM((s5_w // LANES, tm, LANES), jnp.float32),
                        pltpu.VMEM((tm, d), _MXU_DTYPE), pltpu.VMEM((tm, d), jnp.float32)],
        compiler_params=pltpu.CompilerParams(dimension_semantics=("arbitrary",),
                                             vmem_limit_bytes=VMEM_LIMIT_BYTES),
        name="out_ffn",
    )(x1, yna, ys5, mod, mod, lng, lnb, wglu, bglu, wout, wup, wdn)


def _pick_tile(n, target):
    t = min(n, target)
    while n % t:
        t //= 2
    return t


def kernel(x, c, ctx, c_ctx, w_ada, b_ada, ln_g, ln_b, ffn1_w_up, ffn1_w_down, w_in, na_rpb, s5_a_re, s5_a_im, s5_log_dt, s5_b_re, s5_b_im, s5_c_re, s5_c_im, s5_d, s5_w_glu, s5_b_glu, w_out, ffn2_w_up, ffn2_w_down):
    depth = w_ada.shape[0]
    assert depth == 1, "single-layer stack only"
    bsz, seq, d = x.shape
    l_ctx = ctx.shape[1]
    s5_w = s5_w_glu.shape[1]
    na_w = w_out.shape[1] - s5_w
    groups = s5_w // S5_CG
    rows = seq // GRID_W
    assert bsz % 8 == 0 and seq % (GRID_W * NA_QROWS) == 0 and rows >= NA_KROWS
    assert seq % CHUNK == 0 and l_ctx % CHUNK == 0 and groups % 2 == 0
    assert na_w % (2 * NA_DH) == 0 and ffn1_w_down.shape[1] % MXU_TILE == 0
    alpha = (2.0 * depth) ** 0.25
    cd = _MXU_DTYPE

    n_rows = -(-(bsz + 1) // 8) * 8
    cc = jnp.zeros((n_rows, d), jnp.float32).at[:bsz].set(c).at[bsz].set(c_ctx)
    n_ada = w_ada.shape[2]
    mod = _ada(cc, w_ada[0], b_ada[0][None, :], _pick_tile(n_ada, 1152)).reshape(n_rows, N_MOD, d)

    lng, lnb = ln_g[0], ln_b[0]
    wup1, wdn1 = ffn1_w_up[0].astype(cd), ffn1_w_down[0].astype(cd)
    wup2, wdn2 = ffn2_w_up[0].astype(cd), ffn2_w_down[0].astype(cd)
    win = w_in[0].astype(cd)

    tm = _pick_tile(seq, 512)
    x1, qkv, xs = _ffn_in(x.reshape(bsz * seq, d), mod, seq // tm, lng, lnb, wup1, wdn1, win, alpha, na_w, tm)
    tmc = _pick_tile(bsz * l_ctx, 512)
    _, qkv_c, xs_c = _ffn_in(ctx.reshape(bsz * l_ctx, d), mod[bsz:bsz + 1], None, lng, lnb, wup1, wdn1, win,
                             alpha, na_w, tmc)

    tables = _s5_tables(s5_a_re[0], s5_a_im[0], s5_log_dt[0], s5_b_re[0], s5_b_im[0], s5_c_re[0], s5_c_im[0],
                        s5_d[0])
    y_s5 = _s5(xs, xs_c, *tables, bsz)

    y_na = _na(qkv.reshape(bsz, seq, 3 * na_w), qkv_c.reshape(bsz, l_ctx, 3 * na_w), _na_bias(na_rpb[0]),
               na_w).reshape(bsz * seq, na_w)

    out = _out_ffn(x1, y_na, y_s5, mod, seq // tm, lng, lnb, s5_w_glu[0].astype(cd), s5_b_glu[0][None, :],
                   w_out[0].astype(cd), wup2, wdn2, alpha, tm)
    return out.reshape(bsz, seq, d)
```

```python
import functools

import jax
import jax.numpy as jnp
import numpy as np
from jax import lax
from jax.experimental import pallas as pl
from jax.experimental.pallas import tpu as pltpu

GRID_W = 64
NA_DH = 64
NA_KR = 8
NA_KC = 16
S5_CG = 16
S5_P = 64
LN_EPS = 1e-6
N_MOD = 9

CHUNK = 16
NA_QROWS = 4
NA_KROWS = 12
NA_BLOCK = NA_QROWS * GRID_W
NA_STEP_BLOCKS = 2
MASK_VALUE = -1e30
LOG2_E = 1.4426950408889634
SCAN_UNROLL = 4

MXU_TILE = 256
LANES = 128
SUBLANES = 8
VMEM_LIMIT_BYTES = 56 * 1024 * 1024

_MXU_DTYPE = jnp.bfloat16
_HI = lax.Precision.HIGHEST


def _mm(a, b):
    return jnp.dot(a, b, preferred_element_type=jnp.float32)


def _mm_nt(a, b):
    return lax.dot_general(a, b, (((1,), (1,)), ((), ())), preferred_element_type=jnp.float32)


def _const_spec(shape):
    nd = len(shape)
    return pl.BlockSpec(shape, lambda *_: (0,) * nd, pipeline_mode=pl.Buffered(1))


def _layer_norm(r, g, b):
    mu = jnp.mean(r, axis=-1, keepdims=True)
    d = r - mu
    var = jnp.mean(d * d, axis=-1, keepdims=True)
    return d * lax.rsqrt(var + LN_EPS) * g + b


def _swiglu(h, wup_ref, wdn_ref, p_s, between=()):
    ff = wdn_ref.shape[0]
    n_chunks = ff // MXU_TILE
    slots = [(k + 1) * n_chunks // (len(between) + 1) for k in range(len(between))]
    for j in range(n_chunks):
        for slot, fn in zip(slots, between):
            if slot == j:
                fn()
        lo = j * MXU_TILE
        a = _mm(h, wup_ref[:, lo:lo + MXU_TILE])
        g = _mm(h, wup_ref[:, ff + lo:ff + lo + MXU_TILE])
        p_s[:, lo:lo + MXU_TILE] = (g * jax.nn.sigmoid(g) * a).astype(p_s.dtype)
    return _mm(p_s[...], wdn_ref[...])


def _block_transpose(tiles):
    per = len(tiles)
    blk = lax.broadcasted_iota(jnp.int32, (1, LANES), 1) // S5_CG
    tiles = list(tiles)
    d = per // 2
    while d:
        upper = (blk & d) != 0
        for a in range(per):
            if a & d:
                continue
            lo, hi = tiles[a], tiles[a + d]
            tiles[a] = jnp.where(upper, pltpu.roll(hi, S5_CG * d, axis=1), lo)
            tiles[a + d] = jnp.where(upper, hi, pltpu.roll(lo, LANES - S5_CG * d, axis=1))
        d //= 2
    return tiles


def _tokens_to_chunks(u_s, xs_ref):
    n_ch = u_s.shape[1] // CHUNK
    per = LANES // S5_CG
    for v in range(u_s.shape[0]):
        for w in range(CHUNK // per):
            by_group = _block_transpose([u_s[v, pl.ds(per * w + jj, n_ch, stride=CHUNK), :] for jj in range(per)])
            for gg in range(per):
                xs_ref[per * v + gg, :, w * LANES:(w + 1) * LANES] = by_group[gg].astype(xs_ref.dtype)


def _chunks_to_tokens(ys_ref, y_s):
    n_ch = y_s.shape[1] // CHUNK
    per = LANES // S5_CG
    for v in range(y_s.shape[0]):
        for w in range(CHUNK // per):
            by_phase = _block_transpose([ys_ref[per * v + gg, :, w * LANES:(w + 1) * LANES] for gg in range(per)])
            for jj in range(per):
                y_s[v, pl.ds(per * w + jj, n_ch, stride=CHUNK), :] = by_phase[jj]


def _ada_kernel(c_ref, w_ref, b_ref, o_ref):
    cc = c_ref[...]
    s = cc * jax.nn.sigmoid(cc)
    o_ref[...] = jnp.dot(s, w_ref[...], precision=_HI,
                         preferred_element_type=jnp.float32) + b_ref[...]


def _ada(cc, w, b, bn):
    rows, d = cc.shape
    n = w.shape[1]
    return pl.pallas_call(
        _ada_kernel,
        grid=(n // bn,),
        in_specs=[pl.BlockSpec((rows, d), lambda j: (0, 0)),
                  pl.BlockSpec((d, bn), lambda j: (0, j)),
                  pl.BlockSpec((1, bn), lambda j: (0, j))],
        out_specs=pl.BlockSpec((rows, bn), lambda j: (0, j)),
        out_shape=jax.ShapeDtypeStruct((rows, n), jnp.float32),
        compiler_params=pltpu.CompilerParams(dimension_semantics=("arbitrary",),
                                             vmem_limit_bytes=VMEM_LIMIT_BYTES),
        name="ada",
    )(cc, w, b)


def _ffn_in_kernel(alpha, na_w, x_ref, mod_ref, lng_ref, lnb_ref, wup_ref, wdn_ref, win_ref,
                   x1_ref, qk_ref, vt_ref, xs_ref, p_s, u_s):
    x = x_ref[...]
    mod = mod_ref[0]
    h = (x * (1.0 + mod[1:2]) + mod[0:1]).astype(p_s.dtype)
    y = _swiglu(h, wup_ref, wdn_ref, p_s)
    x1 = _layer_norm(alpha * x + (0.5 * mod[2:3]) * y, lng_ref[0:1], lnb_ref[0:1])
    x1_ref[...] = x1
    h2 = (x1 * (1.0 + mod[4:5]) + mod[3:4]).astype(p_s.dtype)
    pu = _mm(h2, win_ref[:, 3 * na_w:])
    for v in range(u_s.shape[0]):
        u_s[v] = pu[:, v * LANES:(v + 1) * LANES]
    _tokens_to_chunks(u_s, xs_ref)
    pr = _mm(h2, win_ref[:, :3 * na_w])
    qk_ref[:, :na_w] = (pr[:, :na_w] * (NA_DH ** -0.5 * LOG2_E)).astype(qk_ref.dtype)
    qk_ref[:, na_w:] = pr[:, na_w:2 * na_w].astype(qk_ref.dtype)
    for t in range(vt_ref.shape[0]):
        vt_ref[t] = pr[t * NA_BLOCK:(t + 1) * NA_BLOCK, 2 * na_w:].T.astype(vt_ref.dtype)


def _ffn_in(x2d, mod, tiles_per_mod, lng, lnb, wup, wdn, win, alpha, na_w, tm):
    t, d = x2d.shape
    ff = wdn.shape[0]
    ncol = win.shape[1]
    s5_w = ncol - 3 * na_w
    groups, cw = s5_w // S5_CG, CHUNK * S5_CG
    if tiles_per_mod is None:
        mod_map = lambda i: (0, 0, 0)
    else:
        mod_map = lambda i: (i // tiles_per_mod, 0, 0)
    return pl.pallas_call(
        functools.partial(_ffn_in_kernel, alpha, na_w),
        grid=(t // tm,),
        in_specs=[pl.BlockSpec((tm, d), lambda i: (i, 0)),
                  pl.BlockSpec((1, N_MOD, d), mod_map),
                  _const_spec(lng.shape), _const_spec(lnb.shape),
                  _const_spec(wup.shape), _const_spec(wdn.shape), _const_spec(win.shape)],
        out_specs=[pl.BlockSpec((tm, d), lambda i: (i, 0)),
                   pl.BlockSpec((tm, 2 * na_w), lambda i: (i, 0)),
                   pl.BlockSpec((tm // NA_BLOCK, na_w, NA_BLOCK), lambda i: (i, 0, 0)),
                   pl.BlockSpec((groups, tm // CHUNK, cw), lambda i: (0, i, 0))],
        out_shape=[jax.ShapeDtypeStruct((t, d), jnp.float32),
                   jax.ShapeDtypeStruct((t, 2 * na_w), _MXU_DTYPE),
                   jax.ShapeDtypeStruct((t // NA_BLOCK, na_w, NA_BLOCK), _MXU_DTYPE),
                   jax.ShapeDtypeStruct((groups, t // CHUNK, cw), jnp.float32)],
        scratch_shapes=[pltpu.VMEM((tm, ff), _MXU_DTYPE), pltpu.VMEM((s5_w // LANES, tm, LANES), jnp.float32)],
        compiler_params=pltpu.CompilerParams(dimension_semantics=("arbitrary",),
                                             vmem_limit_bytes=VMEM_LIMIT_BYTES),
        name="ffn_in",
    )(x2d, mod, lng, lnb, wup, wdn, win)


def _toeplitz_kernel(lhs_ref, rhs_ref, o_ref):
    width = o_ref.shape[2]
    lane = lax.broadcasted_iota(jnp.int32, (1, width), 1)
    k_f = jnp.dot(lhs_ref[0, 0], rhs_ref[0, 0], precision=_HI, preferred_element_type=jnp.float32)
    k_b = jnp.dot(lhs_ref[1, 0], rhs_ref[1, 0], precision=_HI, preferred_element_type=jnp.float32)
    for i in range(CHUNK):
        lo, hi = i * S5_CG, (i + 1) * S5_CG
        fwd = k_f if lo == 0 else jnp.where(lane >= lo, pltpu.roll(k_f, lo, axis=1), 0.0)
        bwd = k_b if hi == width else jnp.where(lane < hi, pltpu.roll(k_b, hi, axis=1), 0.0)
        o_ref[0, lo:hi, :] = (fwd + bwd).astype(o_ref.dtype)


def _toeplitz(lhs, rhs):
    _, g, cg, kdim = lhs.shape
    width = rhs.shape[3]
    return pl.pallas_call(
        _toeplitz_kernel,
        grid=(g,),
        in_specs=[pl.BlockSpec((2, 1, cg, kdim), lambda k: (0, k, 0, 0)),
                  pl.BlockSpec((2, 1, kdim, width), lambda k: (0, k, 0, 0))],
        out_specs=pl.BlockSpec((1, width, width), lambda k: (k, 0, 0)),
        out_shape=jax.ShapeDtypeStruct((g, width, width), _MXU_DTYPE),
        compiler_params=pltpu.CompilerParams(dimension_semantics=("arbitrary",)),
        name="s5_toeplitz",
    )(lhs, rhs)


def _s5_tables(a_re, a_im, log_dt, b_re, b_im, c_re, c_im, d_skip):
    g, p = a_re.shape[1], a_re.shape[2]
    f32 = jnp.float32
    a_re, a_im = a_re.astype(f32), a_im.astype(f32)
    dt = jnp.exp(log_dt.astype(f32))[..., None]
    lr, li = a_re * dt, a_im * dt

    def powers(expo, p_last):
        e = jnp.asarray(expo, f32)
        e = e[:, None, :, None] if p_last else e[:, None, None, :]
        x_r = lr[:, :, None, :] if p_last else lr[..., None]
        x_i = li[:, :, None, :] if p_last else li[..., None]
        mag = jnp.exp(e * x_r)
        return mag * jnp.cos(e * x_i), mag * jnp.sin(e * x_i)

    ab_r, ab_i = jnp.exp(lr) * jnp.cos(li), jnp.exp(lr) * jnp.sin(li)
    den = a_re ** 2 + a_im ** 2
    nr, ni = ab_r - 1.0, ab_i
    sr = (nr * a_re + ni * a_im) / den
    si = (ni * a_re - nr * a_im) / den
    bb_r = sr[..., None] * b_re - si[..., None] * b_im
    bb_i = sr[..., None] * b_im + si[..., None] * b_re
    bt_r, bt_i = jnp.swapaxes(bb_r, 2, 3), jnp.swapaxes(bb_i, 2, 3)
    ct_r, ct_i = jnp.swapaxes(c_re, 2, 3).astype(f32), jnp.swapaxes(c_im, 2, 3).astype(f32)

    asc = np.arange(CHUNK)
    desc = asc[::-1].copy()

    q_r, q_i = powers(np.stack([asc, desc]), False)
    rhs_r = ct_r[:, :, :, None, :] * q_r[..., None] - ct_i[:, :, :, None, :] * q_i[..., None]
    rhs_i = ct_r[:, :, :, None, :] * q_i[..., None] + ct_i[:, :, :, None, :] * q_r[..., None]
    rhs = jnp.concatenate([rhs_r, rhs_i], axis=2).reshape(2, g, 2 * p, CHUNK * S5_CG)
    toep = _toeplitz(jnp.concatenate([bt_r, -bt_i], axis=3), rhs)

    w_r, w_i = powers(np.stack([desc, asc]), True)
    bs_r = (w_r[:, :, :, None, :] * bt_r[:, :, None] - w_i[:, :, :, None, :] * bt_i[:, :, None])
    bs_i = (w_r[:, :, :, None, :] * bt_i[:, :, None] + w_i[:, :, :, None, :] * bt_r[:, :, None])
    bs_r = bs_r.reshape(2, g, CHUNK * S5_CG, p)
    bs_i = bs_i.reshape(2, g, CHUNK * S5_CG, p)
    v_r, v_i = powers(np.stack([asc + 1, desc + 1]), False)
    cs_r = (ct_r[:, :, :, None, :] * v_r[..., None] - ct_i[:, :, :, None, :] * v_i[..., None])
    cs_i = -(ct_r[:, :, :, None, :] * v_i[..., None] + ct_i[:, :, :, None, :] * v_r[..., None])
    cs_r = cs_r.reshape(2, g, p, CHUNK * S5_CG)
    cs_i = cs_i.reshape(2, g, p, CHUNK * S5_CG)
    even = jnp.asarray((np.arange(g) % 2) == 0)[None, :, None, None]
    bs = jnp.concatenate([jnp.where(even, bs_r, 0.0), jnp.where(even, 0.0, bs_r),
                          jnp.where(even, bs_i, 0.0), jnp.where(even, 0.0, bs_i)], axis=3)
    cs = jnp.concatenate([jnp.where(even, cs_r, 0.0), jnp.where(even, 0.0, cs_r),
                          jnp.where(even, cs_i, 0.0), jnp.where(even, 0.0, cs_i)], axis=2)

    e_r, e_i = jnp.exp(CHUNK * lr), CHUNK * li
    half = 2 * p
    a_pow = jnp.stack([(e_r * jnp.cos(e_i)).reshape(2, g // 2, half),
                       (e_r * jnp.sin(e_i)).reshape(2, g // 2, half)], axis=2)
    dvec = jnp.tile(d_skip.astype(f32)[:, None, :], (1, CHUNK, 1)).reshape(g, 1, CHUNK * S5_CG)
    return toep, bs.astype(_MXU_DTYPE), cs.astype(_MXU_DTYPE), a_pow, dvec


def _row_pitch(n):
    return n + (4 - n) % 8


def _s5_kernel(nb, xl_ref, xc_ref, toep_ref, bs_ref, cs_ref, apow_ref, dvec_ref, y_ref, zc_s, zl_s, h_s):
    rows_l = xl_ref.shape[1]
    rows_c = xc_ref.shape[1]
    n_l, n_c = rows_l // nb, rows_c // nb
    pitch_l, pitch_c = _row_pitch(n_l), _row_pitch(n_c)
    half = zl_s.shape[3]
    xl = [xl_ref[g] for g in range(2)]
    xlb = [v.astype(_MXU_DTYPE) for v in xl]
    xcb = [xc_ref[g].astype(_MXU_DTYPE) for g in range(2)]

    for d in range(2):
        zc = _mm(xcb[0], bs_ref[d, 0]) + _mm(xcb[1], bs_ref[d, 1])
        zl = _mm(xlb[0], bs_ref[d, 0]) + _mm(xlb[1], bs_ref[d, 1])
        for ri in range(2):
            for b in range(nb):
                zc_s[d, ri, b * pitch_c:b * pitch_c + n_c, :] = zc[b * n_c:(b + 1) * n_c, ri * half:(ri + 1) * half]
                zl_s[d, ri, b * pitch_l:b * pitch_l + n_l, :] = zl[b * n_l:(b + 1) * n_l, ri * half:(ri + 1) * half]

    a_r = [jnp.broadcast_to(apow_ref[d, 0, 0:1, :], (nb, half)) for d in range(2)]
    a_i = [jnp.broadcast_to(apow_ref[d, 0, 1:2, :], (nb, half)) for d in range(2)]

    def step(d, n, latent, h_r, h_i):
        if latent:
            h_s[d, 0, pl.ds(n, nb, stride=pitch_l), :] = h_r
            h_s[d, 1, pl.ds(n, nb, stride=pitch_l), :] = h_i
            z_r = zl_s[d, 0, pl.ds(n, nb, stride=pitch_l), :]
            z_i = zl_s[d, 1, pl.ds(n, nb, stride=pitch_l), :]
        else:
            z_r = zc_s[d, 0, pl.ds(n, nb, stride=pitch_c), :]
            z_i = zc_s[d, 1, pl.ds(n, nb, stride=pitch_c), :]
        return (a_r[d] * h_r - a_i[d] * h_i + z_r,
                a_r[d] * h_i + a_i[d] * h_r + z_i)

    def ctx_body(n, carry):
        f_r, f_i, b_r, b_i = carry
        f_r, f_i = step(0, n, False, f_r, f_i)
        b_r, b_i = step(1, n_c - 1 - n, False, b_r, b_i)
        return f_r, f_i, b_r, b_i

    def lat_body(n, carry):
        f_r, f_i, b_r, b_i = carry
        f_r, f_i = step(0, n, True, f_r, f_i)
        b_r, b_i = step(1, n_l - 1 - n, True, b_r, b_i)
        return f_r, f_i, b_r, b_i

    zero = jnp.zeros((nb, half), jnp.float32)
    carry = lax.fori_loop(0, n_c, ctx_body, (zero, zero, zero, zero), unroll=SCAN_UNROLL)
    lax.fori_loop(0, n_l, lat_body, carry, unroll=SCAN_UNROLL)

    def plane(d, ri):
        return jnp.concatenate([h_s[d, ri, b * pitch_l:b * pitch_l + n_l, :] for b in range(nb)], axis=0)

    hb = [jnp.concatenate([plane(d, 0), plane(d, 1)], axis=1).astype(_MXU_DTYPE) for d in range(2)]
    for g in range(2):
        y = _mm(xlb[g], toep_ref[g]) + _mm(hb[0], cs_ref[0, g]) + _mm(hb[1], cs_ref[1, g])
        y_ref[g] = y + xl[g] * dvec_ref[g]


def _s5(xl, xc, toep, bs, cs, a_pow, dvec, nb):
    g, rows_l, w = xl.shape
    rows_c = xc.shape[1]
    sw = bs.shape[3]
    return pl.pallas_call(
        functools.partial(_s5_kernel, nb),
        grid=(g // 2,),
        in_specs=[pl.BlockSpec((2, rows_l, w), lambda k: (k, 0, 0)),
                  pl.BlockSpec((2, rows_c, w), lambda k: (k, 0, 0)),
                  pl.BlockSpec((2, w, w), lambda k: (k, 0, 0)),
                  pl.BlockSpec((2, 2, w, sw), lambda k: (0, k, 0, 0)),
                  pl.BlockSpec((2, 2, sw, w), lambda k: (0, k, 0, 0)),
                  pl.BlockSpec((2, 1, 2, sw // 2), lambda k: (0, k, 0, 0)),
                  pl.BlockSpec((2, 1, w), lambda k: (k, 0, 0))],
        out_specs=pl.BlockSpec((2, rows_l, w), lambda k: (k, 0, 0)),
        out_shape=jax.ShapeDtypeStruct((g, rows_l, w), jnp.float32),
        scratch_shapes=[pltpu.VMEM((2, 2, nb * _row_pitch(rows_c // nb), sw // 2), jnp.float32),
                        pltpu.VMEM((2, 2, nb * _row_pitch(rows_l // nb), sw // 2), jnp.float32),
                        pltpu.VMEM((2, 2, nb * _row_pitch(rows_l // nb), sw // 2), jnp.float32)],
        compiler_params=pltpu.CompilerParams(dimension_semantics=("arbitrary",),
                                             vmem_limit_bytes=VMEM_LIMIT_BYTES),
        name="s5",
    )(xl, xc, toep, bs, cs, a_pow, dvec)


def _na_row_plan():
    last_q = NA_KROWS - NA_QROWS
    plan = []
    for r0s, qrs in ((lambda lr: 0, lambda lr: lr),
                     (lambda lr: lr, lambda lr: NA_KR // 2 + lr),
                     (lambda lr: last_q - NA_KR // 2, lambda lr: last_q + lr)):
        plan.append([[kl - qrs(lr) + (NA_KR - 1) if r0s(lr) <= kl < r0s(lr) + NA_KR else None
                      for kl in range(NA_KROWS)] for lr in range(NA_QROWS)])
    return plan


def _na_bias_kernel(plan, col_ref, o_ref):
    lane = lax.broadcasted_iota(jnp.int32, (1, 2 * GRID_W), 1)
    masked = jnp.full((GRID_W, 2 * GRID_W), MASK_VALUE, jnp.float32)
    for kind in range(len(plan)):
        for kl in range(NA_KROWS):
            for lp in range(NA_QROWS // 2):
                a0, a1 = plan[kind][2 * lp][kl], plan[kind][2 * lp + 1][kl]
                t0 = masked if a0 is None else col_ref[0, a0]
                t1 = masked if a1 is None else col_ref[0, a1]
                tile = t0 if a0 == a1 else jnp.where(lane < GRID_W, t0, t1)
                o_ref[kind, 0, kl * GRID_W:(kl + 1) * GRID_W, lp * 2 * GRID_W:(lp + 1) * 2 * GRID_W] = tile


def _na_bias(rpb):
    heads = rpb.shape[0]
    kc = np.arange(GRID_W)[:, None]
    qc = np.arange(GRID_W)[None, :]
    c0 = np.clip(qc - NA_KC // 2, 0, GRID_W - NA_KC)
    col_ok = (kc >= c0) & (kc < c0 + NA_KC)
    sel_c = ((kc - qc + (NA_KC - 1))[:, :, None] == np.arange(2 * NA_KC - 1)) & col_ok[:, :, None]
    by_col = jnp.einsum('hab,kqb->hakq', rpb.astype(jnp.float32), jnp.asarray(sel_c, jnp.float32), precision=_HI)
    by_col = jnp.where(jnp.asarray(col_ok), by_col * LOG2_E, MASK_VALUE)
    by_col = jnp.concatenate([by_col, by_col], axis=3)
    plan = _na_row_plan()
    nk = NA_KROWS * GRID_W
    return pl.pallas_call(
        functools.partial(_na_bias_kernel, plan),
        grid=(heads,),
        in_specs=[pl.BlockSpec((1,) + by_col.shape[1:], lambda h: (h, 0, 0, 0))],
        out_specs=pl.BlockSpec((len(plan), 1, nk, NA_BLOCK), lambda h: (0, h, 0, 0)),
        out_shape=jax.ShapeDtypeStruct((len(plan), heads, nk, NA_BLOCK), jnp.float32),
        compiler_params=pltpu.CompilerParams(dimension_semantics=("arbitrary",)),
        name="na_bias",
    )(by_col)


def _row_max(x):
    while x.shape[0] % (2 * SUBLANES) == 0:
        half = x.shape[0] // 2
        x = jnp.maximum(x[:half], x[half:])
    return jnp.max(x, axis=0, keepdims=True)


def _na_kernel(rows, q_ref, k_ref, vt_ref, kc_ref, vct_ref, bias_ref, o_ref):
    n_rb = rows // NA_QROWS
    nk = NA_KROWS * GRID_W
    pair = 2 * NA_DH
    lane_first = lax.broadcasted_iota(jnp.int32, (1, pair), 1) < NA_DH
    row_first = lax.broadcasted_iota(jnp.int32, (pair, 1), 0) < NA_DH
    n_heads = q_ref.shape[2] // NA_DH
    n_blocks = q_ref.shape[1] // NA_BLOCK

    def block(j):
        rb = pl.program_id(1) * n_blocks + j
        kind = jnp.where(rb == 0, 0, jnp.where(rb == n_rb - 1, 2, 1))
        first_row = jnp.clip(rb * NA_QROWS - NA_KR // 2, 0, rows - NA_KROWS)
        return kind, pl.multiple_of(first_row * GRID_W, GRID_W), first_row // NA_QROWS

    geometry = [block(j) for j in range(n_blocks)]

    def scores(u):
        j, h = divmod(u, n_heads)
        kind, start, _ = geometry[j]
        cols = slice(h // 2 * pair, (h // 2 + 1) * pair)
        lane_mine = lane_first if h % 2 == 0 else jnp.logical_not(lane_first)
        qm = jnp.where(lane_mine, q_ref[0, j * NA_BLOCK:(j + 1) * NA_BLOCK, cols],
                       jnp.zeros((NA_BLOCK, pair), q_ref.dtype))
        s_w = _mm_nt(k_ref[0, pl.ds(start, nk), cols], qm) + bias_ref[kind, h]
        s_c = _mm_nt(kc_ref[0, :, cols], qm)
        return s_w, s_c

    def weights(s_w, s_c):
        mx = jnp.maximum(_row_max(s_w), _row_max(s_c))
        return jnp.exp2(s_w - mx).astype(vt_ref.dtype), jnp.exp2(s_c - mx).astype(vt_ref.dtype)

    def values(u, p_w, p_c):
        j, h = divmod(u, n_heads)
        blk0 = geometry[j][2]
        cols = slice(h // 2 * pair, (h // 2 + 1) * pair)
        row_mine = row_first if h % 2 == 0 else jnp.logical_not(row_first)
        vtw = jnp.concatenate([vt_ref[blk0 + t, cols, :] for t in range(nk // NA_BLOCK)], axis=1)
        vtc = jnp.concatenate([vct_ref[t, cols, :] for t in range(vct_ref.shape[0])], axis=1)
        o = (_mm(jnp.where(row_mine, vtw, jnp.ones_like(vtw)), p_w)
             + _mm(jnp.where(row_mine, vtc, jnp.ones_like(vtc)), p_c))
        num, den = (o[:NA_DH], o[NA_DH:]) if h % 2 == 0 else (o[NA_DH:], o[:NA_DH])
        return num / den

    n_units = n_blocks * n_heads
    s = {0: scores(0), 1: scores(1)}
    p = {0: weights(*s.pop(0))}
    outs = []
    for u in range(n_units):
        if u + 2 < n_units:
            s[u + 2] = scores(u + 2)
        if u + 1 < n_units:
            p[u + 1] = weights(*s.pop(u + 1))
        outs.append(values(u, *p.pop(u)))
        if len(outs) == n_heads:
            j = u // n_heads
            o_ref[0, j * NA_BLOCK:(j + 1) * NA_BLOCK, :] = jnp.concatenate(outs, axis=0).T.astype(o_ref.dtype)
            outs = []


def _na(qk, vt, qk_c, vt_c, bias, na_w):
    b, s, _ = qk.shape
    l = qk_c.shape[1]
    rows = s // GRID_W
    tq = NA_STEP_BLOCKS * NA_BLOCK
    return pl.pallas_call(
        functools.partial(_na_kernel, rows),
        grid=(b, s // tq),
        in_specs=[pl.BlockSpec((1, tq, na_w), lambda bi, rb: (bi, rb, 0)),
                  pl.BlockSpec((1, s, na_w), lambda bi, rb: (bi, 0, 1)),
                  pl.BlockSpec((s // NA_BLOCK, na_w, NA_BLOCK), lambda bi, rb: (bi, 0, 0)),
                  pl.BlockSpec((1, l, na_w), lambda bi, rb: (bi, 0, 1)),
                  pl.BlockSpec((l // NA_BLOCK, na_w, NA_BLOCK), lambda bi, rb: (bi, 0, 0)),
                  _const_spec(bias.shape)],
        out_specs=pl.BlockSpec((1, tq, na_w), lambda bi, rb: (bi, rb, 0)),
        out_shape=jax.ShapeDtypeStruct((b, s, na_w), _MXU_DTYPE),
        compiler_params=pltpu.CompilerParams(
            dimension_semantics=("arbitrary", "arbitrary"),
            vmem_limit_bytes=VMEM_LIMIT_BYTES),
        name="na",
    )(qk, qk, vt, qk_c, vt_c, bias)


def _out_ffn_kernel(alpha, x1_ref, yna_ref, ys5_ref, mod_a_ref, mod_b_ref, lng_ref, lnb_ref, wglu_ref, bglu_ref,
                    wout_ref, wup_ref, wdn_ref, o_ref, p_s, y_s, h_s, x2_s):
    @pl.when(pl.program_id(0) == 0)
    def _():
        h_s[...] = jnp.zeros_like(h_s)
        x2_s[...] = jnp.zeros_like(x2_s)

    mod_a = mod_a_ref[0]
    na_w = yna_ref.shape[1]
    stage_a = {}

    def relayout():
        _chunks_to_tokens(ys5_ref, y_s)

    def gated():
        gl = jax.nn.gelu(jnp.concatenate([y_s[v] for v in range(y_s.shape[0])], axis=1))
        gate = jax.nn.sigmoid(_mm(gl.astype(p_s.dtype), wglu_ref[...]) + bglu_ref[...])
        stage_a["glu"] = (gl * gate).astype(p_s.dtype)

    def mixed():
        y = _mm(yna_ref[...], wout_ref[:na_w, :]) + _mm(stage_a["glu"], wout_ref[na_w:, :])
        x2 = _layer_norm(alpha * x1_ref[...] + mod_a[5:6] * y, lng_ref[1:2], lnb_ref[1:2])
        stage_a["x2"] = x2
        stage_a["h"] = (x2 * (1.0 + mod_a[7:8]) + mod_a[6:7]).astype(h_s.dtype)

    mod_b = mod_b_ref[0]
    f = _swiglu(h_s[...], wup_ref, wdn_ref, p_s, between=(relayout, gated, mixed))
    o_ref[...] = _layer_norm(alpha * x2_s[...] + (0.5 * mod_b[8:9]) * f, lng_ref[2:3], lnb_ref[2:3])

    x2_s[...] = stage_a["x2"]
    h_s[...] = stage_a["h"]


def _out_ffn(x1, yna, ys5, mod, tiles_per_mod, lng, lnb, wglu, bglu, wout, wup, wdn, alpha, tm):
    t, d = x1.shape
    ff = wdn.shape[0]
    na_w = yna.shape[1]
    groups, _, cw = ys5.shape
    s5_w = groups * S5_CG
    n = t // tm

    def cur(i):
        return jnp.minimum(i, n - 1)

    def prev(i):
        return jnp.maximum(i - 1, 0)

    return pl.pallas_call(
        functools.partial(_out_ffn_kernel, alpha),
        grid=(n + 1,),
        in_specs=[pl.BlockSpec((tm, d), lambda i: (cur(i), 0)),
                  pl.BlockSpec((tm, na_w), lambda i: (cur(i), 0)),
                  pl.BlockSpec((groups, tm // CHUNK, cw), lambda i: (0, cur(i), 0)),
                  pl.BlockSpec((1, N_MOD, d), lambda i: (cur(i) // tiles_per_mod, 0, 0)),
                  pl.BlockSpec((1, N_MOD, d), lambda i: (prev(i) // tiles_per_mod, 0, 0)),
                  _const_spec(lng.shape), _const_spec(lnb.shape),
                  _const_spec(wglu.shape), _const_spec(bglu.shape), _const_spec(wout.shape),
                  _const_spec(wup.shape), _const_spec(wdn.shape)],
        out_specs=pl.BlockSpec((tm, d), lambda i: (prev(i), 0)),
        out_shape=jax.ShapeDtypeStruct((t, d), jnp.float32),
        scratch_shapes=[pltpu.VMEM((tm, ff), _MXU_DTYPE), pltpu.VMEM((s5_w // LANES, tm, LANES), jnp.float32),
                        pltpu.VMEM((tm, d), _MXU_DTYPE), pltpu.VMEM((tm, d), jnp.float32)],
        compiler_params=pltpu.CompilerParams(dimension_semantics=("arbitrary",),
                                             vmem_limit_bytes=VMEM_LIMIT_BYTES),
        name="out_ffn",
    )(x1, yna, ys5, mod, mod, lng, lnb, wglu, bglu, wout, wup, wdn)


def _pick_tile(n, target):
    t = min(n, target)
    while n % t:
        t //= 2
    return t


def kernel(x, c, ctx, c_ctx, w_ada, b_ada, ln_g, ln_b, ffn1_w_up, ffn1_w_down, w_in, na_rpb, s5_a_re, s5_a_im, s5_log_dt, s5_b_re, s5_b_im, s5_c_re, s5_c_im, s5_d, s5_w_glu, s5_b_glu, w_out, ffn2_w_up, ffn2_w_down):
    depth = w_ada.shape[0]
    assert depth == 1, "single-layer stack only"
    bsz, seq, d = x.shape
    l_ctx = ctx.shape[1]
    s5_w = s5_w_glu.shape[1]
    na_w = w_out.shape[1] - s5_w
    groups = s5_w // S5_CG
    rows = seq // GRID_W
    assert bsz % 8 == 0 and seq % (NA_STEP_BLOCKS * NA_BLOCK) == 0 and rows >= NA_KROWS
    assert l_ctx % NA_BLOCK == 0 and groups % 2 == 0 and 2 * s5_a_re.shape[3] == LANES
    assert na_w % (2 * NA_DH) == 0 and ffn1_w_down.shape[1] % MXU_TILE == 0
    alpha = (2.0 * depth) ** 0.25
    cd = _MXU_DTYPE

    n_rows = -(-(bsz + 1) // 8) * 8
    cc = jnp.zeros((n_rows, d), jnp.float32).at[:bsz].set(c).at[bsz].set(c_ctx)
    n_ada = w_ada.shape[2]
    mod = _ada(cc, w_ada[0], b_ada[0][None, :], _pick_tile(n_ada, 1152)).reshape(n_rows, N_MOD, d)

    lng, lnb = ln_g[0], ln_b[0]
    wup1, wdn1 = ffn1_w_up[0].astype(cd), ffn1_w_down[0].astype(cd)
    wup2, wdn2 = ffn2_w_up[0].astype(cd), ffn2_w_down[0].astype(cd)
    win = w_in[0].astype(cd)

    tm = _pick_tile(seq, 512)
    x1, qk, vt, xs = _ffn_in(x.reshape(bsz * seq, d), mod, seq // tm, lng, lnb, wup1, wdn1, win, alpha, na_w, tm)
    tmc = _pick_tile(bsz * l_ctx, 512)
    _, qk_c, vt_c, xs_c = _ffn_in(ctx.reshape(bsz * l_ctx, d), mod[bsz:bsz + 1], None, lng, lnb, wup1, wdn1, win,
                                  alpha, na_w, tmc)

    tables = _s5_tables(s5_a_re[0], s5_a_im[0], s5_log_dt[0], s5_b_re[0], s5_b_im[0], s5_c_re[0], s5_c_im[0],
                        s5_d[0])
    y_s5 = _s5(xs, xs_c, *tables, bsz)

    y_na = _na(qk.reshape(bsz, seq, 2 * na_w), vt, qk_c.reshape(bsz, l_ctx, 2 * na_w), vt_c, _na_bias(na_rpb[0]),
               na_w).reshape(bsz * seq, na_w)

    out = _out_ffn(x1, y_na, y_s5, mod, seq // tm, lng, lnb, s5_w_glu[0].astype(cd), s5_b_glu[0][None, :],
                   w_out[0].astype(cd), wup2, wdn2, alpha, tm)
    return out.reshape(bsz, seq, d)
```

```python
import functools

import jax
import jax.numpy as jnp
import numpy as np
from jax import lax
from jax.experimental import pallas as pl
from jax.experimental.pallas import tpu as pltpu

GRID_W = 64
NA_DH = 64
NA_KR = 8
NA_KC = 16
S5_CG = 16
S5_P = 64
LN_EPS = 1e-6
N_MOD = 9

CHUNK = 16
NA_QROWS = 4
NA_KROWS = 12
NA_BLOCK = NA_QROWS * GRID_W
NA_STEP_BLOCKS = 4
MASK_VALUE = -1e30
LOG2_E = 1.4426950408889634
SCAN_UNROLL = 4

MXU_TILE = 256
LANES = 128
SUBLANES = 8
VMEM_LIMIT_BYTES = 56 * 1024 * 1024

_MXU_DTYPE = jnp.bfloat16
_HI = lax.Precision.HIGHEST


def _mm(a, b):
    return jnp.dot(a, b, preferred_element_type=jnp.float32)


def _mm_nt(a, b):
    return lax.dot_general(a, b, (((1,), (1,)), ((), ())), preferred_element_type=jnp.float32)


def _const_spec(shape):
    nd = len(shape)
    return pl.BlockSpec(shape, lambda *_: (0,) * nd, pipeline_mode=pl.Buffered(1))


def _layer_norm(r, g, b):
    mu = jnp.mean(r, axis=-1, keepdims=True)
    d = r - mu
    var = jnp.mean(d * d, axis=-1, keepdims=True)
    return d * lax.rsqrt(var + LN_EPS) * g + b


def _swiglu(h, wup_ref, wdn_ref, p_s, between=()):
    ff = wdn_ref.shape[0]
    n_chunks = ff // MXU_TILE
    slots = [(k + 1) * n_chunks // (len(between) + 1) for k in range(len(between))]
    for j in range(n_chunks):
        for slot, fn in zip(slots, between):
            if slot == j:
                fn()
        lo = j * MXU_TILE
        a = _mm(h, wup_ref[:, lo:lo + MXU_TILE])
        g = _mm(h, wup_ref[:, ff + lo:ff + lo + MXU_TILE])
        p_s[:, lo:lo + MXU_TILE] = (g * jax.nn.sigmoid(g) * a).astype(p_s.dtype)
    return _mm(p_s[...], wdn_ref[...])


def _block_transpose(tiles):
    per = len(tiles)
    blk = lax.broadcasted_iota(jnp.int32, (1, LANES), 1) // S5_CG
    tiles = list(tiles)
    d = per // 2
    while d:
        upper = (blk & d) != 0
        for a in range(per):
            if a & d:
                continue
            lo, hi = tiles[a], tiles[a + d]
            tiles[a] = jnp.where(upper, pltpu.roll(hi, S5_CG * d, axis=1), lo)
            tiles[a + d] = jnp.where(upper, hi, pltpu.roll(lo, LANES - S5_CG * d, axis=1))
        d //= 2
    return tiles


def _tokens_to_chunks(u_s, xs_ref):
    n_ch = u_s.shape[1] // CHUNK
    per = LANES // S5_CG
    for v in range(u_s.shape[0]):
        for w in range(CHUNK // per):
            by_group = _block_transpose([u_s[v, pl.ds(per * w + jj, n_ch, stride=CHUNK), :] for jj in range(per)])
            for gg in range(per):
                xs_ref[per * v + gg, :, w * LANES:(w + 1) * LANES] = by_group[gg].astype(xs_ref.dtype)


def _chunks_to_tokens(ys_ref, y_s):
    n_ch = y_s.shape[1] // CHUNK
    per = LANES // S5_CG
    for v in range(y_s.shape[0]):
        for w in range(CHUNK // per):
            by_phase = _block_transpose([ys_ref[per * v + gg, :, w * LANES:(w + 1) * LANES] for gg in range(per)])
            for jj in range(per):
                y_s[v, pl.ds(per * w + jj, n_ch, stride=CHUNK), :] = by_phase[jj]


def _ada_kernel(c_ref, w_ref, b_ref, o_ref):
    cc = c_ref[...]
    s = cc * jax.nn.sigmoid(cc)
    o_ref[...] = jnp.dot(s, w_ref[...], precision=_HI,
                         preferred_element_type=jnp.float32) + b_ref[...]


def _ada(cc, w, b, bn):
    rows, d = cc.shape
    n = w.shape[1]
    return pl.pallas_call(
        _ada_kernel,
        grid=(n // bn,),
        in_specs=[pl.BlockSpec((rows, d), lambda j: (0, 0)),
                  pl.BlockSpec((d, bn), lambda j: (0, j)),
                  pl.BlockSpec((1, bn), lambda j: (0, j))],
        out_specs=pl.BlockSpec((rows, bn), lambda j: (0, j)),
        out_shape=jax.ShapeDtypeStruct((rows, n), jnp.float32),
        compiler_params=pltpu.CompilerParams(dimension_semantics=("arbitrary",),
                                             vmem_limit_bytes=VMEM_LIMIT_BYTES),
        name="ada",
    )(cc, w, b)


def _ffn_in_kernel(alpha, na_w, x_ref, mod_ref, lng_ref, lnb_ref, wup_ref, wdn_ref, win_ref,
                   x1_ref, qk_ref, vt_ref, xs_ref, p_s, u_s):
    x = x_ref[...]
    mod = mod_ref[0]
    h = (x * (1.0 + mod[1:2]) + mod[0:1]).astype(p_s.dtype)
    y = _swiglu(h, wup_ref, wdn_ref, p_s)
    x1 = _layer_norm(alpha * x + (0.5 * mod[2:3]) * y, lng_ref[0:1], lnb_ref[0:1])
    x1_ref[...] = x1
    h2 = (x1 * (1.0 + mod[4:5]) + mod[3:4]).astype(p_s.dtype)
    pu = _mm(h2, win_ref[:, 3 * na_w:])
    for v in range(u_s.shape[0]):
        u_s[v] = pu[:, v * LANES:(v + 1) * LANES]
    _tokens_to_chunks(u_s, xs_ref)
    pr = _mm(h2, win_ref[:, :3 * na_w])
    qk_ref[:, :na_w] = (pr[:, :na_w] * (NA_DH ** -0.5 * LOG2_E)).astype(qk_ref.dtype)
    qk_ref[:, na_w:] = pr[:, na_w:2 * na_w].astype(qk_ref.dtype)
    for t in range(vt_ref.shape[0]):
        vt_ref[t] = pr[t * NA_BLOCK:(t + 1) * NA_BLOCK, 2 * na_w:].T.astype(vt_ref.dtype)


def _ffn_in(x2d, mod, tiles_per_mod, lng, lnb, wup, wdn, win, alpha, na_w, tm):
    t, d = x2d.shape
    ff = wdn.shape[0]
    ncol = win.shape[1]
    s5_w = ncol - 3 * na_w
    groups, cw = s5_w // S5_CG, CHUNK * S5_CG
    if tiles_per_mod is None:
        mod_map = lambda i: (0, 0, 0)
    else:
        mod_map = lambda i: (i // tiles_per_mod, 0, 0)
    return pl.pallas_call(
        functools.partial(_ffn_in_kernel, alpha, na_w),
        grid=(t // tm,),
        in_specs=[pl.BlockSpec((tm, d), lambda i: (i, 0)),
                  pl.BlockSpec((1, N_MOD, d), mod_map),
                  _const_spec(lng.shape), _const_spec(lnb.shape),
                  _const_spec(wup.shape), _const_spec(wdn.shape), _const_spec(win.shape)],
        out_specs=[pl.BlockSpec((tm, d), lambda i: (i, 0)),
                   pl.BlockSpec((tm, 2 * na_w), lambda i: (i, 0)),
                   pl.BlockSpec((tm // NA_BLOCK, na_w, NA_BLOCK), lambda i: (i, 0, 0)),
                   pl.BlockSpec((groups, tm // CHUNK, cw), lambda i: (0, i, 0))],
        out_shape=[jax.ShapeDtypeStruct((t, d), jnp.float32),
                   jax.ShapeDtypeStruct((t, 2 * na_w), _MXU_DTYPE),
                   jax.ShapeDtypeStruct((t // NA_BLOCK, na_w, NA_BLOCK), _MXU_DTYPE),
                   jax.ShapeDtypeStruct((groups, t // CHUNK, cw), jnp.float32)],
        scratch_shapes=[pltpu.VMEM((tm, ff), _MXU_DTYPE), pltpu.VMEM((s5_w // LANES, tm, LANES), jnp.float32)],
        compiler_params=pltpu.CompilerParams(dimension_semantics=("arbitrary",),
                                             vmem_limit_bytes=VMEM_LIMIT_BYTES),
        name="ffn_in",
    )(x2d, mod, lng, lnb, wup, wdn, win)


def _toeplitz_kernel(lhs_ref, rhs_ref, o_ref):
    width = o_ref.shape[2]
    lane = lax.broadcasted_iota(jnp.int32, (1, width), 1)
    for g in range(o_ref.shape[0]):
        k_f = jnp.dot(lhs_ref[0, g], rhs_ref[0, g], precision=_HI, preferred_element_type=jnp.float32)
        k_b = jnp.dot(lhs_ref[1, g], rhs_ref[1, g], precision=_HI, preferred_element_type=jnp.float32)
        for i in range(CHUNK):
            lo, hi = i * S5_CG, (i + 1) * S5_CG
            fwd = k_f if lo == 0 else jnp.where(lane >= lo, pltpu.roll(k_f, lo, axis=1), 0.0)
            bwd = k_b if hi == width else jnp.where(lane < hi, pltpu.roll(k_b, hi, axis=1), 0.0)
            o_ref[g, lo:hi, :] = (fwd + bwd).astype(o_ref.dtype)


def _toeplitz(lhs, rhs):
    _, g, cg, kdim = lhs.shape
    width = rhs.shape[3]
    gb = _pick_tile(g, 8)
    return pl.pallas_call(
        _toeplitz_kernel,
        grid=(g // gb,),
        in_specs=[pl.BlockSpec((2, gb, cg, kdim), lambda k: (0, k, 0, 0)),
                  pl.BlockSpec((2, gb, kdim, width), lambda k: (0, k, 0, 0))],
        out_specs=pl.BlockSpec((gb, width, width), lambda k: (k, 0, 0)),
        out_shape=jax.ShapeDtypeStruct((g, width, width), _MXU_DTYPE),
        compiler_params=pltpu.CompilerParams(dimension_semantics=("arbitrary",)),
        name="s5_toeplitz",
    )(lhs, rhs)


def _s5_tables(a_re, a_im, log_dt, b_re, b_im, c_re, c_im, d_skip):
    g, p = a_re.shape[1], a_re.shape[2]
    f32 = jnp.float32
    a_re, a_im = a_re.astype(f32), a_im.astype(f32)
    dt = jnp.exp(log_dt.astype(f32))[..., None]
    lr, li = a_re * dt, a_im * dt

    width = CHUNK * S5_CG
    asc = np.arange(CHUNK)
    desc = asc[::-1].copy()
    lane_m, lane_c = np.arange(width) // S5_CG, np.arange(width) % S5_CG
    row_i = np.arange(width) // S5_CG

    def powers(expo, lr_b, li_b):
        e = jnp.asarray(expo, f32)
        mag = jnp.exp(e * lr_b)
        return mag * jnp.cos(e * li_b), mag * jnp.sin(e * li_b)


    ab_r, ab_i = jnp.exp(lr) * jnp.cos(li), jnp.exp(lr) * jnp.sin(li)
    den = a_re ** 2 + a_im ** 2
    nr, ni = ab_r - 1.0, ab_i
    sr = ((nr * a_re + ni * a_im) / den)[:, :, None, :]
    si = ((ni * a_re - nr * a_im) / den)[:, :, None, :]
    bt_re, bt_im = jnp.swapaxes(b_re, 2, 3).astype(f32), jnp.swapaxes(b_im, 2, 3).astype(f32)
    bt_r = sr * bt_re - si * bt_im
    bt_i = sr * bt_im + si * bt_re

    tile_c = jnp.asarray(lane_c[None, :] == np.arange(S5_CG)[:, None], f32)
    ct_r = jnp.einsum('dgcp,cl->dgpl', c_re.astype(f32), tile_c, precision=_HI)
    ct_i = jnp.einsum('dgcp,cl->dgpl', c_im.astype(f32), tile_c, precision=_HI)

    def by_lane(expo):
        return powers(np.stack(expo)[:, lane_m][:, None, None, :], lr[..., None], li[..., None])

    q_r, q_i = by_lane([asc, desc])
    rhs = jnp.concatenate([ct_r * q_r - ct_i * q_i, ct_r * q_i + ct_i * q_r], axis=2)
    toep = _toeplitz(jnp.concatenate([bt_r, -bt_i], axis=3), rhs)

    v_r, v_i = by_lane([asc + 1, desc + 1])
    cs_r = ct_r * v_r - ct_i * v_i
    cs_i = -(ct_r * v_i + ct_i * v_r)

    w_r, w_i = powers(np.stack([desc, asc])[:, row_i][:, None, :, None], lr[:, :, None, :], li[:, :, None, :])
    rows_r, rows_i = jnp.tile(bt_r, (1, 1, CHUNK, 1)), jnp.tile(bt_i, (1, 1, CHUNK, 1))
    bs_r = w_r * rows_r - w_i * rows_i
    bs_i = w_r * rows_i + w_i * rows_r
    even = jnp.asarray((np.arange(g) % 2) == 0)[None, :, None, None]
    bs = jnp.concatenate([jnp.where(even, bs_r, 0.0), jnp.where(even, 0.0, bs_r),
                          jnp.where(even, bs_i, 0.0), jnp.where(even, 0.0, bs_i)], axis=3)
    cs = jnp.concatenate([jnp.where(even, cs_r, 0.0), jnp.where(even, 0.0, cs_r),
                          jnp.where(even, cs_i, 0.0), jnp.where(even, 0.0, cs_i)], axis=2)

    e_r, e_i = jnp.exp(CHUNK * lr), CHUNK * li
    half = 2 * p
    a_pow = jnp.stack([(e_r * jnp.cos(e_i)).reshape(2, g // 2, half),
                       (e_r * jnp.sin(e_i)).reshape(2, g // 2, half)], axis=2)
    dvec = jnp.dot(d_skip.astype(f32), tile_c, precision=_HI)[:, None, :]
    return toep, bs.astype(_MXU_DTYPE), cs.astype(_MXU_DTYPE), a_pow, dvec


def _row_pitch(n):
    return n + (4 - n) % 8


def _s5_kernel(nb, xl_ref, xc_ref, toep_ref, bs_ref, cs_ref, apow_ref, dvec_ref, y_ref, zc_s, zl_s, h_s):
    rows_l = xl_ref.shape[1]
    rows_c = xc_ref.shape[1]
    n_l, n_c = rows_l // nb, rows_c // nb
    pitch_l, pitch_c = _row_pitch(n_l), _row_pitch(n_c)
    half = zl_s.shape[3]
    xl = [xl_ref[g] for g in range(2)]
    xlb = [v.astype(_MXU_DTYPE) for v in xl]
    xcb = [xc_ref[g].astype(_MXU_DTYPE) for g in range(2)]

    for d in range(2):
        zc = _mm(xcb[0], bs_ref[d, 0]) + _mm(xcb[1], bs_ref[d, 1])
        zl = _mm(xlb[0], bs_ref[d, 0]) + _mm(xlb[1], bs_ref[d, 1])
        for ri in range(2):
            for b in range(nb):
                zc_s[d, ri, b * pitch_c:b * pitch_c + n_c, :] = zc[b * n_c:(b + 1) * n_c, ri * half:(ri + 1) * half]
                zl_s[d, ri, b * pitch_l:b * pitch_l + n_l, :] = zl[b * n_l:(b + 1) * n_l, ri * half:(ri + 1) * half]

    a_r = [jnp.broadcast_to(apow_ref[d, 0, 0:1, :], (nb, half)) for d in range(2)]
    a_i = [jnp.broadcast_to(apow_ref[d, 0, 1:2, :], (nb, half)) for d in range(2)]

    def step(d, n, latent, h_r, h_i):
        if latent:
            h_s[d, 0, pl.ds(n, nb, stride=pitch_l), :] = h_r
            h_s[d, 1, pl.ds(n, nb, stride=pitch_l), :] = h_i
            z_r = zl_s[d, 0, pl.ds(n, nb, stride=pitch_l), :]
            z_i = zl_s[d, 1, pl.ds(n, nb, stride=pitch_l), :]
        else:
            z_r = zc_s[d, 0, pl.ds(n, nb, stride=pitch_c), :]
            z_i = zc_s[d, 1, pl.ds(n, nb, stride=pitch_c), :]
        return (a_r[d] * h_r - a_i[d] * h_i + z_r,
                a_r[d] * h_i + a_i[d] * h_r + z_i)

    def ctx_body(n, carry):
        f_r, f_i, b_r, b_i = carry
        f_r, f_i = step(0, n, False, f_r, f_i)
        b_r, b_i = step(1, n_c - 1 - n, False, b_r, b_i)
        return f_r, f_i, b_r, b_i

    def lat_body(n, carry):
        f_r, f_i, b_r, b_i = carry
        f_r, f_i = step(0, n, True, f_r, f_i)
        b_r, b_i = step(1, n_l - 1 - n, True, b_r, b_i)
        return f_r, f_i, b_r, b_i

    zero = jnp.zeros((nb, half), jnp.float32)
    carry = lax.fori_loop(0, n_c, ctx_body, (zero, zero, zero, zero), unroll=SCAN_UNROLL)
    lax.fori_loop(0, n_l, lat_body, carry, unroll=SCAN_UNROLL)

    def plane(d, ri):
        return jnp.concatenate([h_s[d, ri, b * pitch_l:b * pitch_l + n_l, :] for b in range(nb)], axis=0)

    hb = [jnp.concatenate([plane(d, 0), plane(d, 1)], axis=1).astype(_MXU_DTYPE) for d in range(2)]
    for g in range(2):
        y = _mm(xlb[g], toep_ref[g]) + _mm(hb[0], cs_ref[0, g]) + _mm(hb[1], cs_ref[1, g])
        y_ref[g] = y + xl[g] * dvec_ref[g]


def _s5(xl, xc, toep, bs, cs, a_pow, dvec, nb):
    g, rows_l, w = xl.shape
    rows_c = xc.shape[1]
    sw = bs.shape[3]
    return pl.pallas_call(
        functools.partial(_s5_kernel, nb),
        grid=(g // 2,),
        in_specs=[pl.BlockSpec((2, rows_l, w), lambda k: (k, 0, 0)),
                  pl.BlockSpec((2, rows_c, w), lambda k: (k, 0, 0)),
                  pl.BlockSpec((2, w, w), lambda k: (k, 0, 0)),
                  pl.BlockSpec((2, 2, w, sw), lambda k: (0, k, 0, 0)),
                  pl.BlockSpec((2, 2, sw, w), lambda k: (0, k, 0, 0)),
                  pl.BlockSpec((2, 1, 2, sw // 2), lambda k: (0, k, 0, 0)),
                  pl.BlockSpec((2, 1, w), lambda k: (k, 0, 0))],
        out_specs=pl.BlockSpec((2, rows_l, w), lambda k: (k, 0, 0)),
        out_shape=jax.ShapeDtypeStruct((g, rows_l, w), jnp.float32),
        scratch_shapes=[pltpu.VMEM((2, 2, nb * _row_pitch(rows_c // nb), sw // 2), jnp.float32),
                        pltpu.VMEM((2, 2, nb * _row_pitch(rows_l // nb), sw // 2), jnp.float32),
                        pltpu.VMEM((2, 2, nb * _row_pitch(rows_l // nb), sw // 2), jnp.float32)],
        compiler_params=pltpu.CompilerParams(dimension_semantics=("arbitrary",),
                                             vmem_limit_bytes=VMEM_LIMIT_BYTES),
        name="s5",
    )(xl, xc, toep, bs, cs, a_pow, dvec)


def _na_row_plan():
    last_q = NA_KROWS - NA_QROWS
    plan = []
    for r0s, qrs in ((lambda lr: 0, lambda lr: lr),
                     (lambda lr: lr, lambda lr: NA_KR // 2 + lr),
                     (lambda lr: last_q - NA_KR // 2, lambda lr: last_q + lr)):
        plan.append([[kl - qrs(lr) + (NA_KR - 1) if r0s(lr) <= kl < r0s(lr) + NA_KR else None
                      for kl in range(NA_KROWS)] for lr in range(NA_QROWS)])
    return plan


def _na_bias_kernel(plan, col_ref, o_ref):
    lane = lax.broadcasted_iota(jnp.int32, (1, 2 * GRID_W), 1)
    masked = jnp.full((GRID_W, 2 * GRID_W), MASK_VALUE, jnp.float32)
    for kind in range(len(plan)):
        for kl in range(NA_KROWS):
            for lp in range(NA_QROWS // 2):
                a0, a1 = plan[kind][2 * lp][kl], plan[kind][2 * lp + 1][kl]
                t0 = masked if a0 is None else col_ref[0, a0]
                t1 = masked if a1 is None else col_ref[0, a1]
                tile = t0 if a0 == a1 else jnp.where(lane < GRID_W, t0, t1)
                o_ref[kind, 0, kl * GRID_W:(kl + 1) * GRID_W, lp * 2 * GRID_W:(lp + 1) * 2 * GRID_W] = tile


def _na_bias(rpb):
    heads = rpb.shape[0]
    kc = np.arange(GRID_W)[:, None]
    qc = np.arange(GRID_W)[None, :]
    c0 = np.clip(qc - NA_KC // 2, 0, GRID_W - NA_KC)
    col_ok = (kc >= c0) & (kc < c0 + NA_KC)
    sel_c = ((kc - qc + (NA_KC - 1))[:, :, None] == np.arange(2 * NA_KC - 1)) & col_ok[:, :, None]
    by_col = jnp.einsum('hab,kqb->hakq', rpb.astype(jnp.float32), jnp.asarray(sel_c, jnp.float32), precision=_HI)
    by_col = jnp.where(jnp.asarray(col_ok), by_col * LOG2_E, MASK_VALUE)
    by_col = jnp.concatenate([by_col, by_col], axis=3)
    plan = _na_row_plan()
    nk = NA_KROWS * GRID_W
    return pl.pallas_call(
        functools.partial(_na_bias_kernel, plan),
        grid=(heads,),
        in_specs=[pl.BlockSpec((1,) + by_col.shape[1:], lambda h: (h, 0, 0, 0))],
        out_specs=pl.BlockSpec((len(plan), 1, nk, NA_BLOCK), lambda h: (0, h, 0, 0)),
        out_shape=jax.ShapeDtypeStruct((len(plan), heads, nk, NA_BLOCK), jnp.float32),
        compiler_params=pltpu.CompilerParams(dimension_semantics=("arbitrary",)),
        name="na_bias",
    )(by_col)


def _row_max(x):
    while x.shape[0] % (2 * SUBLANES) == 0:
        half = x.shape[0] // 2
        x = jnp.maximum(x[:half], x[half:])
    return jnp.max(x, axis=0, keepdims=True)


def _na_kernel(rows, q_ref, k_ref, vt_ref, kc_ref, vct_ref, bias_ref, o_ref):
    n_rb = rows // NA_QROWS
    nk = NA_KROWS * GRID_W
    pair = 2 * NA_DH
    lane_first = lax.broadcasted_iota(jnp.int32, (1, pair), 1) < NA_DH
    row_first = lax.broadcasted_iota(jnp.int32, (pair, 1), 0) < NA_DH
    n_heads = q_ref.shape[2] // NA_DH
    n_blocks = q_ref.shape[1] // NA_BLOCK

    def block(j):
        rb = pl.program_id(1) * n_blocks + j
        kind = jnp.where(rb == 0, 0, jnp.where(rb == n_rb - 1, 2, 1))
        first_row = jnp.clip(rb * NA_QROWS - NA_KR // 2, 0, rows - NA_KROWS)
        return kind, pl.multiple_of(first_row * GRID_W, GRID_W), first_row // NA_QROWS

    geometry = [block(j) for j in range(n_blocks)]

    def scores(u):
        j, h = divmod(u, n_heads)
        kind, start, _ = geometry[j]
        cols = slice(h // 2 * pair, (h // 2 + 1) * pair)
        lane_mine = lane_first if h % 2 == 0 else jnp.logical_not(lane_first)
        qm = jnp.where(lane_mine, q_ref[0, j * NA_BLOCK:(j + 1) * NA_BLOCK, cols],
                       jnp.zeros((NA_BLOCK, pair), q_ref.dtype))
        s_w = _mm_nt(k_ref[0, pl.ds(start, nk), cols], qm) + bias_ref[kind, h]
        s_c = _mm_nt(kc_ref[0, :, cols], qm)
        return s_w, s_c

    def weights(s_w, s_c):
        mx = jnp.maximum(_row_max(s_w), _row_max(s_c))
        return jnp.exp2(s_w - mx).astype(vt_ref.dtype), jnp.exp2(s_c - mx).astype(vt_ref.dtype)

    def values(u, p_w, p_c):
        j, h = divmod(u, n_heads)
        blk0 = geometry[j][2]
        cols = slice(h // 2 * pair, (h // 2 + 1) * pair)
        row_mine = row_first if h % 2 == 0 else jnp.logical_not(row_first)
        vtw = jnp.concatenate([vt_ref[blk0 + t, cols, :] for t in range(nk // NA_BLOCK)], axis=1)
        vtc = jnp.concatenate([vct_ref[t, cols, :] for t in range(vct_ref.shape[0])], axis=1)
        o = (_mm(jnp.where(row_mine, vtw, jnp.ones_like(vtw)), p_w)
             + _mm(jnp.where(row_mine, vtc, jnp.ones_like(vtc)), p_c))
        num, den = (o[:NA_DH], o[NA_DH:]) if h % 2 == 0 else (o[NA_DH:], o[:NA_DH])
        return num / den

    n_units = n_blocks * n_heads
    s = {0: scores(0), 1: scores(1)}
    p = {0: weights(*s.pop(0))}
    outs = []
    for u in range(n_units):
        if u + 2 < n_units:
            s[u + 2] = scores(u + 2)
        if u + 1 < n_units:
            p[u + 1] = weights(*s.pop(u + 1))
        outs.append(values(u, *p.pop(u)))
        if len(outs) == n_heads:
            j = u // n_heads
            o_ref[0, j * NA_BLOCK:(j + 1) * NA_BLOCK, :] = jnp.concatenate(outs, axis=0).T.astype(o_ref.dtype)
            outs = []


def _na(qk, vt, qk_c, vt_c, bias, na_w):
    b, s, _ = qk.shape
    l = qk_c.shape[1]
    rows = s // GRID_W
    tq = NA_STEP_BLOCKS * NA_BLOCK
    return pl.pallas_call(
        functools.partial(_na_kernel, rows),
        grid=(b, s // tq),
        in_specs=[pl.BlockSpec((1, tq, na_w), lambda bi, rb: (bi, rb, 0)),
                  pl.BlockSpec((1, s, na_w), lambda bi, rb: (bi, 0, 1)),
                  pl.BlockSpec((s // NA_BLOCK, na_w, NA_BLOCK), lambda bi, rb: (bi, 0, 0)),
                  pl.BlockSpec((1, l, na_w), lambda bi, rb: (bi, 0, 1)),
                  pl.BlockSpec((l // NA_BLOCK, na_w, NA_BLOCK), lambda bi, rb: (bi, 0, 0)),
                  _const_spec(bias.shape)],
        out_specs=pl.BlockSpec((1, tq, na_w), lambda bi, rb: (bi, rb, 0)),
        out_shape=jax.ShapeDtypeStruct((b, s, na_w), _MXU_DTYPE),
        compiler_params=pltpu.CompilerParams(
            dimension_semantics=("arbitrary", "arbitrary"),
            vmem_limit_bytes=VMEM_LIMIT_BYTES),
        name="na",
    )(qk, qk, vt, qk_c, vt_c, bias)


def _out_ffn_kernel(alpha, x1_ref, yna_ref, ys5_ref, mod_a_ref, mod_b_ref, lng_ref, lnb_ref, wglu_ref, bglu_ref,
                    wout_ref, wup_ref, wdn_ref, o_ref, p_s, y_s, h_s, x2_s):
    @pl.when(pl.program_id(0) == 0)
    def _():
        h_s[...] = jnp.zeros_like(h_s)
        x2_s[...] = jnp.zeros_like(x2_s)

    mod_a = mod_a_ref[0]
    na_w = yna_ref.shape[1]
    stage_a = {}

    def relayout():
        _chunks_to_tokens(ys5_ref, y_s)

    def gated():
        gl = jax.nn.gelu(jnp.concatenate([y_s[v] for v in range(y_s.shape[0])], axis=1))
        gate = jax.nn.sigmoid(_mm(gl.astype(p_s.dtype), wglu_ref[...]) + bglu_ref[...])
        stage_a["glu"] = (gl * gate).astype(p_s.dtype)

    def mixed():
        y = _mm(yna_ref[...], wout_ref[:na_w, :]) + _mm(stage_a["glu"], wout_ref[na_w:, :])
        x2 = _layer_norm(alpha * x1_ref[...] + mod_a[5:6] * y, lng_ref[1:2], lnb_ref[1:2])
        stage_a["x2"] = x2
        stage_a["h"] = (x2 * (1.0 + mod_a[7:8]) + mod_a[6:7]).astype(h_s.dtype)

    mod_b = mod_b_ref[0]
    f = _swiglu(h_s[...], wup_ref, wdn_ref, p_s, between=(relayout, gated, mixed))
    o_ref[...] = _layer_norm(alpha * x2_s[...] + (0.5 * mod_b[8:9]) * f, lng_ref[2:3], lnb_ref[2:3])

    x2_s[...] = stage_a["x2"]
    h_s[...] = stage_a["h"]


def _out_ffn(x1, yna, ys5, mod, tiles_per_mod, lng, lnb, wglu, bglu, wout, wup, wdn, alpha, tm):
    t, d = x1.shape
    ff = wdn.shape[0]
    na_w = yna.shape[1]
    groups, _, cw = ys5.shape
    s5_w = groups * S5_CG
    n = t // tm

    def cur(i):
        return jnp.minimum(i, n - 1)

    def prev(i):
        return jnp.maximum(i - 1, 0)

    return pl.pallas_call(
        functools.partial(_out_ffn_kernel, alpha),
        grid=(n + 1,),
        in_specs=[pl.BlockSpec((tm, d), lambda i: (cur(i), 0)),
                  pl.BlockSpec((tm, na_w), lambda i: (cur(i), 0)),
                  pl.BlockSpec((groups, tm // CHUNK, cw), lambda i: (0, cur(i), 0)),
                  pl.BlockSpec((1, N_MOD, d), lambda i: (cur(i) // tiles_per_mod, 0, 0)),
                  pl.BlockSpec((1, N_MOD, d), lambda i: (prev(i) // tiles_per_mod, 0, 0)),
                  _const_spec(lng.shape), _const_spec(lnb.shape),
                  _const_spec(wglu.shape), _const_spec(bglu.shape), _const_spec(wout.shape),
                  _const_spec(wup.shape), _const_spec(wdn.shape)],
        out_specs=pl.BlockSpec((tm, d), lambda i: (prev(i), 0)),
        out_shape=jax.ShapeDtypeStruct((t, d), jnp.float32),
        scratch_shapes=[pltpu.VMEM((tm, ff), _MXU_DTYPE), pltpu.VMEM((s5_w // LANES, tm, LANES), jnp.float32),
                        pltpu.VMEM((tm, d), _MXU_DTYPE), pltpu.VMEM((tm, d), jnp.float32)],
        compiler_params=pltpu.CompilerParams(dimension_semantics=("arbitrary",),
                                             vmem_limit_bytes=VMEM_LIMIT_BYTES),
        name="out_ffn",
    )(x1, yna, ys5, mod, mod, lng, lnb, wglu, bglu, wout, wup, wdn)


def _pick_tile(n, target):
    t = min(n, target)
    while n % t:
        t //= 2
    return t


def kernel(x, c, ctx, c_ctx, w_ada, b_ada, ln_g, ln_b, ffn1_w_up, ffn1_w_down, w_in, na_rpb, s5_a_re, s5_a_im, s5_log_dt, s5_b_re, s5_b_im, s5_c_re, s5_c_im, s5_d, s5_w_glu, s5_b_glu, w_out, ffn2_w_up, ffn2_w_down):
    depth = w_ada.shape[0]
    assert depth == 1, "single-layer stack only"
    bsz, seq, d = x.shape
    l_ctx = ctx.shape[1]
    s5_w = s5_w_glu.shape[1]
    na_w = w_out.shape[1] - s5_w
    groups = s5_w // S5_CG
    rows = seq // GRID_W
    assert bsz % 8 == 0 and seq % (NA_STEP_BLOCKS * NA_BLOCK) == 0 and rows >= NA_KROWS
    assert l_ctx % NA_BLOCK == 0 and groups % 2 == 0 and 2 * s5_a_re.shape[3] == LANES
    assert na_w % (2 * NA_DH) == 0 and ffn1_w_down.shape[1] % MXU_TILE == 0
    alpha = (2.0 * depth) ** 0.25
    cd = _MXU_DTYPE

    n_rows = -(-(bsz + 1) // 8) * 8
    cc = jnp.zeros((n_rows, d), jnp.float32).at[:bsz].set(c).at[bsz].set(c_ctx)
    n_ada = w_ada.shape[2]
    mod = _ada(cc, w_ada[0], b_ada[0][None, :], _pick_tile(n_ada, 1152)).reshape(n_rows, N_MOD, d)

    lng, lnb = ln_g[0], ln_b[0]
    wup1, wdn1 = ffn1_w_up[0].astype(cd), ffn1_w_down[0].astype(cd)
    wup2, wdn2 = ffn2_w_up[0].astype(cd), ffn2_w_down[0].astype(cd)
    win = w_in[0].astype(cd)

    tm = _pick_tile(seq, 512)
    x1, qk, vt, xs = _ffn_in(x.reshape(bsz * seq, d), mod, seq // tm, lng, lnb, wup1, wdn1, win, alpha, na_w, tm)
    tmc = _pick_tile(bsz * l_ctx, 512)
    _, qk_c, vt_c, xs_c = _ffn_in(ctx.reshape(bsz * l_ctx, d), mod[bsz:bsz + 1], None, lng, lnb, wup1, wdn1, win,
                                  alpha, na_w, tmc)

    tables = _s5_tables(s5_a_re[0], s5_a_im[0], s5_log_dt[0], s5_b_re[0], s5_b_im[0], s5_c_re[0], s5_c_im[0],
                        s5_d[0])
    y_s5 = _s5(xs, xs_c, *tables, bsz)

    y_na = _na(qk.reshape(bsz, seq, 2 * na_w), vt, qk_c.reshape(bsz, l_ctx, 2 * na_w), vt_c, _na_bias(na_rpb[0]),
               na_w).reshape(bsz * seq, na_w)

    out = _out_ffn(x1, y_na, y_s5, mod, seq // tm, lng, lnb, s5_w_glu[0].astype(cd), s5_b_glu[0][None, :],
                   w_out[0].astype(cd), wup2, wdn2, alpha, tm)
    return out.reshape(bsz, seq, d)
```

```python
import functools

import jax
import jax.numpy as jnp
import numpy as np
from jax import lax
from jax.experimental import pallas as pl
from jax.experimental.pallas import tpu as pltpu

GRID_W = 64
NA_DH = 64
NA_KR = 8
NA_KC = 16
S5_CG = 16
S5_P = 64
LN_EPS = 1e-6
N_MOD = 9

CHUNK = 16
NA_QROWS = 4
NA_KROWS = 12
NA_BLOCK = NA_QROWS * GRID_W
NA_STEP_BLOCKS = 4
MASK_VALUE = -1e30
LOG2_E = 1.4426950408889634
SCAN_UNROLL = 4

MXU_TILE = 256
LANES = 128
SUBLANES = 8
VMEM_LIMIT_BYTES = 56 * 1024 * 1024

_MXU_DTYPE = jnp.bfloat16
_HI = lax.Precision.HIGHEST


def _mm(a, b):
    return jnp.dot(a, b, preferred_element_type=jnp.float32)


def _mm_nt(a, b):
    return lax.dot_general(a, b, (((1,), (1,)), ((), ())), preferred_element_type=jnp.float32)


def _const_spec(shape):
    nd = len(shape)
    return pl.BlockSpec(shape, lambda *_: (0,) * nd, pipeline_mode=pl.Buffered(1))


def _layer_norm(r, g, b):
    mu = jnp.mean(r, axis=-1, keepdims=True)
    d = r - mu
    var = jnp.mean(d * d, axis=-1, keepdims=True)
    return d * lax.rsqrt(var + LN_EPS) * g + b


def _swiglu_hidden(h, wup_ref, p_s, between=()):
    ff = p_s.shape[1]
    n_chunks = ff // MXU_TILE
    slots = [(k + 1) * n_chunks // (len(between) + 1) for k in range(len(between))]
    for j in range(n_chunks):
        for slot, fn in zip(slots, between):
            if slot == j:
                fn()
        lo = j * MXU_TILE
        a = _mm(h, wup_ref[:, lo:lo + MXU_TILE])
        g = _mm(h, wup_ref[:, ff + lo:ff + lo + MXU_TILE])
        p_s[:, lo:lo + MXU_TILE] = (g * jax.nn.sigmoid(g) * a).astype(p_s.dtype)


def _swiglu(h, wup_ref, wdn_ref, p_s, between=()):
    _swiglu_hidden(h, wup_ref, p_s, between)
    return _mm(p_s[...], wdn_ref[...])


def _block_transpose(tiles):
    per = len(tiles)
    blk = lax.broadcasted_iota(jnp.int32, (1, LANES), 1) // S5_CG
    tiles = list(tiles)
    d = per // 2
    while d:
        upper = (blk & d) != 0
        for a in range(per):
            if a & d:
                continue
            lo, hi = tiles[a], tiles[a + d]
            tiles[a] = jnp.where(upper, pltpu.roll(hi, S5_CG * d, axis=1), lo)
            tiles[a + d] = jnp.where(upper, hi, pltpu.roll(lo, LANES - S5_CG * d, axis=1))
        d //= 2
    return tiles


def _tokens_to_chunks(u_s, xs_ref, row0, n_rows):
    n_ch = n_rows // CHUNK
    ch0 = row0 // CHUNK
    per = LANES // S5_CG
    for v in range(u_s.shape[0]):
        for w in range(CHUNK // per):
            by_group = _block_transpose(
                [u_s[v, pl.ds(row0 + per * w + jj, n_ch, stride=CHUNK), :] for jj in range(per)])
            for gg in range(per):
                xs_ref[per * v + gg, ch0:ch0 + n_ch, w * LANES:(w + 1) * LANES] = by_group[gg].astype(xs_ref.dtype)


def _chunks_to_tokens(ys_ref, y_s):
    n_ch = y_s.shape[1] // CHUNK
    per = LANES // S5_CG
    for v in range(y_s.shape[0]):
        for w in range(CHUNK // per):
            by_phase = _block_transpose([ys_ref[per * v + gg, :, w * LANES:(w + 1) * LANES] for gg in range(per)])
            for jj in range(per):
                y_s[v, pl.ds(per * w + jj, n_ch, stride=CHUNK), :] = by_phase[jj]


def _ada_kernel(c_ref, w_ref, b_ref, o_ref):
    cc = c_ref[...]
    s = cc * jax.nn.sigmoid(cc)
    o_ref[...] = jnp.dot(s, w_ref[...], precision=_HI,
                         preferred_element_type=jnp.float32) + b_ref[...]


def _ada(cc, w, b, bn):
    rows, d = cc.shape
    n = w.shape[1]
    return pl.pallas_call(
        _ada_kernel,
        grid=(n // bn,),
        in_specs=[pl.BlockSpec((rows, d), lambda j: (0, 0)),
                  pl.BlockSpec((d, bn), lambda j: (0, j)),
                  pl.BlockSpec((1, bn), lambda j: (0, j))],
        out_specs=pl.BlockSpec((rows, bn), lambda j: (0, j)),
        out_shape=jax.ShapeDtypeStruct((rows, n), jnp.float32),
        compiler_params=pltpu.CompilerParams(dimension_semantics=("arbitrary",),
                                             vmem_limit_bytes=VMEM_LIMIT_BYTES),
        name="ada",
    )(cc, w, b)


def _ffn_in_kernel(alpha, na_w, x_ref, mod_ref, lng_ref, lnb_ref, wup_ref, wdn_ref, win_ref,
                   x1_ref, qk_ref, vt_ref, xs_ref, p_s, u_s):
    mod = mod_ref[0]
    h = (x_ref[...] * (1.0 + mod[1:2]) + mod[0:1]).astype(p_s.dtype)
    _swiglu_hidden(h, wup_ref, p_s)
    n_blk = vt_ref.shape[0]
    ys = [_mm(p_s[t * NA_BLOCK:(t + 1) * NA_BLOCK, :], wdn_ref[...]) for t in range(n_blk)]
    for t in range(n_blk):
        rows = slice(t * NA_BLOCK, (t + 1) * NA_BLOCK)
        x1 = _layer_norm(alpha * x_ref[rows, :] + (0.5 * mod[2:3]) * ys[t], lng_ref[0:1], lnb_ref[0:1])
        x1_ref[rows, :] = x1
        h2 = (x1 * (1.0 + mod[4:5]) + mod[3:4]).astype(p_s.dtype)
        pu = _mm(h2, win_ref[:, 3 * na_w:])
        for v in range(u_s.shape[0]):
            u_s[v, rows, :] = pu[:, v * LANES:(v + 1) * LANES]
        _tokens_to_chunks(u_s, xs_ref, t * NA_BLOCK, NA_BLOCK)
        pr = _mm(h2, win_ref[:, :3 * na_w])
        qk_ref[rows, :na_w] = (pr[:, :na_w] * (NA_DH ** -0.5 * LOG2_E)).astype(qk_ref.dtype)
        qk_ref[rows, na_w:] = pr[:, na_w:2 * na_w].astype(qk_ref.dtype)
        vt_ref[t] = pr[:, 2 * na_w:].T.astype(vt_ref.dtype)


def _ffn_in(x2d, mod, tiles_per_mod, lng, lnb, wup, wdn, win, alpha, na_w, tm):
    t, d = x2d.shape
    ff = wdn.shape[0]
    ncol = win.shape[1]
    s5_w = ncol - 3 * na_w
    groups, cw = s5_w // S5_CG, CHUNK * S5_CG
    if tiles_per_mod is None:
        mod_map = lambda i: (0, 0, 0)
    else:
        mod_map = lambda i: (i // tiles_per_mod, 0, 0)
    return pl.pallas_call(
        functools.partial(_ffn_in_kernel, alpha, na_w),
        grid=(t // tm,),
        in_specs=[pl.BlockSpec((tm, d), lambda i: (i, 0)),
                  pl.BlockSpec((1, N_MOD, d), mod_map),
                  _const_spec(lng.shape), _const_spec(lnb.shape),
                  _const_spec(wup.shape), _const_spec(wdn.shape), _const_spec(win.shape)],
        out_specs=[pl.BlockSpec((tm, d), lambda i: (i, 0)),
                   pl.BlockSpec((tm, 2 * na_w), lambda i: (i, 0)),
                   pl.BlockSpec((tm // NA_BLOCK, na_w, NA_BLOCK), lambda i: (i, 0, 0)),
                   pl.BlockSpec((groups, tm // CHUNK, cw), lambda i: (0, i, 0))],
        out_shape=[jax.ShapeDtypeStruct((t, d), jnp.float32),
                   jax.ShapeDtypeStruct((t, 2 * na_w), _MXU_DTYPE),
                   jax.ShapeDtypeStruct((t // NA_BLOCK, na_w, NA_BLOCK), _MXU_DTYPE),
                   jax.ShapeDtypeStruct((groups, t // CHUNK, cw), jnp.float32)],
        scratch_shapes=[pltpu.VMEM((tm, ff), _MXU_DTYPE), pltpu.VMEM((s5_w // LANES, tm, LANES), jnp.float32)],
        compiler_params=pltpu.CompilerParams(dimension_semantics=("arbitrary",),
                                             vmem_limit_bytes=VMEM_LIMIT_BYTES),
        name="ffn_in",
    )(x2d, mod, lng, lnb, wup, wdn, win)


def _toeplitz_kernel(lhs_ref, rhs_ref, o_ref):
    width = o_ref.shape[2]
    lane = lax.broadcasted_iota(jnp.int32, (1, width), 1)
    for g in range(o_ref.shape[0]):
        k_f = jnp.dot(lhs_ref[0, g], rhs_ref[0, g], precision=_HI, preferred_element_type=jnp.float32)
        k_b = jnp.dot(lhs_ref[1, g], rhs_ref[1, g], precision=_HI, preferred_element_type=jnp.float32)
        for i in range(CHUNK):
            lo, hi = i * S5_CG, (i + 1) * S5_CG
            fwd = k_f if lo == 0 else jnp.where(lane >= lo, pltpu.roll(k_f, lo, axis=1), 0.0)
            bwd = k_b if hi == width else jnp.where(lane < hi, pltpu.roll(k_b, hi, axis=1), 0.0)
            o_ref[g, lo:hi, :] = (fwd + bwd).astype(o_ref.dtype)


def _toeplitz(lhs, rhs):
    _, g, cg, kdim = lhs.shape
    width = rhs.shape[3]
    gb = _pick_tile(g, 8)
    return pl.pallas_call(
        _toeplitz_kernel,
        grid=(g // gb,),
        in_specs=[pl.BlockSpec((2, gb, cg, kdim), lambda k: (0, k, 0, 0)),
                  pl.BlockSpec((2, gb, kdim, width), lambda k: (0, k, 0, 0))],
        out_specs=pl.BlockSpec((gb, width, width), lambda k: (k, 0, 0)),
        out_shape=jax.ShapeDtypeStruct((g, width, width), _MXU_DTYPE),
        compiler_params=pltpu.CompilerParams(dimension_semantics=("arbitrary",)),
        name="s5_toeplitz",
    )(lhs, rhs)


def _s5_tables(a_re, a_im, log_dt, b_re, b_im, c_re, c_im, d_skip):
    g, p = a_re.shape[1], a_re.shape[2]
    f32 = jnp.float32
    a_re, a_im = a_re.astype(f32), a_im.astype(f32)
    dt = jnp.exp(log_dt.astype(f32))[..., None]
    lr, li = a_re * dt, a_im * dt

    width = CHUNK * S5_CG
    asc = np.arange(CHUNK)
    desc = asc[::-1].copy()
    lane_m, lane_c = np.arange(width) // S5_CG, np.arange(width) % S5_CG

    def powers(expo, lr_b, li_b):
        e = jnp.asarray(expo, f32)
        mag = jnp.exp(e * lr_b)
        return mag * jnp.cos(e * li_b), mag * jnp.sin(e * li_b)


    ab_r, ab_i = jnp.exp(lr) * jnp.cos(li), jnp.exp(lr) * jnp.sin(li)
    den = a_re ** 2 + a_im ** 2
    nr, ni = ab_r - 1.0, ab_i
    sr = ((nr * a_re + ni * a_im) / den)[:, :, None, :]
    si = ((ni * a_re - nr * a_im) / den)[:, :, None, :]
    bt_re, bt_im = jnp.swapaxes(b_re, 2, 3).astype(f32), jnp.swapaxes(b_im, 2, 3).astype(f32)
    bt_r = sr * bt_re - si * bt_im
    bt_i = sr * bt_im + si * bt_re

    tile_c = jnp.asarray(lane_c[None, :] == np.arange(S5_CG)[:, None], f32)
    ct_r = jnp.einsum('dgcp,cl->dgpl', c_re.astype(f32), tile_c, precision=_HI)
    ct_i = jnp.einsum('dgcp,cl->dgpl', c_im.astype(f32), tile_c, precision=_HI)

    rep_m = jnp.asarray(lane_m[None, :] == np.arange(CHUNK)[:, None], f32)

    def by_lane(expo):
        pw = powers(np.stack(expo)[:, None, None, :], lr[..., None], li[..., None])
        return tuple(jnp.einsum('dgpm,ml->dgpl', v, rep_m, precision=_HI) for v in pw)

    q_r, q_i = by_lane([asc, desc])
    rhs = jnp.concatenate([ct_r * q_r - ct_i * q_i, ct_r * q_i + ct_i * q_r], axis=2)
    toep = _toeplitz(jnp.concatenate([bt_r, -bt_i], axis=3), rhs)

    v_r, v_i = by_lane([asc + 1, desc + 1])
    cs_r = ct_r * v_r - ct_i * v_i
    cs_i = -(ct_r * v_i + ct_i * v_r)

    w_r, w_i = (jnp.repeat(v, S5_CG, axis=2) for v in
                powers(np.stack([desc, asc])[:, None, :, None], lr[:, :, None, :], li[:, :, None, :]))
    rows_r, rows_i = jnp.tile(bt_r, (1, 1, CHUNK, 1)), jnp.tile(bt_i, (1, 1, CHUNK, 1))
    bs_r = w_r * rows_r - w_i * rows_i
    bs_i = w_r * rows_i + w_i * rows_r
    even = jnp.asarray((np.arange(g) % 2) == 0)[None, :, None, None]
    bs = jnp.concatenate([jnp.where(even, bs_r, 0.0), jnp.where(even, 0.0, bs_r),
                          jnp.where(even, bs_i, 0.0), jnp.where(even, 0.0, bs_i)], axis=3)
    cs = jnp.concatenate([jnp.where(even, cs_r, 0.0), jnp.where(even, 0.0, cs_r),
                          jnp.where(even, cs_i, 0.0), jnp.where(even, 0.0, cs_i)], axis=2)

    e_r, e_i = jnp.exp(CHUNK * lr), CHUNK * li
    half = 2 * p
    a_pow = jnp.stack([(e_r * jnp.cos(e_i)).reshape(2, g // 2, half),
                       (e_r * jnp.sin(e_i)).reshape(2, g // 2, half)], axis=2)
    dvec = jnp.dot(d_skip.astype(f32), tile_c, precision=_HI)[:, None, :]
    return toep, bs.astype(_MXU_DTYPE), cs.astype(_MXU_DTYPE), a_pow, dvec


def _row_pitch(n):
    return n + (4 - n) % 8


def _s5_kernel(nb, xl_ref, xc_ref, toep_ref, bs_ref, cs_ref, apow_ref, dvec_ref, y_ref, zc_s, zl_s, h_s):
    rows_l = xl_ref.shape[1]
    rows_c = xc_ref.shape[1]
    n_l, n_c = rows_l // nb, rows_c // nb
    pitch_l, pitch_c = _row_pitch(n_l), _row_pitch(n_c)
    half = zl_s.shape[3]
    xl = [xl_ref[g] for g in range(2)]
    xlb = [v.astype(_MXU_DTYPE) for v in xl]
    xcb = [xc_ref[g].astype(_MXU_DTYPE) for g in range(2)]

    for d in range(2):
        zc = _mm(xcb[0], bs_ref[d, 0]) + _mm(xcb[1], bs_ref[d, 1])
        zl = _mm(xlb[0], bs_ref[d, 0]) + _mm(xlb[1], bs_ref[d, 1])
        for ri in range(2):
            for b in range(nb):
                zc_s[d, ri, b * pitch_c:b * pitch_c + n_c, :] = zc[b * n_c:(b + 1) * n_c, ri * half:(ri + 1) * half]
                zl_s[d, ri, b * pitch_l:b * pitch_l + n_l, :] = zl[b * n_l:(b + 1) * n_l, ri * half:(ri + 1) * half]

    a_r = [jnp.broadcast_to(apow_ref[d, 0, 0:1, :], (nb, half)) for d in range(2)]
    a_i = [jnp.broadcast_to(apow_ref[d, 0, 1:2, :], (nb, half)) for d in range(2)]

    def step(d, n, latent, h_r, h_i):
        if latent:
            h_s[d, 0, pl.ds(n, nb, stride=pitch_l), :] = h_r
            h_s[d, 1, pl.ds(n, nb, stride=pitch_l), :] = h_i
            z_r = zl_s[d, 0, pl.ds(n, nb, stride=pitch_l), :]
            z_i = zl_s[d, 1, pl.ds(n, nb, stride=pitch_l), :]
        else:
            z_r = zc_s[d, 0, pl.ds(n, nb, stride=pitch_c), :]
            z_i = zc_s[d, 1, pl.ds(n, nb, stride=pitch_c), :]
        return (a_r[d] * h_r - a_i[d] * h_i + z_r,
                a_r[d] * h_i + a_i[d] * h_r + z_i)

    def ctx_body(n, carry):
        f_r, f_i, b_r, b_i = carry
        f_r, f_i = step(0, n, False, f_r, f_i)
        b_r, b_i = step(1, n_c - 1 - n, False, b_r, b_i)
        return f_r, f_i, b_r, b_i

    def lat_body(n, carry):
        f_r, f_i, b_r, b_i = carry
        f_r, f_i = step(0, n, True, f_r, f_i)
        b_r, b_i = step(1, n_l - 1 - n, True, b_r, b_i)
        return f_r, f_i, b_r, b_i

    zero = jnp.zeros((nb, half), jnp.float32)
    carry = lax.fori_loop(0, n_c, ctx_body, (zero, zero, zero, zero), unroll=SCAN_UNROLL)
    lax.fori_loop(0, n_l, lat_body, carry, unroll=SCAN_UNROLL)

    def plane(d, ri):
        return jnp.concatenate([h_s[d, ri, b * pitch_l:b * pitch_l + n_l, :] for b in range(nb)], axis=0)

    hb = [jnp.concatenate([plane(d, 0), plane(d, 1)], axis=1).astype(_MXU_DTYPE) for d in range(2)]
    for g in range(2):
        y = _mm(xlb[g], toep_ref[g]) + _mm(hb[0], cs_ref[0, g]) + _mm(hb[1], cs_ref[1, g])
        y_ref[g] = y + xl[g] * dvec_ref[g]


def _s5(xl, xc, toep, bs, cs, a_pow, dvec, nb):
    g, rows_l, w = xl.shape
    rows_c = xc.shape[1]
    sw = bs.shape[3]
    return pl.pallas_call(
        functools.partial(_s5_kernel, nb),
        grid=(g // 2,),
        in_specs=[pl.BlockSpec((2, rows_l, w), lambda k: (k, 0, 0)),
                  pl.BlockSpec((2, rows_c, w), lambda k: (k, 0, 0)),
                  pl.BlockSpec((2, w, w), lambda k: (k, 0, 0)),
                  pl.BlockSpec((2, 2, w, sw), lambda k: (0, k, 0, 0)),
                  pl.BlockSpec((2, 2, sw, w), lambda k: (0, k, 0, 0)),
                  pl.BlockSpec((2, 1, 2, sw // 2), lambda k: (0, k, 0, 0)),
                  pl.BlockSpec((2, 1, w), lambda k: (k, 0, 0))],
        out_specs=pl.BlockSpec((2, rows_l, w), lambda k: (k, 0, 0)),
        out_shape=jax.ShapeDtypeStruct((g, rows_l, w), jnp.float32),
        scratch_shapes=[pltpu.VMEM((2, 2, nb * _row_pitch(rows_c // nb), sw // 2), jnp.float32),
                        pltpu.VMEM((2, 2, nb * _row_pitch(rows_l // nb), sw // 2), jnp.float32),
                        pltpu.VMEM((2, 2, nb * _row_pitch(rows_l // nb), sw // 2), jnp.float32)],
        compiler_params=pltpu.CompilerParams(dimension_semantics=("arbitrary",),
                                             vmem_limit_bytes=VMEM_LIMIT_BYTES),
        name="s5",
    )(xl, xc, toep, bs, cs, a_pow, dvec)


def _na_row_plan():
    last_q = NA_KROWS - NA_QROWS
    plan = []
    for r0s, qrs in ((lambda lr: 0, lambda lr: lr),
                     (lambda lr: lr, lambda lr: NA_KR // 2 + lr),
                     (lambda lr: last_q - NA_KR // 2, lambda lr: last_q + lr)):
        plan.append([[kl - qrs(lr) + (NA_KR - 1) if r0s(lr) <= kl < r0s(lr) + NA_KR else None
                      for kl in range(NA_KROWS)] for lr in range(NA_QROWS)])
    return plan


def _na_bias_kernel(plan, col_ref, o_ref):
    lane = lax.broadcasted_iota(jnp.int32, (1, 2 * GRID_W), 1)
    masked = jnp.full((GRID_W, 2 * GRID_W), MASK_VALUE, jnp.float32)
    for kind in range(len(plan)):
        for kl in range(NA_KROWS):
            for lp in range(NA_QROWS // 2):
                a0, a1 = plan[kind][2 * lp][kl], plan[kind][2 * lp + 1][kl]
                t0 = masked if a0 is None else col_ref[0, a0]
                t1 = masked if a1 is None else col_ref[0, a1]
                tile = t0 if a0 == a1 else jnp.where(lane < GRID_W, t0, t1)
                o_ref[kind, 0, kl * GRID_W:(kl + 1) * GRID_W, lp * 2 * GRID_W:(lp + 1) * 2 * GRID_W] = tile


def _na_bias(rpb):
    heads = rpb.shape[0]
    kc = np.arange(GRID_W)[:, None]
    qc = np.arange(GRID_W)[None, :]
    c0 = np.clip(qc - NA_KC // 2, 0, GRID_W - NA_KC)
    col_ok = (kc >= c0) & (kc < c0 + NA_KC)
    sel_c = ((kc - qc + (NA_KC - 1))[:, :, None] == np.arange(2 * NA_KC - 1)) & col_ok[:, :, None]
    by_col = jnp.einsum('hab,kqb->hakq', rpb.astype(jnp.float32), jnp.asarray(sel_c, jnp.float32), precision=_HI)
    by_col = jnp.where(jnp.asarray(col_ok), by_col * LOG2_E, MASK_VALUE)
    by_col = jnp.concatenate([by_col, by_col], axis=3)
    plan = _na_row_plan()
    nk = NA_KROWS * GRID_W
    return pl.pallas_call(
        functools.partial(_na_bias_kernel, plan),
        grid=(heads,),
        in_specs=[pl.BlockSpec((1,) + by_col.shape[1:], lambda h: (h, 0, 0, 0))],
        out_specs=pl.BlockSpec((len(plan), 1, nk, NA_BLOCK), lambda h: (0, h, 0, 0)),
        out_shape=jax.ShapeDtypeStruct((len(plan), heads, nk, NA_BLOCK), jnp.float32),
        compiler_params=pltpu.CompilerParams(dimension_semantics=("arbitrary",)),
        name="na_bias",
    )(by_col)


def _row_max(x):
    while x.shape[0] % (2 * SUBLANES) == 0:
        half = x.shape[0] // 2
        x = jnp.maximum(x[:half], x[half:])
    return jnp.max(x, axis=0, keepdims=True)


def _na_kernel(rows, q_ref, k_ref, vt_ref, kc_ref, vct_ref, bias_ref, o_ref):
    n_rb = rows // NA_QROWS
    nk = NA_KROWS * GRID_W
    pair = 2 * NA_DH
    lane_first = lax.broadcasted_iota(jnp.int32, (1, pair), 1) < NA_DH
    row_first = lax.broadcasted_iota(jnp.int32, (pair, 1), 0) < NA_DH
    n_heads = q_ref.shape[2] // NA_DH
    n_blocks = q_ref.shape[1] // NA_BLOCK

    def block(j):
        rb = pl.program_id(1) * n_blocks + j
        kind = jnp.where(rb == 0, 0, jnp.where(rb == n_rb - 1, 2, 1))
        first_row = jnp.clip(rb * NA_QROWS - NA_KR // 2, 0, rows - NA_KROWS)
        return kind, pl.multiple_of(first_row * GRID_W, GRID_W), first_row // NA_QROWS

    geometry = [block(j) for j in range(n_blocks)]

    def scores(u):
        j, h = divmod(u, n_heads)
        kind, start, _ = geometry[j]
        cols = slice(h // 2 * pair, (h // 2 + 1) * pair)
        lane_mine = lane_first if h % 2 == 0 else jnp.logical_not(lane_first)
        qm = jnp.where(lane_mine, q_ref[0, j * NA_BLOCK:(j + 1) * NA_BLOCK, cols],
                       jnp.zeros((NA_BLOCK, pair), q_ref.dtype))
        s_w = _mm_nt(k_ref[0, pl.ds(start, nk), cols], qm) + bias_ref[kind, h]
        s_c = _mm_nt(kc_ref[0, :, cols], qm)
        return s_w, s_c

    def weights(s_w, s_c):
        mx = jnp.maximum(_row_max(s_w), _row_max(s_c))
        return jnp.exp2(s_w - mx).astype(vt_ref.dtype), jnp.exp2(s_c - mx).astype(vt_ref.dtype)

    def values(u, p_w, p_c):
        j, h = divmod(u, n_heads)
        blk0 = geometry[j][2]
        cols = slice(h // 2 * pair, (h // 2 + 1) * pair)
        row_mine = row_first if h % 2 == 0 else jnp.logical_not(row_first)
        vtw = jnp.concatenate([vt_ref[blk0 + t, cols, :] for t in range(nk // NA_BLOCK)], axis=1)
        vtc = jnp.concatenate([vct_ref[t, cols, :] for t in range(vct_ref.shape[0])], axis=1)
        o = (_mm(jnp.where(row_mine, vtw, jnp.ones_like(vtw)), p_w)
             + _mm(jnp.where(row_mine, vtc, jnp.ones_like(vtc)), p_c))
        num, den = (o[:NA_DH], o[NA_DH:]) if h % 2 == 0 else (o[NA_DH:], o[:NA_DH])
        return num / den

    n_units = n_blocks * n_heads
    s = {0: scores(0), 1: scores(1)}
    p = {0: weights(*s.pop(0))}
    outs = []
    for u in range(n_units):
        if u + 2 < n_units:
            s[u + 2] = scores(u + 2)
        if u + 1 < n_units:
            p[u + 1] = weights(*s.pop(u + 1))
        outs.append(values(u, *p.pop(u)))
        if len(outs) == n_heads:
            j = u // n_heads
            o_ref[0, j * NA_BLOCK:(j + 1) * NA_BLOCK, :] = jnp.concatenate(outs, axis=0).T.astype(o_ref.dtype)
            outs = []


def _na(qk, vt, qk_c, vt_c, bias, na_w):
    b, s, _ = qk.shape
    l = qk_c.shape[1]
    rows = s // GRID_W
    tq = NA_STEP_BLOCKS * NA_BLOCK
    return pl.pallas_call(
        functools.partial(_na_kernel, rows),
        grid=(b, s // tq),
        in_specs=[pl.BlockSpec((1, tq, na_w), lambda bi, rb: (bi, rb, 0)),
                  pl.BlockSpec((1, s, na_w), lambda bi, rb: (bi, 0, 1)),
                  pl.BlockSpec((s // NA_BLOCK, na_w, NA_BLOCK), lambda bi, rb: (bi, 0, 0)),
                  pl.BlockSpec((1, l, na_w), lambda bi, rb: (bi, 0, 1)),
                  pl.BlockSpec((l // NA_BLOCK, na_w, NA_BLOCK), lambda bi, rb: (bi, 0, 0)),
                  _const_spec(bias.shape)],
        out_specs=pl.BlockSpec((1, tq, na_w), lambda bi, rb: (bi, rb, 0)),
        out_shape=jax.ShapeDtypeStruct((b, s, na_w), _MXU_DTYPE),
        compiler_params=pltpu.CompilerParams(
            dimension_semantics=("arbitrary", "arbitrary"),
            vmem_limit_bytes=VMEM_LIMIT_BYTES),
        name="na",
    )(qk, qk, vt, qk_c, vt_c, bias)


def _out_ffn_kernel(alpha, x1_ref, yna_ref, ys5_ref, mod_a_ref, mod_b_ref, lng_ref, lnb_ref, wglu_ref, bglu_ref,
                    wout_ref, wup_ref, wdn_ref, o_ref, p_s, y_s, h_s, x2_s):
    @pl.when(pl.program_id(0) == 0)
    def _():
        h_s[...] = jnp.zeros_like(h_s)
        x2_s[...] = jnp.zeros_like(x2_s)

    mod_a = mod_a_ref[0]
    na_w = yna_ref.shape[1]
    stage_a = {}

    def relayout():
        _chunks_to_tokens(ys5_ref, y_s)

    def gated():
        gl = jax.nn.gelu(jnp.concatenate([y_s[v] for v in range(y_s.shape[0])], axis=1))
        gate = jax.nn.sigmoid(_mm(gl.astype(p_s.dtype), wglu_ref[...]) + bglu_ref[...])
        stage_a["glu"] = (gl * gate).astype(p_s.dtype)

    def mixed():
        y = _mm(yna_ref[...], wout_ref[:na_w, :]) + _mm(stage_a["glu"], wout_ref[na_w:, :])
        x2 = _layer_norm(alpha * x1_ref[...] + mod_a[5:6] * y, lng_ref[1:2], lnb_ref[1:2])
        stage_a["x2"] = x2
        stage_a["h"] = (x2 * (1.0 + mod_a[7:8]) + mod_a[6:7]).astype(h_s.dtype)

    mod_b = mod_b_ref[0]
    _swiglu_hidden(h_s[...], wup_ref, p_s, between=(relayout, gated, mixed))
    n_blk = o_ref.shape[0] // NA_BLOCK
    fs = [_mm(p_s[t * NA_BLOCK:(t + 1) * NA_BLOCK, :], wdn_ref[...]) for t in range(n_blk)]
    for t in range(n_blk):
        rows = slice(t * NA_BLOCK, (t + 1) * NA_BLOCK)
        o_ref[rows, :] = _layer_norm(alpha * x2_s[rows, :] + (0.5 * mod_b[8:9]) * fs[t], lng_ref[2:3], lnb_ref[2:3])

    x2_s[...] = stage_a["x2"]
    h_s[...] = stage_a["h"]


def _out_ffn(x1, yna, ys5, mod, tiles_per_mod, lng, lnb, wglu, bglu, wout, wup, wdn, alpha, tm):
    t, d = x1.shape
    ff = wdn.shape[0]
    na_w = yna.shape[1]
    groups, _, cw = ys5.shape
    s5_w = groups * S5_CG
    n = t // tm

    def cur(i):
        return jnp.minimum(i, n - 1)

    def prev(i):
        return jnp.maximum(i - 1, 0)

    return pl.pallas_call(
        functools.partial(_out_ffn_kernel, alpha),
        grid=(n + 1,),
        in_specs=[pl.BlockSpec((tm, d), lambda i: (cur(i), 0)),
                  pl.BlockSpec((tm, na_w), lambda i: (cur(i), 0)),
                  pl.BlockSpec((groups, tm // CHUNK, cw), lambda i: (0, cur(i), 0)),
                  pl.BlockSpec((1, N_MOD, d), lambda i: (cur(i) // tiles_per_mod, 0, 0)),
                  pl.BlockSpec((1, N_MOD, d), lambda i: (prev(i) // tiles_per_mod, 0, 0)),
                  _const_spec(lng.shape), _const_spec(lnb.shape),
                  _const_spec(wglu.shape), _const_spec(bglu.shape), _const_spec(wout.shape),
                  _const_spec(wup.shape), _const_spec(wdn.shape)],
        out_specs=pl.BlockSpec((tm, d), lambda i: (prev(i), 0)),
        out_shape=jax.ShapeDtypeStruct((t, d), jnp.float32),
        scratch_shapes=[pltpu.VMEM((tm, ff), _MXU_DTYPE), pltpu.VMEM((s5_w // LANES, tm, LANES), jnp.float32),
                        pltpu.VMEM((tm, d), _MXU_DTYPE), pltpu.VMEM((tm, d), jnp.float32)],
        compiler_params=pltpu.CompilerParams(dimension_semantics=("arbitrary",),
                                             vmem_limit_bytes=VMEM_LIMIT_BYTES),
        name="out_ffn",
    )(x1, yna, ys5, mod, mod, lng, lnb, wglu, bglu, wout, wup, wdn)


def _pick_tile(n, target):
    t = min(n, target)
    while n % t:
        t //= 2
    return t


def kernel(x, c, ctx, c_ctx, w_ada, b_ada, ln_g, ln_b, ffn1_w_up, ffn1_w_down, w_in, na_rpb, s5_a_re, s5_a_im, s5_log_dt, s5_b_re, s5_b_im, s5_c_re, s5_c_im, s5_d, s5_w_glu, s5_b_glu, w_out, ffn2_w_up, ffn2_w_down):
    depth = w_ada.shape[0]
    assert depth == 1, "single-layer stack only"
    bsz, seq, d = x.shape
    l_ctx = ctx.shape[1]
    s5_w = s5_w_glu.shape[1]
    na_w = w_out.shape[1] - s5_w
    groups = s5_w // S5_CG
    rows = seq // GRID_W
    assert bsz % 8 == 0 and seq % (NA_STEP_BLOCKS * NA_BLOCK) == 0 and rows >= NA_KROWS
    assert l_ctx % NA_BLOCK == 0 and groups % 2 == 0 and 2 * s5_a_re.shape[3] == LANES
    assert na_w % (2 * NA_DH) == 0 and ffn1_w_down.shape[1] % MXU_TILE == 0
    alpha = (2.0 * depth) ** 0.25
    cd = _MXU_DTYPE

    n_rows = -(-(bsz + 1) // 8) * 8
    cc = jnp.zeros((n_rows, d), jnp.float32).at[:bsz].set(c).at[bsz].set(c_ctx)
    n_ada = w_ada.shape[2]
    mod = _ada(cc, w_ada[0], b_ada[0][None, :], _pick_tile(n_ada, 1152)).reshape(n_rows, N_MOD, d)

    lng, lnb = ln_g[0], ln_b[0]
    wup1, wdn1 = ffn1_w_up[0].astype(cd), ffn1_w_down[0].astype(cd)
    wup2, wdn2 = ffn2_w_up[0].astype(cd), ffn2_w_down[0].astype(cd)
    win = w_in[0].astype(cd)

    tm = _pick_tile(seq, 512)
    x1, qk, vt, xs = _ffn_in(x.reshape(bsz * seq, d), mod, seq // tm, lng, lnb, wup1, wdn1, win, alpha, na_w, tm)
    tmc = _pick_tile(bsz * l_ctx, 512)
    _, qk_c, vt_c, xs_c = _ffn_in(ctx.reshape(bsz * l_ctx, d), mod[bsz:bsz + 1], None, lng, lnb, wup1, wdn1, win,
                                  alpha, na_w, tmc)

    tables = _s5_tables(s5_a_re[0], s5_a_im[0], s5_log_dt[0], s5_b_re[0], s5_b_im[0], s5_c_re[0], s5_c_im[0],
                        s5_d[0])
    y_s5 = _s5(xs, xs_c, *tables, bsz)

    y_na = _na(qk.reshape(bsz, seq, 2 * na_w), vt, qk_c.reshape(bsz, l_ctx, 2 * na_w), vt_c, _na_bias(na_rpb[0]),
               na_w).reshape(bsz * seq, na_w)

    out = _out_ffn(x1, y_na, y_s5, mod, seq // tm, lng, lnb, s5_w_glu[0].astype(cd), s5_b_glu[0][None, :],
                   w_out[0].astype(cd), wup2, wdn2, alpha, tm)
    return out.reshape(bsz, seq, d)
```

```python
import functools

import jax
import jax.numpy as jnp
import numpy as np
from jax import lax
from jax.experimental import pallas as pl
from jax.experimental.pallas import tpu as pltpu

GRID_W = 64
NA_DH = 64
NA_KR = 8
NA_KC = 16
S5_CG = 16
S5_P = 64
LN_EPS = 1e-6
N_MOD = 9

CHUNK = 16
NA_QROWS = 4
NA_KROWS = 12
NA_BLOCK = NA_QROWS * GRID_W
NA_STEP_BLOCKS = 4
MASK_VALUE = -1e30
LOG2_E = 1.4426950408889634
SCAN_UNROLL = 4

MXU_TILE = 256
LANES = 128
SUBLANES = 8
VMEM_LIMIT_BYTES = 56 * 1024 * 1024

_MXU_DTYPE = jnp.bfloat16
_HI = lax.Precision.HIGHEST


def _mm(a, b):
    return jnp.dot(a, b, preferred_element_type=jnp.float32)


def _mm_nt(a, b):
    return lax.dot_general(a, b, (((1,), (1,)), ((), ())), preferred_element_type=jnp.float32)


def _const_spec(shape):
    nd = len(shape)
    return pl.BlockSpec(shape, lambda *_: (0,) * nd, pipeline_mode=pl.Buffered(1))


def _layer_norm(r, g, b):
    mu = jnp.mean(r, axis=-1, keepdims=True)
    d = r - mu
    var = jnp.mean(d * d, axis=-1, keepdims=True)
    return d * lax.rsqrt(var + LN_EPS) * g + b


def _swiglu_hidden(h, wup_ref, p_s, between=()):
    ff = p_s.shape[1]
    n_chunks = ff // MXU_TILE
    slots = [(k + 1) * n_chunks // (len(between) + 1) for k in range(len(between))]
    for j in range(n_chunks):
        for slot, fn in zip(slots, between):
            if slot == j:
                fn()
        lo = j * MXU_TILE
        a = _mm(h, wup_ref[:, lo:lo + MXU_TILE])
        g = _mm(h, wup_ref[:, ff + lo:ff + lo + MXU_TILE])
        p_s[:, lo:lo + MXU_TILE] = (g * jax.nn.sigmoid(g) * a).astype(p_s.dtype)


def _swiglu(h, wup_ref, wdn_ref, p_s, between=()):
    _swiglu_hidden(h, wup_ref, p_s, between)
    return _mm(p_s[...], wdn_ref[...])


def _block_transpose(tiles):
    per = len(tiles)
    blk = lax.broadcasted_iota(jnp.int32, (1, LANES), 1) // S5_CG
    tiles = list(tiles)
    d = per // 2
    while d:
        upper = (blk & d) != 0
        for a in range(per):
            if a & d:
                continue
            lo, hi = tiles[a], tiles[a + d]
            tiles[a] = jnp.where(upper, pltpu.roll(hi, S5_CG * d, axis=1), lo)
            tiles[a + d] = jnp.where(upper, hi, pltpu.roll(lo, LANES - S5_CG * d, axis=1))
        d //= 2
    return tiles


def _tokens_to_chunks(u_s, xs_ref, row0, n_rows):
    n_ch = n_rows // CHUNK
    ch0 = row0 // CHUNK
    per = LANES // S5_CG
    for v in range(u_s.shape[0]):
        for w in range(CHUNK // per):
            by_group = _block_transpose(
                [u_s[v, pl.ds(row0 + per * w + jj, n_ch, stride=CHUNK), :] for jj in range(per)])
            for gg in range(per):
                xs_ref[per * v + gg, ch0:ch0 + n_ch, w * LANES:(w + 1) * LANES] = by_group[gg].astype(xs_ref.dtype)


def _chunks_to_tokens(ys_ref, y_s):
    n_ch = y_s.shape[1] // CHUNK
    per = LANES // S5_CG
    for v in range(y_s.shape[0]):
        for w in range(CHUNK // per):
            by_phase = _block_transpose([ys_ref[per * v + gg, :, w * LANES:(w + 1) * LANES] for gg in range(per)])
            for jj in range(per):
                y_s[v, pl.ds(per * w + jj, n_ch, stride=CHUNK), :] = by_phase[jj]


def _ada_kernel(c_ref, w_ref, b_ref, o_ref):
    cc = c_ref[...]
    s = cc * jax.nn.sigmoid(cc)
    w = w_ref[...]
    s_hi = s.astype(_MXU_DTYPE)
    s_lo = (s - s_hi.astype(jnp.float32)).astype(_MXU_DTYPE)
    w_hi = w.astype(_MXU_DTYPE)
    w_lo = (w - w_hi.astype(jnp.float32)).astype(_MXU_DTYPE)
    o_ref[...] = _mm(s_hi, w_hi) + (_mm(s_lo, w_hi) + _mm(s_hi, w_lo)) + b_ref[...]


def _ada(cc, w, b, bn):
    rows, d = cc.shape
    n = w.shape[1]
    return pl.pallas_call(
        _ada_kernel,
        grid=(n // bn,),
        in_specs=[pl.BlockSpec((rows, d), lambda j: (0, 0)),
                  pl.BlockSpec((d, bn), lambda j: (0, j)),
                  pl.BlockSpec((1, bn), lambda j: (0, j))],
        out_specs=pl.BlockSpec((rows, bn), lambda j: (0, j)),
        out_shape=jax.ShapeDtypeStruct((rows, n), jnp.float32),
        compiler_params=pltpu.CompilerParams(dimension_semantics=("arbitrary",),
                                             vmem_limit_bytes=VMEM_LIMIT_BYTES),
        name="ada",
    )(cc, w, b)


def _ffn_in_kernel(alpha, na_w, x_ref, mod_ref, lng_ref, lnb_ref, wup_ref, wdn_ref, win_ref,
                   x1_ref, qk_ref, vt_ref, xs_ref, p_s, u_s):
    mod = mod_ref[0]
    h = (x_ref[...] * (1.0 + mod[1:2]) + mod[0:1]).astype(p_s.dtype)
    _swiglu_hidden(h, wup_ref, p_s)
    n_blk = vt_ref.shape[0]
    ys = [_mm(p_s[t * NA_BLOCK:(t + 1) * NA_BLOCK, :], wdn_ref[...]) for t in range(n_blk)]
    for t in range(n_blk):
        rows = slice(t * NA_BLOCK, (t + 1) * NA_BLOCK)
        x1 = _layer_norm(alpha * x_ref[rows, :] + (0.5 * mod[2:3]) * ys[t], lng_ref[0:1], lnb_ref[0:1])
        x1_ref[rows, :] = x1
        h2 = (x1 * (1.0 + mod[4:5]) + mod[3:4]).astype(p_s.dtype)
        pu = _mm(h2, win_ref[:, 3 * na_w:])
        for v in range(u_s.shape[0]):
            u_s[v, rows, :] = pu[:, v * LANES:(v + 1) * LANES]
        _tokens_to_chunks(u_s, xs_ref, t * NA_BLOCK, NA_BLOCK)
        pr = _mm(h2, win_ref[:, :3 * na_w])
        qk_ref[rows, :na_w] = (pr[:, :na_w] * (NA_DH ** -0.5 * LOG2_E)).astype(qk_ref.dtype)
        qk_ref[rows, na_w:] = pr[:, na_w:2 * na_w].astype(qk_ref.dtype)
        vt_ref[t] = pr[:, 2 * na_w:].T.astype(vt_ref.dtype)


def _ffn_in(x2d, mod, tiles_per_mod, lng, lnb, wup, wdn, win, alpha, na_w, tm):
    t, d = x2d.shape
    ff = wdn.shape[0]
    ncol = win.shape[1]
    s5_w = ncol - 3 * na_w
    groups, cw = s5_w // S5_CG, CHUNK * S5_CG
    if tiles_per_mod is None:
        mod_map = lambda i: (0, 0, 0)
    else:
        mod_map = lambda i: (i // tiles_per_mod, 0, 0)
    return pl.pallas_call(
        functools.partial(_ffn_in_kernel, alpha, na_w),
        grid=(t // tm,),
        in_specs=[pl.BlockSpec((tm, d), lambda i: (i, 0)),
                  pl.BlockSpec((1, N_MOD, d), mod_map),
                  _const_spec(lng.shape), _const_spec(lnb.shape),
                  _const_spec(wup.shape), _const_spec(wdn.shape), _const_spec(win.shape)],
        out_specs=[pl.BlockSpec((tm, d), lambda i: (i, 0)),
                   pl.BlockSpec((tm, 2 * na_w), lambda i: (i, 0)),
                   pl.BlockSpec((tm // NA_BLOCK, na_w, NA_BLOCK), lambda i: (i, 0, 0)),
                   pl.BlockSpec((groups, tm // CHUNK, cw), lambda i: (0, i, 0))],
        out_shape=[jax.ShapeDtypeStruct((t, d), jnp.float32),
                   jax.ShapeDtypeStruct((t, 2 * na_w), _MXU_DTYPE),
                   jax.ShapeDtypeStruct((t // NA_BLOCK, na_w, NA_BLOCK), _MXU_DTYPE),
                   jax.ShapeDtypeStruct((groups, t // CHUNK, cw), jnp.float32)],
        scratch_shapes=[pltpu.VMEM((tm, ff), _MXU_DTYPE), pltpu.VMEM((s5_w // LANES, tm, LANES), jnp.float32)],
        compiler_params=pltpu.CompilerParams(dimension_semantics=("arbitrary",),
                                             vmem_limit_bytes=VMEM_LIMIT_BYTES),
        name="ffn_in",
    )(x2d, mod, lng, lnb, wup, wdn, win)


def _toeplitz_kernel(lhs_ref, rhs_ref, o_ref):
    width = o_ref.shape[2]
    lane = lax.broadcasted_iota(jnp.int32, (1, width), 1)
    for g in range(o_ref.shape[0]):
        k_f = jnp.dot(lhs_ref[0, g], rhs_ref[0, g], precision=_HI, preferred_element_type=jnp.float32)
        k_b = jnp.dot(lhs_ref[1, g], rhs_ref[1, g], precision=_HI, preferred_element_type=jnp.float32)
        for i in range(CHUNK):
            lo, hi = i * S5_CG, (i + 1) * S5_CG
            fwd = k_f if lo == 0 else jnp.where(lane >= lo, pltpu.roll(k_f, lo, axis=1), 0.0)
            bwd = k_b if hi == width else jnp.where(lane < hi, pltpu.roll(k_b, hi, axis=1), 0.0)
            o_ref[g, lo:hi, :] = (fwd + bwd).astype(o_ref.dtype)


def _toeplitz(lhs, rhs):
    _, g, cg, kdim = lhs.shape
    width = rhs.shape[3]
    gb = _pick_tile(g, 8)
    return pl.pallas_call(
        _toeplitz_kernel,
        grid=(g // gb,),
        in_specs=[pl.BlockSpec((2, gb, cg, kdim), lambda k: (0, k, 0, 0)),
                  pl.BlockSpec((2, gb, kdim, width), lambda k: (0, k, 0, 0))],
        out_specs=pl.BlockSpec((gb, width, width), lambda k: (k, 0, 0)),
        out_shape=jax.ShapeDtypeStruct((g, width, width), _MXU_DTYPE),
        compiler_params=pltpu.CompilerParams(dimension_semantics=("arbitrary",)),
        name="s5_toeplitz",
    )(lhs, rhs)


def _s5_tables(a_re, a_im, log_dt, b_re, b_im, c_re, c_im, d_skip):
    g, p = a_re.shape[1], a_re.shape[2]
    f32 = jnp.float32
    a_re, a_im = a_re.astype(f32), a_im.astype(f32)
    dt = jnp.exp(log_dt.astype(f32))[..., None]
    lr, li = a_re * dt, a_im * dt

    width = CHUNK * S5_CG
    asc = np.arange(CHUNK)
    desc = asc[::-1].copy()
    lane_m, lane_c = np.arange(width) // S5_CG, np.arange(width) % S5_CG

    def powers(expo, lr_b, li_b):
        e = jnp.asarray(expo, f32)
        mag = jnp.exp(e * lr_b)
        return mag * jnp.cos(e * li_b), mag * jnp.sin(e * li_b)


    ab_r, ab_i = jnp.exp(lr) * jnp.cos(li), jnp.exp(lr) * jnp.sin(li)
    den = a_re ** 2 + a_im ** 2
    nr, ni = ab_r - 1.0, ab_i
    sr = ((nr * a_re + ni * a_im) / den)[:, :, None, :]
    si = ((ni * a_re - nr * a_im) / den)[:, :, None, :]
    bt_re, bt_im = jnp.swapaxes(b_re, 2, 3).astype(f32), jnp.swapaxes(b_im, 2, 3).astype(f32)
    bt_r = sr * bt_re - si * bt_im
    bt_i = sr * bt_im + si * bt_re

    tile_c = jnp.asarray(lane_c[None, :] == np.arange(S5_CG)[:, None], f32)
    ct_r = jnp.einsum('dgcp,cl->dgpl', c_re.astype(f32), tile_c, precision=_HI)
    ct_i = jnp.einsum('dgcp,cl->dgpl', c_im.astype(f32), tile_c, precision=_HI)

    rep_m = jnp.asarray(lane_m[None, :] == np.arange(CHUNK)[:, None], f32)

    def by_lane(expo):
        pw = powers(np.stack(expo)[:, None, None, :], lr[..., None], li[..., None])
        return tuple(jnp.einsum('dgpm,ml->dgpl', v, rep_m, precision=_HI) for v in pw)

    q_r, q_i = by_lane([asc, desc])
    rhs = jnp.concatenate([ct_r * q_r - ct_i * q_i, ct_r * q_i + ct_i * q_r], axis=2)
    toep = _toeplitz(jnp.concatenate([bt_r, -bt_i], axis=3), rhs)

    v_r, v_i = by_lane([asc + 1, desc + 1])
    cs_r = ct_r * v_r - ct_i * v_i
    cs_i = -(ct_r * v_i + ct_i * v_r)

    w_r, w_i = (jnp.repeat(v, S5_CG, axis=2) for v in
                powers(np.stack([desc, asc])[:, None, :, None], lr[:, :, None, :], li[:, :, None, :]))
    rows_r, rows_i = jnp.tile(bt_r, (1, 1, CHUNK, 1)), jnp.tile(bt_i, (1, 1, CHUNK, 1))
    bs_r = w_r * rows_r - w_i * rows_i
    bs_i = w_r * rows_i + w_i * rows_r
    even = jnp.asarray((np.arange(g) % 2) == 0)[None, :, None, None]
    bs = jnp.concatenate([jnp.where(even, bs_r, 0.0), jnp.where(even, 0.0, bs_r),
                          jnp.where(even, bs_i, 0.0), jnp.where(even, 0.0, bs_i)], axis=3)
    cs = jnp.concatenate([jnp.where(even, cs_r, 0.0), jnp.where(even, 0.0, cs_r),
                          jnp.where(even, cs_i, 0.0), jnp.where(even, 0.0, cs_i)], axis=2)

    e_r, e_i = jnp.exp(CHUNK * lr), CHUNK * li
    half = 2 * p
    a_pow = jnp.stack([(e_r * jnp.cos(e_i)).reshape(2, g // 2, half),
                       (e_r * jnp.sin(e_i)).reshape(2, g // 2, half)], axis=2)
    dvec = jnp.dot(d_skip.astype(f32), tile_c, precision=_HI)[:, None, :]
    return toep, bs.astype(_MXU_DTYPE), cs.astype(_MXU_DTYPE), a_pow, dvec


def _row_pitch(n):
    return n + (4 - n) % 8


def _s5_kernel(nb, n_cast, xl_ref, xc_ref, toep_ref, bs_ref, cs_ref, apow_ref, dvec_ref, *refs):
    cast_in, (y_ref, *cast_out), (zc_s, zl_s, h_s) = refs[:n_cast], refs[n_cast:2 * n_cast + 1], refs[2 * n_cast + 1:]
    for src, dst in zip(cast_in, cast_out):
        dst[...] = src[...].astype(dst.dtype)
    rows_l = xl_ref.shape[1]
    rows_c = xc_ref.shape[1]
    n_l, n_c = rows_l // nb, rows_c // nb
    pitch_l, pitch_c = _row_pitch(n_l), _row_pitch(n_c)
    half = zl_s.shape[3]
    xl = [xl_ref[g] for g in range(2)]
    xlb = [v.astype(_MXU_DTYPE) for v in xl]
    xcb = [xc_ref[g].astype(_MXU_DTYPE) for g in range(2)]

    for d in range(2):
        zc = _mm(xcb[0], bs_ref[d, 0]) + _mm(xcb[1], bs_ref[d, 1])
        zl = _mm(xlb[0], bs_ref[d, 0]) + _mm(xlb[1], bs_ref[d, 1])
        for ri in range(2):
            for b in range(nb):
                zc_s[d, ri, b * pitch_c:b * pitch_c + n_c, :] = zc[b * n_c:(b + 1) * n_c, ri * half:(ri + 1) * half]
                zl_s[d, ri, b * pitch_l:b * pitch_l + n_l, :] = zl[b * n_l:(b + 1) * n_l, ri * half:(ri + 1) * half]

    a_r = [jnp.broadcast_to(apow_ref[d, 0, 0:1, :], (nb, half)) for d in range(2)]
    a_i = [jnp.broadcast_to(apow_ref[d, 0, 1:2, :], (nb, half)) for d in range(2)]

    def step(d, n, latent, h_r, h_i):
        if latent:
            h_s[d, 0, pl.ds(n, nb, stride=pitch_l), :] = h_r
            h_s[d, 1, pl.ds(n, nb, stride=pitch_l), :] = h_i
            z_r = zl_s[d, 0, pl.ds(n, nb, stride=pitch_l), :]
            z_i = zl_s[d, 1, pl.ds(n, nb, stride=pitch_l), :]
        else:
            z_r = zc_s[d, 0, pl.ds(n, nb, stride=pitch_c), :]
            z_i = zc_s[d, 1, pl.ds(n, nb, stride=pitch_c), :]
        return (a_r[d] * h_r - a_i[d] * h_i + z_r,
                a_r[d] * h_i + a_i[d] * h_r + z_i)

    def ctx_body(n, carry):
        f_r, f_i, b_r, b_i = carry
        f_r, f_i = step(0, n, False, f_r, f_i)
        b_r, b_i = step(1, n_c - 1 - n, False, b_r, b_i)
        return f_r, f_i, b_r, b_i

    def lat_body(n, carry):
        f_r, f_i, b_r, b_i = carry
        f_r, f_i = step(0, n, True, f_r, f_i)
        b_r, b_i = step(1, n_l - 1 - n, True, b_r, b_i)
        return f_r, f_i, b_r, b_i

    zero = jnp.zeros((nb, half), jnp.float32)
    carry = lax.fori_loop(0, n_c, ctx_body, (zero, zero, zero, zero), unroll=SCAN_UNROLL)
    lax.fori_loop(0, n_l, lat_body, carry, unroll=SCAN_UNROLL)

    def plane(d, ri):
        return jnp.concatenate([h_s[d, ri, b * pitch_l:b * pitch_l + n_l, :] for b in range(nb)], axis=0)

    hb = [jnp.concatenate([plane(d, 0), plane(d, 1)], axis=1).astype(_MXU_DTYPE) for d in range(2)]
    for g in range(2):
        y = _mm(xlb[g], toep_ref[g]) + _mm(hb[0], cs_ref[0, g]) + _mm(hb[1], cs_ref[1, g])
        y_ref[g] = y + xl[g] * dvec_ref[g]


def _s5(xl, xc, toep, bs, cs, a_pow, dvec, nb, to_cast):
    g, rows_l, w = xl.shape
    rows_c = xc.shape[1]
    sw = bs.shape[3]
    steps = g // 2
    slabs = [m.shape[0] // steps for m in to_cast]
    assert all(m.shape[0] == s * steps and s % (2 * SUBLANES) == 0 for m, s in zip(to_cast, slabs))
    cast_specs = [pl.BlockSpec((s, m.shape[1]), lambda k: (k, 0)) for m, s in zip(to_cast, slabs)]
    return pl.pallas_call(
        functools.partial(_s5_kernel, nb, len(to_cast)),
        grid=(steps,),
        in_specs=[pl.BlockSpec((2, rows_l, w), lambda k: (k, 0, 0)),
                  pl.BlockSpec((2, rows_c, w), lambda k: (k, 0, 0)),
                  pl.BlockSpec((2, w, w), lambda k: (k, 0, 0)),
                  pl.BlockSpec((2, 2, w, sw), lambda k: (0, k, 0, 0)),
                  pl.BlockSpec((2, 2, sw, w), lambda k: (0, k, 0, 0)),
                  pl.BlockSpec((2, 1, 2, sw // 2), lambda k: (0, k, 0, 0)),
                  pl.BlockSpec((2, 1, w), lambda k: (k, 0, 0))] + cast_specs,
        out_specs=[pl.BlockSpec((2, rows_l, w), lambda k: (k, 0, 0))] + cast_specs,
        out_shape=[jax.ShapeDtypeStruct((g, rows_l, w), jnp.float32)]
        + [jax.ShapeDtypeStruct(m.shape, _MXU_DTYPE) for m in to_cast],
        scratch_shapes=[pltpu.VMEM((2, 2, nb * _row_pitch(rows_c // nb), sw // 2), jnp.float32),
                        pltpu.VMEM((2, 2, nb * _row_pitch(rows_l // nb), sw // 2), jnp.float32),
                        pltpu.VMEM((2, 2, nb * _row_pitch(rows_l // nb), sw // 2), jnp.float32)],
        compiler_params=pltpu.CompilerParams(dimension_semantics=("arbitrary",),
                                             vmem_limit_bytes=VMEM_LIMIT_BYTES),
        name="s5",
    )(xl, xc, toep, bs, cs, a_pow, dvec, *to_cast)


def _na_row_plan():
    last_q = NA_KROWS - NA_QROWS
    plan = []
    for r0s, qrs in ((lambda lr: 0, lambda lr: lr),
                     (lambda lr: lr, lambda lr: NA_KR // 2 + lr),
                     (lambda lr: last_q - NA_KR // 2, lambda lr: last_q + lr)):
        plan.append([[kl - qrs(lr) + (NA_KR - 1) if r0s(lr) <= kl < r0s(lr) + NA_KR else None
                      for kl in range(NA_KROWS)] for lr in range(NA_QROWS)])
    return plan


def _na_bias_fill(col_ref, bias_s):
    plan = _na_row_plan()
    lane = lax.broadcasted_iota(jnp.int32, (1, 2 * GRID_W), 1)
    masked = jnp.full((GRID_W, 2 * GRID_W), MASK_VALUE, jnp.float32)
    for head in range(bias_s.shape[1]):
        for kind in range(len(plan)):
            for kl in range(NA_KROWS):
                for lp in range(NA_QROWS // 2):
                    a0, a1 = plan[kind][2 * lp][kl], plan[kind][2 * lp + 1][kl]
                    t0 = masked if a0 is None else col_ref[head, a0]
                    t1 = masked if a1 is None else col_ref[head, a1]
                    tile = t0 if a0 == a1 else jnp.where(lane < GRID_W, t0, t1)
                    bias_s[kind, head, kl * GRID_W:(kl + 1) * GRID_W, lp * 2 * GRID_W:(lp + 1) * 2 * GRID_W] = tile


def _na_bias_columns(rpb):
    kc = np.arange(GRID_W)[:, None]
    qc = np.arange(GRID_W)[None, :]
    c0 = np.clip(qc - NA_KC // 2, 0, GRID_W - NA_KC)
    col_ok = (kc >= c0) & (kc < c0 + NA_KC)
    sel_c = ((kc - qc + (NA_KC - 1))[:, :, None] == np.arange(2 * NA_KC - 1)) & col_ok[:, :, None]
    by_col = jnp.einsum('hab,kqb->hakq', rpb.astype(jnp.float32), jnp.asarray(sel_c, jnp.float32), precision=_HI)
    by_col = jnp.where(jnp.asarray(col_ok), by_col * LOG2_E, MASK_VALUE)
    return jnp.concatenate([by_col, by_col], axis=3)


def _row_max(x):
    while x.shape[0] % (2 * SUBLANES) == 0:
        half = x.shape[0] // 2
        x = jnp.maximum(x[:half], x[half:])
    return jnp.max(x, axis=0, keepdims=True)


def _na_kernel(rows, q_ref, k_ref, vt_ref, kc_ref, vct_ref, col_ref, o_ref, bias_ref):
    @pl.when((pl.program_id(0) == 0) & (pl.program_id(1) == 0))
    def _():
        _na_bias_fill(col_ref, bias_ref)

    n_rb = rows // NA_QROWS
    nk = NA_KROWS * GRID_W
    pair = 2 * NA_DH
    lane_first = lax.broadcasted_iota(jnp.int32, (1, pair), 1) < NA_DH
    row_first = lax.broadcasted_iota(jnp.int32, (pair, 1), 0) < NA_DH
    n_heads = q_ref.shape[2] // NA_DH
    n_blocks = q_ref.shape[1] // NA_BLOCK

    def block(j):
        rb = pl.program_id(1) * n_blocks + j
        kind = jnp.where(rb == 0, 0, jnp.where(rb == n_rb - 1, 2, 1))
        first_row = jnp.clip(rb * NA_QROWS - NA_KR // 2, 0, rows - NA_KROWS)
        return kind, pl.multiple_of(first_row * GRID_W, GRID_W), first_row // NA_QROWS

    geometry = [block(j) for j in range(n_blocks)]

    def scores(u):
        j, h = divmod(u, n_heads)
        kind, start, _ = geometry[j]
        cols = slice(h // 2 * pair, (h // 2 + 1) * pair)
        lane_mine = lane_first if h % 2 == 0 else jnp.logical_not(lane_first)
        qm = jnp.where(lane_mine, q_ref[0, j * NA_BLOCK:(j + 1) * NA_BLOCK, cols],
                       jnp.zeros((NA_BLOCK, pair), q_ref.dtype))
        s_w = _mm_nt(k_ref[0, pl.ds(start, nk), cols], qm) + bias_ref[kind, h]
        s_c = _mm_nt(kc_ref[0, :, cols], qm)
        return s_w, s_c

    def weights(s_w, s_c):
        mx = jnp.maximum(_row_max(s_w), _row_max(s_c))
        return jnp.exp2(s_w - mx).astype(vt_ref.dtype), jnp.exp2(s_c - mx).astype(vt_ref.dtype)

    def values(u, p_w, p_c):
        j, h = divmod(u, n_heads)
        blk0 = geometry[j][2]
        cols = slice(h // 2 * pair, (h // 2 + 1) * pair)
        row_mine = row_first if h % 2 == 0 else jnp.logical_not(row_first)
        vtw = jnp.concatenate([vt_ref[blk0 + t, cols, :] for t in range(nk // NA_BLOCK)], axis=1)
        vtc = jnp.concatenate([vct_ref[t, cols, :] for t in range(vct_ref.shape[0])], axis=1)
        o = (_mm(jnp.where(row_mine, vtw, jnp.ones_like(vtw)), p_w)
             + _mm(jnp.where(row_mine, vtc, jnp.ones_like(vtc)), p_c))
        num, den = (o[:NA_DH], o[NA_DH:]) if h % 2 == 0 else (o[NA_DH:], o[:NA_DH])
        return num / den

    n_units = n_blocks * n_heads
    s = {0: scores(0), 1: scores(1)}
    p = {0: weights(*s.pop(0))}
    outs = []
    for u in range(n_units):
        if u + 2 < n_units:
            s[u + 2] = scores(u + 2)
        if u + 1 < n_units:
            p[u + 1] = weights(*s.pop(u + 1))
        outs.append(values(u, *p.pop(u)))
        if len(outs) == n_heads:
            j = u // n_heads
            o_ref[0, j * NA_BLOCK:(j + 1) * NA_BLOCK, :] = jnp.concatenate(outs, axis=0).T.astype(o_ref.dtype)
            outs = []


def _na(qk, vt, qk_c, vt_c, bias_cols, na_w):
    b, s, _ = qk.shape
    l = qk_c.shape[1]
    rows = s // GRID_W
    tq = NA_STEP_BLOCKS * NA_BLOCK
    n_kinds = len(_na_row_plan())
    return pl.pallas_call(
        functools.partial(_na_kernel, rows),
        grid=(b, s // tq),
        in_specs=[pl.BlockSpec((1, tq, na_w), lambda bi, rb: (bi, rb, 0)),
                  pl.BlockSpec((1, s, na_w), lambda bi, rb: (bi, 0, 1)),
                  pl.BlockSpec((s // NA_BLOCK, na_w, NA_BLOCK), lambda bi, rb: (bi, 0, 0)),
                  pl.BlockSpec((1, l, na_w), lambda bi, rb: (bi, 0, 1)),
                  pl.BlockSpec((l // NA_BLOCK, na_w, NA_BLOCK), lambda bi, rb: (bi, 0, 0)),
                  _const_spec(bias_cols.shape)],
        out_specs=pl.BlockSpec((1, tq, na_w), lambda bi, rb: (bi, rb, 0)),
        out_shape=jax.ShapeDtypeStruct((b, s, na_w), _MXU_DTYPE),
        scratch_shapes=[pltpu.VMEM((n_kinds, na_w // NA_DH, NA_KROWS * GRID_W, NA_BLOCK), jnp.float32)],
        compiler_params=pltpu.CompilerParams(
            dimension_semantics=("arbitrary", "arbitrary"),
            vmem_limit_bytes=VMEM_LIMIT_BYTES),
        name="na",
    )(qk, qk, vt, qk_c, vt_c, bias_cols)


def _out_ffn_kernel(alpha, x1_ref, yna_ref, ys5_ref, mod_a_ref, mod_b_ref, lng_ref, lnb_ref, wglu_ref, bglu_ref,
                    wout_ref, wup_ref, wdn_ref, o_ref, p_s, y_s, h_s, x2_s):
    @pl.when(pl.program_id(0) == 0)
    def _():
        h_s[...] = jnp.zeros_like(h_s)
        x2_s[...] = jnp.zeros_like(x2_s)

    mod_a = mod_a_ref[0]
    na_w = yna_ref.shape[1]
    stage_a = {}

    def relayout():
        _chunks_to_tokens(ys5_ref, y_s)

    def gated():
        gl = jax.nn.gelu(jnp.concatenate([y_s[v] for v in range(y_s.shape[0])], axis=1))
        gate = jax.nn.sigmoid(_mm(gl.astype(p_s.dtype), wglu_ref[...]) + bglu_ref[...])
        stage_a["glu"] = (gl * gate).astype(p_s.dtype)

    def mixed():
        y = _mm(yna_ref[...], wout_ref[:na_w, :]) + _mm(stage_a["glu"], wout_ref[na_w:, :])
        x2 = _layer_norm(alpha * x1_ref[...] + mod_a[5:6] * y, lng_ref[1:2], lnb_ref[1:2])
        stage_a["x2"] = x2
        stage_a["h"] = (x2 * (1.0 + mod_a[7:8]) + mod_a[6:7]).astype(h_s.dtype)

    mod_b = mod_b_ref[0]
    _swiglu_hidden(h_s[...], wup_ref, p_s, between=(relayout, gated, mixed))
    n_blk = o_ref.shape[0] // NA_BLOCK
    fs = [_mm(p_s[t * NA_BLOCK:(t + 1) * NA_BLOCK, :], wdn_ref[...]) for t in range(n_blk)]
    for t in range(n_blk):
        rows = slice(t * NA_BLOCK, (t + 1) * NA_BLOCK)
        o_ref[rows, :] = _layer_norm(alpha * x2_s[rows, :] + (0.5 * mod_b[8:9]) * fs[t], lng_ref[2:3], lnb_ref[2:3])

    x2_s[...] = stage_a["x2"]
    h_s[...] = stage_a["h"]


def _out_ffn(x1, yna, ys5, mod, tiles_per_mod, lng, lnb, wglu, bglu, wout, wup, wdn, alpha, tm):
    t, d = x1.shape
    ff = wdn.shape[0]
    na_w = yna.shape[1]
    groups, _, cw = ys5.shape
    s5_w = groups * S5_CG
    n = t // tm

    def cur(i):
        return jnp.minimum(i, n - 1)

    def prev(i):
        return jnp.maximum(i - 1, 0)

    return pl.pallas_call(
        functools.partial(_out_ffn_kernel, alpha),
        grid=(n + 1,),
        in_specs=[pl.BlockSpec((tm, d), lambda i: (cur(i), 0)),
                  pl.BlockSpec((tm, na_w), lambda i: (cur(i), 0)),
                  pl.BlockSpec((groups, tm // CHUNK, cw), lambda i: (0, cur(i), 0)),
                  pl.BlockSpec((1, N_MOD, d), lambda i: (cur(i) // tiles_per_mod, 0, 0)),
                  pl.BlockSpec((1, N_MOD, d), lambda i: (prev(i) // tiles_per_mod, 0, 0)),
                  _const_spec(lng.shape), _const_spec(lnb.shape),
                  _const_spec(wglu.shape), _const_spec(bglu.shape), _const_spec(wout.shape),
                  _const_spec(wup.shape), _const_spec(wdn.shape)],
        out_specs=pl.BlockSpec((tm, d), lambda i: (prev(i), 0)),
        out_shape=jax.ShapeDtypeStruct((t, d), jnp.float32),
        scratch_shapes=[pltpu.VMEM((tm, ff), _MXU_DTYPE), pltpu.VMEM((s5_w // LANES, tm, LANES), jnp.float32),
                        pltpu.VMEM((tm, d), _MXU_DTYPE), pltpu.VMEM((tm, d), jnp.float32)],
        compiler_params=pltpu.CompilerParams(dimension_semantics=("arbitrary",),
                                             vmem_limit_bytes=VMEM_LIMIT_BYTES),
        name="out_ffn",
    )(x1, yna, ys5, mod, mod, lng, lnb, wglu, bglu, wout, wup, wdn)


def _pick_tile(n, target):
    t = min(n, target)
    while n % t:
        t //= 2
    return t


def kernel(x, c, ctx, c_ctx, w_ada, b_ada, ln_g, ln_b, ffn1_w_up, ffn1_w_down, w_in, na_rpb, s5_a_re, s5_a_im, s5_log_dt, s5_b_re, s5_b_im, s5_c_re, s5_c_im, s5_d, s5_w_glu, s5_b_glu, w_out, ffn2_w_up, ffn2_w_down):
    depth = w_ada.shape[0]
    assert depth == 1, "single-layer stack only"
    bsz, seq, d = x.shape
    l_ctx = ctx.shape[1]
    s5_w = s5_w_glu.shape[1]
    na_w = w_out.shape[1] - s5_w
    groups = s5_w // S5_CG
    rows = seq // GRID_W
    assert bsz % 8 == 0 and seq % (NA_STEP_BLOCKS * NA_BLOCK) == 0 and rows >= NA_KROWS
    assert l_ctx % NA_BLOCK == 0 and groups % 2 == 0 and 2 * s5_a_re.shape[3] == LANES
    assert na_w % (2 * NA_DH) == 0 and ffn1_w_down.shape[1] % MXU_TILE == 0
    alpha = (2.0 * depth) ** 0.25
    cd = _MXU_DTYPE

    n_rows = -(-(bsz + 1) // 8) * 8
    cc = jnp.zeros((n_rows, d), jnp.float32).at[:bsz].set(c).at[bsz].set(c_ctx)
    n_ada = w_ada.shape[2]
    mod = _ada(cc, w_ada[0], b_ada[0][None, :], _pick_tile(n_ada, 1152)).reshape(n_rows, N_MOD, d)

    lng, lnb = ln_g[0], ln_b[0]
    wup1, wdn1 = ffn1_w_up[0].astype(cd), ffn1_w_down[0].astype(cd)
    win = w_in[0].astype(cd)

    tm = _pick_tile(seq, 512)
    x1, qk, vt, xs = _ffn_in(x.reshape(bsz * seq, d), mod, seq // tm, lng, lnb, wup1, wdn1, win, alpha, na_w, tm)
    tmc = _pick_tile(bsz * l_ctx, 512)
    _, qk_c, vt_c, xs_c = _ffn_in(ctx.reshape(bsz * l_ctx, d), mod[bsz:bsz + 1], None, lng, lnb, wup1, wdn1, win,
                                  alpha, na_w, tmc)

    tables = _s5_tables(s5_a_re[0], s5_a_im[0], s5_log_dt[0], s5_b_re[0], s5_b_im[0], s5_c_re[0], s5_c_im[0],
                        s5_d[0])
    y_s5, wup2, wdn2, wout, wglu = _s5(xs, xs_c, *tables, bsz,
                                       (ffn2_w_up[0], ffn2_w_down[0], w_out[0], s5_w_glu[0]))

    y_na = _na(qk.reshape(bsz, seq, 2 * na_w), vt, qk_c.reshape(bsz, l_ctx, 2 * na_w), vt_c, _na_bias_columns(na_rpb[0]),
               na_w).reshape(bsz * seq, na_w)

    out = _out_ffn(x1, y_na, y_s5, mod, seq // tm, lng, lnb, wglu, s5_b_glu[0][None, :], wout, wup2, wdn2,
                   alpha, tm)
    return out.reshape(bsz, seq, d)
```

```python
import functools

import jax
import jax.numpy as jnp
import numpy as np
from jax import lax
from jax.experimental import pallas as pl
from jax.experimental.pallas import tpu as pltpu

GRID_W = 64
NA_DH = 64
NA_KR = 8
NA_KC = 16
S5_CG = 16
S5_P = 64
LN_EPS = 1e-6
N_MOD = 9

CHUNK = 16
NA_QROWS = 4
NA_KROWS = 12
NA_BLOCK = NA_QROWS * GRID_W
NA_STEP_BLOCKS = 8
MASK_VALUE = -1e30
LOG2_E = 1.4426950408889634
SCAN_UNROLL = 4

MXU_TILE = 256
LANES = 128
SUBLANES = 8
VMEM_LIMIT_BYTES = 56 * 1024 * 1024

_MXU_DTYPE = jnp.bfloat16
_HI = lax.Precision.HIGHEST


def _mm(a, b):
    return jnp.dot(a, b, preferred_element_type=jnp.float32)


def _mm_nt(a, b):
    return lax.dot_general(a, b, (((1,), (1,)), ((), ())), preferred_element_type=jnp.float32)


def _const_spec(shape):
    nd = len(shape)
    return pl.BlockSpec(shape, lambda *_: (0,) * nd, pipeline_mode=pl.Buffered(1))


def _layer_norm(r, g, b):
    mu = jnp.mean(r, axis=-1, keepdims=True)
    d = r - mu
    var = jnp.mean(d * d, axis=-1, keepdims=True)
    return d * lax.rsqrt(var + LN_EPS) * g + b


def _swiglu_hidden(h, wup_ref, p_s, between=()):
    ff = p_s.shape[1]
    n_chunks = ff // MXU_TILE
    slots = [(k + 1) * n_chunks // (len(between) + 1) for k in range(len(between))]
    for j in range(n_chunks):
        for slot, fn in zip(slots, between):
            if slot == j:
                fn()
        lo = j * MXU_TILE
        a = _mm(h, wup_ref[:, lo:lo + MXU_TILE])
        g = _mm(h, wup_ref[:, ff + lo:ff + lo + MXU_TILE])
        p_s[:, lo:lo + MXU_TILE] = (g * jax.nn.sigmoid(g) * a).astype(p_s.dtype)


def _swiglu(h, wup_ref, wdn_ref, p_s, between=()):
    _swiglu_hidden(h, wup_ref, p_s, between)
    return _mm(p_s[...], wdn_ref[...])


def _block_transpose(tiles):
    per = len(tiles)
    blk = lax.broadcasted_iota(jnp.int32, (1, LANES), 1) // S5_CG
    tiles = list(tiles)
    d = per // 2
    while d:
        upper = (blk & d) != 0
        for a in range(per):
            if a & d:
                continue
            lo, hi = tiles[a], tiles[a + d]
            tiles[a] = jnp.where(upper, pltpu.roll(hi, S5_CG * d, axis=1), lo)
            tiles[a + d] = jnp.where(upper, hi, pltpu.roll(lo, LANES - S5_CG * d, axis=1))
        d //= 2
    return tiles


def _tokens_to_chunks(u_s, xs_ref, row0, n_rows):
    n_ch = n_rows // CHUNK
    ch0 = row0 // CHUNK
    per = LANES // S5_CG
    for v in range(u_s.shape[0]):
        for w in range(CHUNK // per):
            by_group = _block_transpose(
                [u_s[v, pl.ds(row0 + per * w + jj, n_ch, stride=CHUNK), :] for jj in range(per)])
            for gg in range(per):
                xs_ref[per * v + gg, ch0:ch0 + n_ch, w * LANES:(w + 1) * LANES] = by_group[gg].astype(xs_ref.dtype)


def _chunks_to_tokens(ys_ref, y_s):
    n_ch = y_s.shape[1] // CHUNK
    per = LANES // S5_CG
    for v in range(y_s.shape[0]):
        for w in range(CHUNK // per):
            by_phase = _block_transpose([ys_ref[per * v + gg, :, w * LANES:(w + 1) * LANES] for gg in range(per)])
            for jj in range(per):
                y_s[v, pl.ds(per * w + jj, n_ch, stride=CHUNK), :] = by_phase[jj]


def _ada_kernel(c_ref, w_ref, b_ref, o_ref):
    cc = c_ref[...]
    s = cc * jax.nn.sigmoid(cc)
    w = w_ref[...]
    s_hi = s.astype(_MXU_DTYPE)
    s_lo = (s - s_hi.astype(jnp.float32)).astype(_MXU_DTYPE)
    w_hi = w.astype(_MXU_DTYPE)
    w_lo = (w - w_hi.astype(jnp.float32)).astype(_MXU_DTYPE)
    o_ref[...] = _mm(s_hi, w_hi) + (_mm(s_lo, w_hi) + _mm(s_hi, w_lo)) + b_ref[...]


def _ada(cc, w, b, bn):
    rows, d = cc.shape
    n = w.shape[1]
    return pl.pallas_call(
        _ada_kernel,
        grid=(n // bn,),
        in_specs=[pl.BlockSpec((rows, d), lambda j: (0, 0)),
                  pl.BlockSpec((d, bn), lambda j: (0, j)),
                  pl.BlockSpec((1, bn), lambda j: (0, j))],
        out_specs=pl.BlockSpec((rows, bn), lambda j: (0, j)),
        out_shape=jax.ShapeDtypeStruct((rows, n), jnp.float32),
        compiler_params=pltpu.CompilerParams(dimension_semantics=("arbitrary",),
                                             vmem_limit_bytes=VMEM_LIMIT_BYTES),
        name="ada",
    )(cc, w, b)


def _ffn_in_kernel(alpha, na_w, x_ref, mod_ref, lng_ref, lnb_ref, wup_ref, wdn_ref, win_ref,
                   x1_ref, qk_ref, vt_ref, xs_ref, p_s, u_s):
    mod = mod_ref[0]
    h = (x_ref[...] * (1.0 + mod[1:2]) + mod[0:1]).astype(p_s.dtype)
    _swiglu_hidden(h, wup_ref, p_s)
    n_blk = vt_ref.shape[0]
    ys = [_mm(p_s[t * NA_BLOCK:(t + 1) * NA_BLOCK, :], wdn_ref[...]) for t in range(n_blk)]
    for t in range(n_blk):
        rows = slice(t * NA_BLOCK, (t + 1) * NA_BLOCK)
        x1 = _layer_norm(alpha * x_ref[rows, :] + (0.5 * mod[2:3]) * ys[t], lng_ref[0:1], lnb_ref[0:1])
        x1_ref[rows, :] = x1
        h2 = (x1 * (1.0 + mod[4:5]) + mod[3:4]).astype(p_s.dtype)
        pu = _mm(h2, win_ref[:, 3 * na_w:])
        for v in range(u_s.shape[0]):
            u_s[v, rows, :] = pu[:, v * LANES:(v + 1) * LANES]
        _tokens_to_chunks(u_s, xs_ref, t * NA_BLOCK, NA_BLOCK)
        pr = _mm(h2, win_ref[:, :3 * na_w])
        qk_ref[rows, :na_w] = (pr[:, :na_w] * (NA_DH ** -0.5 * LOG2_E)).astype(qk_ref.dtype)
        qk_ref[rows, na_w:] = pr[:, na_w:2 * na_w].astype(qk_ref.dtype)
        vt_ref[t] = pr[:, 2 * na_w:].T.astype(vt_ref.dtype)


def _ffn_in(x2d, mod, tiles_per_mod, lng, lnb, wup, wdn, win, alpha, na_w, tm):
    t, d = x2d.shape
    ff = wdn.shape[0]
    ncol = win.shape[1]
    s5_w = ncol - 3 * na_w
    groups, cw = s5_w // S5_CG, CHUNK * S5_CG
    if tiles_per_mod is None:
        mod_map = lambda i: (0, 0, 0)
    else:
        mod_map = lambda i: (i // tiles_per_mod, 0, 0)
    return pl.pallas_call(
        functools.partial(_ffn_in_kernel, alpha, na_w),
        grid=(t // tm,),
        in_specs=[pl.BlockSpec((tm, d), lambda i: (i, 0)),
                  pl.BlockSpec((1, N_MOD, d), mod_map),
                  _const_spec(lng.shape), _const_spec(lnb.shape),
                  _const_spec(wup.shape), _const_spec(wdn.shape), _const_spec(win.shape)],
        out_specs=[pl.BlockSpec((tm, d), lambda i: (i, 0)),
                   pl.BlockSpec((tm, 2 * na_w), lambda i: (i, 0)),
                   pl.BlockSpec((tm // NA_BLOCK, na_w, NA_BLOCK), lambda i: (i, 0, 0)),
                   pl.BlockSpec((groups, tm // CHUNK, cw), lambda i: (0, i, 0))],
        out_shape=[jax.ShapeDtypeStruct((t, d), jnp.float32),
                   jax.ShapeDtypeStruct((t, 2 * na_w), _MXU_DTYPE),
                   jax.ShapeDtypeStruct((t // NA_BLOCK, na_w, NA_BLOCK), _MXU_DTYPE),
                   jax.ShapeDtypeStruct((groups, t // CHUNK, cw), jnp.float32)],
        scratch_shapes=[pltpu.VMEM((tm, ff), _MXU_DTYPE), pltpu.VMEM((s5_w // LANES, tm, LANES), jnp.float32)],
        compiler_params=pltpu.CompilerParams(dimension_semantics=("arbitrary",),
                                             vmem_limit_bytes=VMEM_LIMIT_BYTES),
        name="ffn_in",
    )(x2d, mod, lng, lnb, wup, wdn, win)


def _toeplitz_kernel(lhs_ref, rhs_ref, o_ref):
    width = o_ref.shape[2]
    lane = lax.broadcasted_iota(jnp.int32, (1, width), 1)
    for g in range(o_ref.shape[0]):
        k_f = jnp.dot(lhs_ref[0, g], rhs_ref[0, g], precision=_HI, preferred_element_type=jnp.float32)
        k_b = jnp.dot(lhs_ref[1, g], rhs_ref[1, g], precision=_HI, preferred_element_type=jnp.float32)
        for i in range(CHUNK):
            lo, hi = i * S5_CG, (i + 1) * S5_CG
            fwd = k_f if lo == 0 else jnp.where(lane >= lo, pltpu.roll(k_f, lo, axis=1), 0.0)
            bwd = k_b if hi == width else jnp.where(lane < hi, pltpu.roll(k_b, hi, axis=1), 0.0)
            o_ref[g, lo:hi, :] = (fwd + bwd).astype(o_ref.dtype)


def _toeplitz(lhs, rhs):
    _, g, cg, kdim = lhs.shape
    width = rhs.shape[3]
    gb = _pick_tile(g, 8)
    return pl.pallas_call(
        _toeplitz_kernel,
        grid=(g // gb,),
        in_specs=[pl.BlockSpec((2, gb, cg, kdim), lambda k: (0, k, 0, 0)),
                  pl.BlockSpec((2, gb, kdim, width), lambda k: (0, k, 0, 0))],
        out_specs=pl.BlockSpec((gb, width, width), lambda k: (k, 0, 0)),
        out_shape=jax.ShapeDtypeStruct((g, width, width), _MXU_DTYPE),
        compiler_params=pltpu.CompilerParams(dimension_semantics=("arbitrary",)),
        name="s5_toeplitz",
    )(lhs, rhs)


def _s5_tables(a_re, a_im, log_dt, b_re, b_im, c_re, c_im, d_skip):
    g, p = a_re.shape[1], a_re.shape[2]
    f32 = jnp.float32
    a_re, a_im = a_re.astype(f32), a_im.astype(f32)
    dt = jnp.exp(log_dt.astype(f32))[..., None]
    lr, li = a_re * dt, a_im * dt

    width = CHUNK * S5_CG
    asc = np.arange(CHUNK)
    desc = asc[::-1].copy()
    lane_m, lane_c = np.arange(width) // S5_CG, np.arange(width) % S5_CG

    def powers(expo, lr_b, li_b):
        e = jnp.asarray(expo, f32)
        mag = jnp.exp(e * lr_b)
        return mag * jnp.cos(e * li_b), mag * jnp.sin(e * li_b)


    ab_r, ab_i = jnp.exp(lr) * jnp.cos(li), jnp.exp(lr) * jnp.sin(li)
    den = a_re ** 2 + a_im ** 2
    nr, ni = ab_r - 1.0, ab_i
    sr = ((nr * a_re + ni * a_im) / den)[:, :, None, :]
    si = ((ni * a_re - nr * a_im) / den)[:, :, None, :]
    bt_re, bt_im = jnp.swapaxes(b_re, 2, 3).astype(f32), jnp.swapaxes(b_im, 2, 3).astype(f32)
    bt_r = sr * bt_re - si * bt_im
    bt_i = sr * bt_im + si * bt_re

    tile_c = jnp.asarray(lane_c[None, :] == np.arange(S5_CG)[:, None], f32)
    ct_r = jnp.einsum('dgcp,cl->dgpl', c_re.astype(f32), tile_c, precision=_HI)
    ct_i = jnp.einsum('dgcp,cl->dgpl', c_im.astype(f32), tile_c, precision=_HI)

    rep_m = jnp.asarray(lane_m[None, :] == np.arange(CHUNK)[:, None], f32)

    def by_lane(expo):
        pw = powers(np.stack(expo)[:, None, None, :], lr[..., None], li[..., None])
        return tuple(jnp.einsum('dgpm,ml->dgpl', v, rep_m, precision=_HI) for v in pw)

    q_r, q_i = by_lane([asc, desc])
    rhs = jnp.concatenate([ct_r * q_r - ct_i * q_i, ct_r * q_i + ct_i * q_r], axis=2)
    toep = _toeplitz(jnp.concatenate([bt_r, -bt_i], axis=3), rhs)

    v_r, v_i = by_lane([asc + 1, desc + 1])
    cs_r = ct_r * v_r - ct_i * v_i
    cs_i = -(ct_r * v_i + ct_i * v_r)

    w_r, w_i = (jnp.repeat(v, S5_CG, axis=2) for v in
                powers(np.stack([desc, asc])[:, None, :, None], lr[:, :, None, :], li[:, :, None, :]))
    rows_r, rows_i = jnp.tile(bt_r, (1, 1, CHUNK, 1)), jnp.tile(bt_i, (1, 1, CHUNK, 1))
    bs_r = w_r * rows_r - w_i * rows_i
    bs_i = w_r * rows_i + w_i * rows_r
    even = jnp.asarray((np.arange(g) % 2) == 0)[None, :, None, None]
    bs = jnp.concatenate([jnp.where(even, bs_r, 0.0), jnp.where(even, 0.0, bs_r),
                          jnp.where(even, bs_i, 0.0), jnp.where(even, 0.0, bs_i)], axis=3)
    cs = jnp.concatenate([jnp.where(even, cs_r, 0.0), jnp.where(even, 0.0, cs_r),
                          jnp.where(even, cs_i, 0.0), jnp.where(even, 0.0, cs_i)], axis=2)

    e_r, e_i = jnp.exp(CHUNK * lr), CHUNK * li
    half = 2 * p
    a_pow = jnp.stack([(e_r * jnp.cos(e_i)).reshape(2, g // 2, half),
                       (e_r * jnp.sin(e_i)).reshape(2, g // 2, half)], axis=2)
    dvec = jnp.dot(d_skip.astype(f32), tile_c, precision=_HI)[:, None, :]
    return toep, bs.astype(_MXU_DTYPE), cs.astype(_MXU_DTYPE), a_pow, dvec


def _row_pitch(n):
    return n + (4 - n) % 8


def _s5_kernel(nb, n_cast, xl_ref, xc_ref, toep_ref, bs_ref, cs_ref, apow_ref, dvec_ref, *refs):
    cast_in, (y_ref, *cast_out), (zc_s, zl_s, h_s) = refs[:n_cast], refs[n_cast:2 * n_cast + 1], refs[2 * n_cast + 1:]
    for src, dst in zip(cast_in, cast_out):
        dst[...] = src[...].astype(dst.dtype)
    rows_l = xl_ref.shape[1]
    rows_c = xc_ref.shape[1]
    n_l, n_c = rows_l // nb, rows_c // nb
    pitch_l, pitch_c = _row_pitch(n_l), _row_pitch(n_c)
    half = zl_s.shape[3]
    xl = [xl_ref[g] for g in range(2)]
    xlb = [v.astype(_MXU_DTYPE) for v in xl]
    xcb = [xc_ref[g].astype(_MXU_DTYPE) for g in range(2)]

    for d in range(2):
        zc = _mm(xcb[0], bs_ref[d, 0]) + _mm(xcb[1], bs_ref[d, 1])
        zl = _mm(xlb[0], bs_ref[d, 0]) + _mm(xlb[1], bs_ref[d, 1])
        for ri in range(2):
            for b in range(nb):
                zc_s[d, ri, b * pitch_c:b * pitch_c + n_c, :] = zc[b * n_c:(b + 1) * n_c, ri * half:(ri + 1) * half]
                zl_s[d, ri, b * pitch_l:b * pitch_l + n_l, :] = zl[b * n_l:(b + 1) * n_l, ri * half:(ri + 1) * half]

    a_r = [jnp.broadcast_to(apow_ref[d, 0, 0:1, :], (nb, half)) for d in range(2)]
    a_i = [jnp.broadcast_to(apow_ref[d, 0, 1:2, :], (nb, half)) for d in range(2)]

    def step(d, n, latent, h_r, h_i):
        if latent:
            h_s[d, 0, pl.ds(n, nb, stride=pitch_l), :] = h_r
            h_s[d, 1, pl.ds(n, nb, stride=pitch_l), :] = h_i
            z_r = zl_s[d, 0, pl.ds(n, nb, stride=pitch_l), :]
            z_i = zl_s[d, 1, pl.ds(n, nb, stride=pitch_l), :]
        else:
            z_r = zc_s[d, 0, pl.ds(n, nb, stride=pitch_c), :]
            z_i = zc_s[d, 1, pl.ds(n, nb, stride=pitch_c), :]
        return (a_r[d] * h_r - a_i[d] * h_i + z_r,
                a_r[d] * h_i + a_i[d] * h_r + z_i)

    def ctx_body(n, carry):
        f_r, f_i, b_r, b_i = carry
        f_r, f_i = step(0, n, False, f_r, f_i)
        b_r, b_i = step(1, n_c - 1 - n, False, b_r, b_i)
        return f_r, f_i, b_r, b_i

    def lat_body(n, carry):
        f_r, f_i, b_r, b_i = carry
        f_r, f_i = step(0, n, True, f_r, f_i)
        b_r, b_i = step(1, n_l - 1 - n, True, b_r, b_i)
        return f_r, f_i, b_r, b_i

    zero = jnp.zeros((nb, half), jnp.float32)
    carry = lax.fori_loop(0, n_c, ctx_body, (zero, zero, zero, zero), unroll=SCAN_UNROLL)
    lax.fori_loop(0, n_l, lat_body, carry, unroll=SCAN_UNROLL)

    def plane(d, ri):
        return jnp.concatenate([h_s[d, ri, b * pitch_l:b * pitch_l + n_l, :] for b in range(nb)], axis=0)

    hb = [jnp.concatenate([plane(d, 0), plane(d, 1)], axis=1).astype(_MXU_DTYPE) for d in range(2)]
    for g in range(2):
        y = _mm(xlb[g], toep_ref[g]) + _mm(hb[0], cs_ref[0, g]) + _mm(hb[1], cs_ref[1, g])
        y_ref[g] = y + xl[g] * dvec_ref[g]


def _s5(xl, xc, toep, bs, cs, a_pow, dvec, nb, to_cast):
    g, rows_l, w = xl.shape
    rows_c = xc.shape[1]
    sw = bs.shape[3]
    steps = g // 2
    slabs = [m.shape[0] // steps for m in to_cast]
    assert all(m.shape[0] == s * steps and s % (2 * SUBLANES) == 0 for m, s in zip(to_cast, slabs))
    cast_specs = [pl.BlockSpec((s, m.shape[1]), lambda k: (k, 0)) for m, s in zip(to_cast, slabs)]
    return pl.pallas_call(
        functools.partial(_s5_kernel, nb, len(to_cast)),
        grid=(steps,),
        in_specs=[pl.BlockSpec((2, rows_l, w), lambda k: (k, 0, 0)),
                  pl.BlockSpec((2, rows_c, w), lambda k: (k, 0, 0)),
                  pl.BlockSpec((2, w, w), lambda k: (k, 0, 0)),
                  pl.BlockSpec((2, 2, w, sw), lambda k: (0, k, 0, 0)),
                  pl.BlockSpec((2, 2, sw, w), lambda k: (0, k, 0, 0)),
                  pl.BlockSpec((2, 1, 2, sw // 2), lambda k: (0, k, 0, 0)),
                  pl.BlockSpec((2, 1, w), lambda k: (k, 0, 0))] + cast_specs,
        out_specs=[pl.BlockSpec((2, rows_l, w), lambda k: (k, 0, 0))] + cast_specs,
        out_shape=[jax.ShapeDtypeStruct((g, rows_l, w), jnp.float32)]
        + [jax.ShapeDtypeStruct(m.shape, _MXU_DTYPE) for m in to_cast],
        scratch_shapes=[pltpu.VMEM((2, 2, nb * _row_pitch(rows_c // nb), sw // 2), jnp.float32),
                        pltpu.VMEM((2, 2, nb * _row_pitch(rows_l // nb), sw // 2), jnp.float32),
                        pltpu.VMEM((2, 2, nb * _row_pitch(rows_l // nb), sw // 2), jnp.float32)],
        compiler_params=pltpu.CompilerParams(dimension_semantics=("arbitrary",),
                                             vmem_limit_bytes=VMEM_LIMIT_BYTES),
        name="s5",
    )(xl, xc, toep, bs, cs, a_pow, dvec, *to_cast)


def _na_row_plan():
    last_q = NA_KROWS - NA_QROWS
    plan = []
    for r0s, qrs in ((lambda lr: 0, lambda lr: lr),
                     (lambda lr: lr, lambda lr: NA_KR // 2 + lr),
                     (lambda lr: last_q - NA_KR // 2, lambda lr: last_q + lr)):
        plan.append([[kl - qrs(lr) + (NA_KR - 1) if r0s(lr) <= kl < r0s(lr) + NA_KR else None
                      for kl in range(NA_KROWS)] for lr in range(NA_QROWS)])
    return plan


def _na_bias_fill(col_ref, bias_s):
    plan = _na_row_plan()
    lane = lax.broadcasted_iota(jnp.int32, (1, 2 * GRID_W), 1)
    masked = jnp.full((GRID_W, 2 * GRID_W), MASK_VALUE, jnp.float32)
    for head in range(bias_s.shape[1]):
        for kind in range(len(plan)):
            for kl in range(NA_KROWS):
                for lp in range(NA_QROWS // 2):
                    a0, a1 = plan[kind][2 * lp][kl], plan[kind][2 * lp + 1][kl]
                    t0 = masked if a0 is None else col_ref[head, a0]
                    t1 = masked if a1 is None else col_ref[head, a1]
                    tile = t0 if a0 == a1 else jnp.where(lane < GRID_W, t0, t1)
                    bias_s[kind, head, kl * GRID_W:(kl + 1) * GRID_W, lp * 2 * GRID_W:(lp + 1) * 2 * GRID_W] = tile


def _na_bias_columns(rpb):
    kc = np.arange(GRID_W)[:, None]
    qc = np.arange(GRID_W)[None, :]
    c0 = np.clip(qc - NA_KC // 2, 0, GRID_W - NA_KC)
    col_ok = (kc >= c0) & (kc < c0 + NA_KC)
    sel_c = ((kc - qc + (NA_KC - 1))[:, :, None] == np.arange(2 * NA_KC - 1)) & col_ok[:, :, None]
    by_col = jnp.einsum('hab,kqb->hakq', rpb.astype(jnp.float32), jnp.asarray(sel_c, jnp.float32), precision=_HI)
    by_col = jnp.where(jnp.asarray(col_ok), by_col * LOG2_E, MASK_VALUE)
    return jnp.concatenate([by_col, by_col], axis=3)


def _row_max(x):
    while x.shape[0] % (2 * SUBLANES) == 0:
        half = x.shape[0] // 2
        x = jnp.maximum(x[:half], x[half:])
    return jnp.max(x, axis=0, keepdims=True)


def _na_kernel(rows, q_ref, k_ref, vt_ref, kc_ref, vct_ref, col_ref, o_ref, bias_ref):
    @pl.when((pl.program_id(0) == 0) & (pl.program_id(1) == 0))
    def _():
        _na_bias_fill(col_ref, bias_ref)

    n_rb = rows // NA_QROWS
    nk = NA_KROWS * GRID_W
    pair = 2 * NA_DH
    lane_first = lax.broadcasted_iota(jnp.int32, (1, pair), 1) < NA_DH
    row_first = lax.broadcasted_iota(jnp.int32, (pair, 1), 0) < NA_DH
    n_heads = q_ref.shape[2] // NA_DH
    n_blocks = q_ref.shape[1] // NA_BLOCK

    def block(j):
        rb = pl.program_id(1) * n_blocks + j
        kind = jnp.where(rb == 0, 0, jnp.where(rb == n_rb - 1, 2, 1))
        first_row = jnp.clip(rb * NA_QROWS - NA_KR // 2, 0, rows - NA_KROWS)
        return kind, pl.multiple_of(first_row * GRID_W, GRID_W), first_row // NA_QROWS

    geometry = [block(j) for j in range(n_blocks)]

    def scores(u):
        j, h = divmod(u, n_heads)
        kind, start, _ = geometry[j]
        cols = slice(h // 2 * pair, (h // 2 + 1) * pair)
        lane_mine = lane_first if h % 2 == 0 else jnp.logical_not(lane_first)
        qm = jnp.where(lane_mine, q_ref[0, j * NA_BLOCK:(j + 1) * NA_BLOCK, cols],
                       jnp.zeros((NA_BLOCK, pair), q_ref.dtype))
        s_w = _mm_nt(k_ref[0, pl.ds(start, nk), cols], qm) + bias_ref[kind, h]
        s_c = _mm_nt(kc_ref[0, :, cols], qm)
        return s_w, s_c

    def weights(s_w, s_c):
        mx = jnp.maximum(_row_max(s_w), _row_max(s_c))
        return jnp.exp2(s_w - mx).astype(vt_ref.dtype), jnp.exp2(s_c - mx).astype(vt_ref.dtype)

    def values(u, p_w, p_c):
        j, h = divmod(u, n_heads)
        blk0 = geometry[j][2]
        cols = slice(h // 2 * pair, (h // 2 + 1) * pair)
        row_mine = row_first if h % 2 == 0 else jnp.logical_not(row_first)
        vtw = jnp.concatenate([vt_ref[blk0 + t, cols, :] for t in range(nk // NA_BLOCK)], axis=1)
        vtc = jnp.concatenate([vct_ref[t, cols, :] for t in range(vct_ref.shape[0])], axis=1)
        o = (_mm(jnp.where(row_mine, vtw, jnp.ones_like(vtw)), p_w)
             + _mm(jnp.where(row_mine, vtc, jnp.ones_like(vtc)), p_c))
        num, den = (o[:NA_DH], o[NA_DH:]) if h % 2 == 0 else (o[NA_DH:], o[:NA_DH])
        return num / den

    n_units = n_blocks * n_heads
    s = {0: scores(0), 1: scores(1)}
    p = {0: weights(*s.pop(0))}
    outs = []
    for u in range(n_units):
        if u + 2 < n_units:
            s[u + 2] = scores(u + 2)
        if u + 1 < n_units:
            p[u + 1] = weights(*s.pop(u + 1))
        outs.append(values(u, *p.pop(u)))
        if len(outs) == n_heads:
            j = u // n_heads
            o_ref[0, j * NA_BLOCK:(j + 1) * NA_BLOCK, :] = jnp.concatenate(outs, axis=0).T.astype(o_ref.dtype)
            outs = []


def _na(qk, vt, qk_c, vt_c, bias_cols, na_w):
    b, s, _ = qk.shape
    l = qk_c.shape[1]
    rows = s // GRID_W
    tq = NA_STEP_BLOCKS * NA_BLOCK
    n_kinds = len(_na_row_plan())
    return pl.pallas_call(
        functools.partial(_na_kernel, rows),
        grid=(b, s // tq),
        in_specs=[pl.BlockSpec((1, tq, na_w), lambda bi, rb: (bi, rb, 0)),
                  pl.BlockSpec((1, s, na_w), lambda bi, rb: (bi, 0, 1)),
                  pl.BlockSpec((s // NA_BLOCK, na_w, NA_BLOCK), lambda bi, rb: (bi, 0, 0)),
                  pl.BlockSpec((1, l, na_w), lambda bi, rb: (bi, 0, 1)),
                  pl.BlockSpec((l // NA_BLOCK, na_w, NA_BLOCK), lambda bi, rb: (bi, 0, 0)),
                  _const_spec(bias_cols.shape)],
        out_specs=pl.BlockSpec((1, tq, na_w), lambda bi, rb: (bi, rb, 0)),
        out_shape=jax.ShapeDtypeStruct((b, s, na_w), _MXU_DTYPE),
        scratch_shapes=[pltpu.VMEM((n_kinds, na_w // NA_DH, NA_KROWS * GRID_W, NA_BLOCK), jnp.float32)],
        compiler_params=pltpu.CompilerParams(
            dimension_semantics=("arbitrary", "arbitrary"),
            vmem_limit_bytes=VMEM_LIMIT_BYTES),
        name="na",
    )(qk, qk, vt, qk_c, vt_c, bias_cols)


def _out_ffn_kernel(alpha, x1_ref, yna_ref, ys5_ref, mod_a_ref, mod_b_ref, lng_ref, lnb_ref, wglu_ref, bglu_ref,
                    wout_ref, wup_ref, wdn_ref, o_ref, p_s, y_s, h_s, x2_s):
    @pl.when(pl.program_id(0) == 0)
    def _():
        h_s[...] = jnp.zeros_like(h_s)
        x2_s[...] = jnp.zeros_like(x2_s)

    mod_a = mod_a_ref[0]
    na_w = yna_ref.shape[1]
    stage_a = {}

    def relayout():
        _chunks_to_tokens(ys5_ref, y_s)

    def gated():
        gl = jax.nn.gelu(jnp.concatenate([y_s[v] for v in range(y_s.shape[0])], axis=1))
        gate = jax.nn.sigmoid(_mm(gl.astype(p_s.dtype), wglu_ref[...]) + bglu_ref[...])
        stage_a["glu"] = (gl * gate).astype(p_s.dtype)

    def mixed():
        y = _mm(yna_ref[...], wout_ref[:na_w, :]) + _mm(stage_a["glu"], wout_ref[na_w:, :])
        x2 = _layer_norm(alpha * x1_ref[...] + mod_a[5:6] * y, lng_ref[1:2], lnb_ref[1:2])
        stage_a["x2"] = x2
        stage_a["h"] = (x2 * (1.0 + mod_a[7:8]) + mod_a[6:7]).astype(h_s.dtype)

    mod_b = mod_b_ref[0]
    _swiglu_hidden(h_s[...], wup_ref, p_s, between=(relayout, gated, mixed))
    n_blk = o_ref.shape[0] // NA_BLOCK
    fs = [_mm(p_s[t * NA_BLOCK:(t + 1) * NA_BLOCK, :], wdn_ref[...]) for t in range(n_blk)]
    for t in range(n_blk):
        rows = slice(t * NA_BLOCK, (t + 1) * NA_BLOCK)
        o_ref[rows, :] = _layer_norm(alpha * x2_s[rows, :] + (0.5 * mod_b[8:9]) * fs[t], lng_ref[2:3], lnb_ref[2:3])

    x2_s[...] = stage_a["x2"]
    h_s[...] = stage_a["h"]


def _out_ffn(x1, yna, ys5, mod, tiles_per_mod, lng, lnb, wglu, bglu, wout, wup, wdn, alpha, tm):
    t, d = x1.shape
    ff = wdn.shape[0]
    na_w = yna.shape[1]
    groups, _, cw = ys5.shape
    s5_w = groups * S5_CG
    n = t // tm

    def cur(i):
        return jnp.minimum(i, n - 1)

    def prev(i):
        return jnp.maximum(i - 1, 0)

    return pl.pallas_call(
        functools.partial(_out_ffn_kernel, alpha),
        grid=(n + 1,),
        in_specs=[pl.BlockSpec((tm, d), lambda i: (cur(i), 0)),
                  pl.BlockSpec((tm, na_w), lambda i: (cur(i), 0)),
                  pl.BlockSpec((groups, tm // CHUNK, cw), lambda i: (0, cur(i), 0)),
                  pl.BlockSpec((1, N_MOD, d), lambda i: (cur(i) // tiles_per_mod, 0, 0)),
                  pl.BlockSpec((1, N_MOD, d), lambda i: (prev(i) // tiles_per_mod, 0, 0)),
                  _const_spec(lng.shape), _const_spec(lnb.shape),
                  _const_spec(wglu.shape), _const_spec(bglu.shape), _const_spec(wout.shape),
                  _const_spec(wup.shape), _const_spec(wdn.shape)],
        out_specs=pl.BlockSpec((tm, d), lambda i: (prev(i), 0)),
        out_shape=jax.ShapeDtypeStruct((t, d), jnp.float32),
        scratch_shapes=[pltpu.VMEM((tm, ff), _MXU_DTYPE), pltpu.VMEM((s5_w // LANES, tm, LANES), jnp.float32),
                        pltpu.VMEM((tm, d), _MXU_DTYPE), pltpu.VMEM((tm, d), jnp.float32)],
        compiler_params=pltpu.CompilerParams(dimension_semantics=("arbitrary",),
                                             vmem_limit_bytes=VMEM_LIMIT_BYTES),
        name="out_ffn",
    )(x1, yna, ys5, mod, mod, lng, lnb, wglu, bglu, wout, wup, wdn)


def _pick_tile(n, target):
    t = min(n, target)
    while n % t:
        t //= 2
    return t


def kernel(x, c, ctx, c_ctx, w_ada, b_ada, ln_g, ln_b, ffn1_w_up, ffn1_w_down, w_in, na_rpb, s5_a_re, s5_a_im, s5_log_dt, s5_b_re, s5_b_im, s5_c_re, s5_c_im, s5_d, s5_w_glu, s5_b_glu, w_out, ffn2_w_up, ffn2_w_down):
    depth = w_ada.shape[0]
    assert depth == 1, "single-layer stack only"
    bsz, seq, d = x.shape
    l_ctx = ctx.shape[1]
    s5_w = s5_w_glu.shape[1]
    na_w = w_out.shape[1] - s5_w
    groups = s5_w // S5_CG
    rows = seq // GRID_W
    assert bsz % 8 == 0 and seq % (NA_STEP_BLOCKS * NA_BLOCK) == 0 and rows >= NA_KROWS
    assert l_ctx % NA_BLOCK == 0 and groups % 2 == 0 and 2 * s5_a_re.shape[3] == LANES
    assert na_w % (2 * NA_DH) == 0 and ffn1_w_down.shape[1] % MXU_TILE == 0
    alpha = (2.0 * depth) ** 0.25
    cd = _MXU_DTYPE

    n_rows = -(-(bsz + 1) // 8) * 8
    cc = jnp.zeros((n_rows, d), jnp.float32).at[:bsz].set(c).at[bsz].set(c_ctx)
    n_ada = w_ada.shape[2]
    mod = _ada(cc, w_ada[0], b_ada[0][None, :], _pick_tile(n_ada, 1152)).reshape(n_rows, N_MOD, d)

    lng, lnb = ln_g[0], ln_b[0]
    wup1, wdn1 = ffn1_w_up[0].astype(cd), ffn1_w_down[0].astype(cd)
    win = w_in[0].astype(cd)

    tm = _pick_tile(seq, 512)
    x1, qk, vt, xs = _ffn_in(x.reshape(bsz * seq, d), mod, seq // tm, lng, lnb, wup1, wdn1, win, alpha, na_w, tm)
    tmc = _pick_tile(bsz * l_ctx, 512)
    _, qk_c, vt_c, xs_c = _ffn_in(ctx.reshape(bsz * l_ctx, d), mod[bsz:bsz + 1], None, lng, lnb, wup1, wdn1, win,
                                  alpha, na_w, tmc)

    tables = _s5_tables(s5_a_re[0], s5_a_im[0], s5_log_dt[0], s5_b_re[0], s5_b_im[0], s5_c_re[0], s5_c_im[0],
                        s5_d[0])
    y_s5, wup2, wdn2, wout, wglu = _s5(xs, xs_c, *tables, bsz,
                                       (ffn2_w_up[0], ffn2_w_down[0], w_out[0], s5_w_glu[0]))

    y_na = _na(qk.reshape(bsz, seq, 2 * na_w), vt, qk_c.reshape(bsz, l_ctx, 2 * na_w), vt_c, _na_bias_columns(na_rpb[0]),
               na_w).reshape(bsz * seq, na_w)

    out = _out_ffn(x1, y_na, y_s5, mod, seq // tm, lng, lnb, wglu, s5_b_glu[0][None, :], wout, wup2, wdn2,
                   alpha, tm)
    return out.reshape(bsz, seq, d)
```

```python
import functools

import jax
import jax.numpy as jnp
import numpy as np
from jax import lax
from jax.experimental import pallas as pl
from jax.experimental.pallas import tpu as pltpu

GRID_W = 64
NA_DH = 64
NA_KR = 8
NA_KC = 16
S5_CG = 16
S5_P = 64
LN_EPS = 1e-6
N_MOD = 9

CHUNK = 16
NA_QROWS = 4
NA_KROWS = 12
NA_BLOCK = NA_QROWS * GRID_W
NA_STEP_BLOCKS = 4
MASK_VALUE = -1e30
LOG2_E = 1.4426950408889634
SCAN_UNROLL = 4

MXU_TILE = 256
LANES = 128
SUBLANES = 8
VMEM_LIMIT_BYTES = 56 * 1024 * 1024

_MXU_DTYPE = jnp.bfloat16
_HI = lax.Precision.HIGHEST


def _mm(a, b):
    return jnp.dot(a, b, preferred_element_type=jnp.float32)


def _mm_nt(a, b):
    return lax.dot_general(a, b, (((1,), (1,)), ((), ())), preferred_element_type=jnp.float32)


def _const_spec(shape):
    nd = len(shape)
    return pl.BlockSpec(shape, lambda *_: (0,) * nd, pipeline_mode=pl.Buffered(1))


def _layer_norm(r, g, b):
    mu = jnp.mean(r, axis=-1, keepdims=True)
    d = r - mu
    var = jnp.mean(d * d, axis=-1, keepdims=True)
    return d * lax.rsqrt(var + LN_EPS) * g + b


def _swiglu_hidden(h, wup_ref, p_s, between=()):
    ff = p_s.shape[1]
    n_chunks = ff // MXU_TILE
    slots = [(k + 1) * n_chunks // (len(between) + 1) for k in range(len(between))]
    for j in range(n_chunks):
        for slot, fn in zip(slots, between):
            if slot == j:
                fn()
        lo = j * MXU_TILE
        a = _mm(h, wup_ref[:, lo:lo + MXU_TILE])
        g = _mm(h, wup_ref[:, ff + lo:ff + lo + MXU_TILE])
        p_s[:, lo:lo + MXU_TILE] = (g * jax.nn.sigmoid(g) * a).astype(p_s.dtype)


def _swiglu(h, wup_ref, wdn_ref, p_s, between=()):
    _swiglu_hidden(h, wup_ref, p_s, between)
    return _mm(p_s[...], wdn_ref[...])


def _block_transpose(tiles):
    per = len(tiles)
    blk = lax.broadcasted_iota(jnp.int32, (1, LANES), 1) // S5_CG
    tiles = list(tiles)
    d = per // 2
    while d:
        upper = (blk & d) != 0
        for a in range(per):
            if a & d:
                continue
            lo, hi = tiles[a], tiles[a + d]
            tiles[a] = jnp.where(upper, pltpu.roll(hi, S5_CG * d, axis=1), lo)
            tiles[a + d] = jnp.where(upper, hi, pltpu.roll(lo, LANES - S5_CG * d, axis=1))
        d //= 2
    return tiles


def _tokens_to_chunks(u_s, xs_ref, row0, n_rows):
    n_ch = n_rows // CHUNK
    ch0 = row0 // CHUNK
    per = LANES // S5_CG
    for v in range(u_s.shape[0]):
        for w in range(CHUNK // per):
            by_group = _block_transpose(
                [u_s[v, pl.ds(row0 + per * w + jj, n_ch, stride=CHUNK), :] for jj in range(per)])
            for gg in range(per):
                xs_ref[per * v + gg, ch0:ch0 + n_ch, w * LANES:(w + 1) * LANES] = by_group[gg].astype(xs_ref.dtype)


def _chunks_to_tokens(ys_ref, y_s):
    n_ch = y_s.shape[1] // CHUNK
    per = LANES // S5_CG
    for v in range(y_s.shape[0]):
        for w in range(CHUNK // per):
            by_phase = _block_transpose([ys_ref[per * v + gg, :, w * LANES:(w + 1) * LANES] for gg in range(per)])
            for jj in range(per):
                y_s[v, pl.ds(per * w + jj, n_ch, stride=CHUNK), :] = by_phase[jj]


def _ada_kernel(c_ref, w_ref, b_ref, o_ref):
    cc = c_ref[...]
    s = cc * jax.nn.sigmoid(cc)
    w = w_ref[...]
    s_hi = s.astype(_MXU_DTYPE)
    s_lo = (s - s_hi.astype(jnp.float32)).astype(_MXU_DTYPE)
    w_hi = w.astype(_MXU_DTYPE)
    w_lo = (w - w_hi.astype(jnp.float32)).astype(_MXU_DTYPE)
    o_ref[...] = _mm(s_hi, w_hi) + (_mm(s_lo, w_hi) + _mm(s_hi, w_lo)) + b_ref[...]


def _ada(cc, w, b, bn):
    rows, d = cc.shape
    n = w.shape[1]
    return pl.pallas_call(
        _ada_kernel,
        grid=(n // bn,),
        in_specs=[pl.BlockSpec((rows, d), lambda j: (0, 0)),
                  pl.BlockSpec((d, bn), lambda j: (0, j)),
                  pl.BlockSpec((1, bn), lambda j: (0, j))],
        out_specs=pl.BlockSpec((rows, bn), lambda j: (0, j)),
        out_shape=jax.ShapeDtypeStruct((rows, n), jnp.float32),
        compiler_params=pltpu.CompilerParams(dimension_semantics=("arbitrary",),
                                             vmem_limit_bytes=VMEM_LIMIT_BYTES),
        name="ada",
    )(cc, w, b)


def _ffn_in_kernel(alpha, na_w, x_ref, mod_ref, lng_ref, lnb_ref, wup_ref, wdn_ref, win_ref,
                   x1_ref, qk_ref, vt_ref, xs_ref, p_s, u_s):
    mod = mod_ref[0]
    h = (x_ref[...] * (1.0 + mod[1:2]) + mod[0:1]).astype(p_s.dtype)
    _swiglu_hidden(h, wup_ref, p_s)
    n_blk = vt_ref.shape[0]
    ys = [_mm(p_s[t * NA_BLOCK:(t + 1) * NA_BLOCK, :], wdn_ref[...]) for t in range(n_blk)]
    for t in range(n_blk):
        rows = slice(t * NA_BLOCK, (t + 1) * NA_BLOCK)
        x1 = _layer_norm(alpha * x_ref[rows, :] + (0.5 * mod[2:3]) * ys[t], lng_ref[0:1], lnb_ref[0:1])
        x1_ref[rows, :] = x1
        h2 = (x1 * (1.0 + mod[4:5]) + mod[3:4]).astype(p_s.dtype)
        pu = _mm(h2, win_ref[:, 3 * na_w:])
        for v in range(u_s.shape[0]):
            u_s[v, rows, :] = pu[:, v * LANES:(v + 1) * LANES]
        _tokens_to_chunks(u_s, xs_ref, t * NA_BLOCK, NA_BLOCK)
        pr = _mm(h2, win_ref[:, :3 * na_w])
        qk_ref[rows, :na_w] = (pr[:, :na_w] * (NA_DH ** -0.5 * LOG2_E)).astype(qk_ref.dtype)
        qk_ref[rows, na_w:] = pr[:, na_w:2 * na_w].astype(qk_ref.dtype)
        vt_ref[t] = pr[:, 2 * na_w:].T.astype(vt_ref.dtype)


def _ffn_in(x2d, mod, tiles_per_mod, lng, lnb, wup, wdn, win, alpha, na_w, tm):
    t, d = x2d.shape
    ff = wdn.shape[0]
    ncol = win.shape[1]
    s5_w = ncol - 3 * na_w
    groups, cw = s5_w // S5_CG, CHUNK * S5_CG
    if tiles_per_mod is None:
        mod_map = lambda i: (0, 0, 0)
    else:
        mod_map = lambda i: (i // tiles_per_mod, 0, 0)
    return pl.pallas_call(
        functools.partial(_ffn_in_kernel, alpha, na_w),
        grid=(t // tm,),
        in_specs=[pl.BlockSpec((tm, d), lambda i: (i, 0)),
                  pl.BlockSpec((1, N_MOD, d), mod_map),
                  _const_spec(lng.shape), _const_spec(lnb.shape),
                  _const_spec(wup.shape), _const_spec(wdn.shape), _const_spec(win.shape)],
        out_specs=[pl.BlockSpec((tm, d), lambda i: (i, 0)),
                   pl.BlockSpec((tm, 2 * na_w), lambda i: (i, 0)),
                   pl.BlockSpec((tm // NA_BLOCK, na_w, NA_BLOCK), lambda i: (i, 0, 0)),
                   pl.BlockSpec((groups, tm // CHUNK, cw), lambda i: (0, i, 0))],
        out_shape=[jax.ShapeDtypeStruct((t, d), jnp.float32),
                   jax.ShapeDtypeStruct((t, 2 * na_w), _MXU_DTYPE),
                   jax.ShapeDtypeStruct((t // NA_BLOCK, na_w, NA_BLOCK), _MXU_DTYPE),
                   jax.ShapeDtypeStruct((groups, t // CHUNK, cw), jnp.float32)],
        scratch_shapes=[pltpu.VMEM((tm, ff), _MXU_DTYPE), pltpu.VMEM((s5_w // LANES, tm, LANES), jnp.float32)],
        compiler_params=pltpu.CompilerParams(dimension_semantics=("arbitrary",),
                                             vmem_limit_bytes=VMEM_LIMIT_BYTES),
        name="ffn_in",
    )(x2d, mod, lng, lnb, wup, wdn, win)


def _toeplitz_kernel(lhs_ref, rhs_ref, o_ref):
    width = o_ref.shape[2]
    lane = lax.broadcasted_iota(jnp.int32, (1, width), 1)
    for g in range(o_ref.shape[0]):
        k_f = jnp.dot(lhs_ref[0, g], rhs_ref[0, g], precision=_HI, preferred_element_type=jnp.float32)
        k_b = jnp.dot(lhs_ref[1, g], rhs_ref[1, g], precision=_HI, preferred_element_type=jnp.float32)
        for i in range(CHUNK):
            lo, hi = i * S5_CG, (i + 1) * S5_CG
            fwd = k_f if lo == 0 else jnp.where(lane >= lo, pltpu.roll(k_f, lo, axis=1), 0.0)
            bwd = k_b if hi == width else jnp.where(lane < hi, pltpu.roll(k_b, hi, axis=1), 0.0)
            o_ref[g, lo:hi, :] = (fwd + bwd).astype(o_ref.dtype)


def _toeplitz(lhs, rhs):
    _, g, cg, kdim = lhs.shape
    width = rhs.shape[3]
    gb = _pick_tile(g, 8)
    return pl.pallas_call(
        _toeplitz_kernel,
        grid=(g // gb,),
        in_specs=[pl.BlockSpec((2, gb, cg, kdim), lambda k: (0, k, 0, 0)),
                  pl.BlockSpec((2, gb, kdim, width), lambda k: (0, k, 0, 0))],
        out_specs=pl.BlockSpec((gb, width, width), lambda k: (k, 0, 0)),
        out_shape=jax.ShapeDtypeStruct((g, width, width), _MXU_DTYPE),
        compiler_params=pltpu.CompilerParams(dimension_semantics=("arbitrary",)),
        name="s5_toeplitz",
    )(lhs, rhs)


def _s5_tables(a_re, a_im, log_dt, b_re, b_im, c_re, c_im, d_skip):
    g, p = a_re.shape[1], a_re.shape[2]
    f32 = jnp.float32
    a_re, a_im = a_re.astype(f32), a_im.astype(f32)
    dt = jnp.exp(log_dt.astype(f32))[..., None]
    lr, li = a_re * dt, a_im * dt

    width = CHUNK * S5_CG
    asc = np.arange(CHUNK)
    desc = asc[::-1].copy()
    lane_m, lane_c = np.arange(width) // S5_CG, np.arange(width) % S5_CG

    def powers(expo, lr_b, li_b):
        e = jnp.asarray(expo, f32)
        mag = jnp.exp(e * lr_b)
        return mag * jnp.cos(e * li_b), mag * jnp.sin(e * li_b)


    ab_r, ab_i = jnp.exp(lr) * jnp.cos(li), jnp.exp(lr) * jnp.sin(li)
    den = a_re ** 2 + a_im ** 2
    nr, ni = ab_r - 1.0, ab_i
    sr = ((nr * a_re + ni * a_im) / den)[:, :, None, :]
    si = ((ni * a_re - nr * a_im) / den)[:, :, None, :]
    bt_re, bt_im = jnp.swapaxes(b_re, 2, 3).astype(f32), jnp.swapaxes(b_im, 2, 3).astype(f32)
    bt_r = sr * bt_re - si * bt_im
    bt_i = sr * bt_im + si * bt_re

    tile_c = jnp.asarray(lane_c[None, :] == np.arange(S5_CG)[:, None], f32)
    ct_r, ct_i = jnp.einsum('tdgcp,cl->tdgpl', jnp.stack([c_re, c_im]).astype(f32), tile_c,
                            precision=lax.Precision.HIGH)

    rep_m = jnp.asarray(lane_m[None, :] == np.arange(CHUNK)[:, None], f32)
    expo = np.stack([np.stack([asc, desc]), np.stack([asc + 1, desc + 1])])
    pw = powers(expo[:, :, None, None, :], lr[None, ..., None], li[None, ..., None])
    (q_r, v_r), (q_i, v_i) = jnp.einsum('tvdgpm,ml->tvdgpl', jnp.stack(pw), rep_m, precision=lax.Precision.HIGH)

    rhs = jnp.concatenate([ct_r * q_r - ct_i * q_i, ct_r * q_i + ct_i * q_r], axis=2)
    toep = _toeplitz(jnp.concatenate([bt_r, -bt_i], axis=3), rhs)

    cs_r = ct_r * v_r - ct_i * v_i
    cs_i = -(ct_r * v_i + ct_i * v_r)

    w_r, w_i = (jnp.repeat(v, S5_CG, axis=2) for v in
                powers(np.stack([desc, asc])[:, None, :, None], lr[:, :, None, :], li[:, :, None, :]))
    rows_r, rows_i = jnp.tile(bt_r, (1, 1, CHUNK, 1)), jnp.tile(bt_i, (1, 1, CHUNK, 1))
    bs_r = w_r * rows_r - w_i * rows_i
    bs_i = w_r * rows_i + w_i * rows_r
    even = jnp.asarray((np.arange(g) % 2) == 0)[None, :, None, None]
    bs = jnp.concatenate([jnp.where(even, bs_r, 0.0), jnp.where(even, 0.0, bs_r),
                          jnp.where(even, bs_i, 0.0), jnp.where(even, 0.0, bs_i)], axis=3)
    cs = jnp.concatenate([jnp.where(even, cs_r, 0.0), jnp.where(even, 0.0, cs_r),
                          jnp.where(even, cs_i, 0.0), jnp.where(even, 0.0, cs_i)], axis=2)

    e_r, e_i = jnp.exp(CHUNK * lr), CHUNK * li
    half = 2 * p
    a_pow = jnp.stack([(e_r * jnp.cos(e_i)).reshape(2, g // 2, half),
                       (e_r * jnp.sin(e_i)).reshape(2, g // 2, half)], axis=2)
    dvec = jnp.dot(d_skip.astype(f32), tile_c, precision=_HI)[:, None, :]
    return toep, bs.astype(_MXU_DTYPE), cs.astype(_MXU_DTYPE), a_pow, dvec


def _row_pitch(n):
    return n + (4 - n) % 8


def _s5_kernel(nb, n_cast, xl_ref, xc_ref, toep_ref, bs_ref, cs_ref, apow_ref, dvec_ref, *refs):
    cast_in, (y_ref, *cast_out), (zc_s, zl_s, h_s) = refs[:n_cast], refs[n_cast:2 * n_cast + 1], refs[2 * n_cast + 1:]
    for src, dst in zip(cast_in, cast_out):
        dst[...] = src[...].astype(dst.dtype)
    rows_l = xl_ref.shape[1]
    rows_c = xc_ref.shape[1]
    n_l, n_c = rows_l // nb, rows_c // nb
    pitch_l, pitch_c = _row_pitch(n_l), _row_pitch(n_c)
    half = zl_s.shape[3]
    xl = [xl_ref[g] for g in range(2)]
    xlb = [v.astype(_MXU_DTYPE) for v in xl]
    xcb = [xc_ref[g].astype(_MXU_DTYPE) for g in range(2)]

    for d in range(2):
        zc = _mm(xcb[0], bs_ref[d, 0]) + _mm(xcb[1], bs_ref[d, 1])
        zl = _mm(xlb[0], bs_ref[d, 0]) + _mm(xlb[1], bs_ref[d, 1])
        for ri in range(2):
            for b in range(nb):
                zc_s[d, ri, b * pitch_c:b * pitch_c + n_c, :] = zc[b * n_c:(b + 1) * n_c, ri * half:(ri + 1) * half]
                zl_s[d, ri, b * pitch_l:b * pitch_l + n_l, :] = zl[b * n_l:(b + 1) * n_l, ri * half:(ri + 1) * half]

    a_r = [jnp.broadcast_to(apow_ref[d, 0, 0:1, :], (nb, half)) for d in range(2)]
    a_i = [jnp.broadcast_to(apow_ref[d, 0, 1:2, :], (nb, half)) for d in range(2)]

    def step(d, n, latent, h_r, h_i):
        if latent:
            h_s[d, 0, pl.ds(n, nb, stride=pitch_l), :] = h_r
            h_s[d, 1, pl.ds(n, nb, stride=pitch_l), :] = h_i
            z_r = zl_s[d, 0, pl.ds(n, nb, stride=pitch_l), :]
            z_i = zl_s[d, 1, pl.ds(n, nb, stride=pitch_l), :]
        else:
            z_r = zc_s[d, 0, pl.ds(n, nb, stride=pitch_c), :]
            z_i = zc_s[d, 1, pl.ds(n, nb, stride=pitch_c), :]
        return (a_r[d] * h_r - a_i[d] * h_i + z_r,
                a_r[d] * h_i + a_i[d] * h_r + z_i)

    def ctx_body(n, carry):
        f_r, f_i, b_r, b_i = carry
        f_r, f_i = step(0, n, False, f_r, f_i)
        b_r, b_i = step(1, n_c - 1 - n, False, b_r, b_i)
        return f_r, f_i, b_r, b_i

    def lat_body(n, carry):
        f_r, f_i, b_r, b_i = carry
        f_r, f_i = step(0, n, True, f_r, f_i)
        b_r, b_i = step(1, n_l - 1 - n, True, b_r, b_i)
        return f_r, f_i, b_r, b_i

    zero = jnp.zeros((nb, half), jnp.float32)
    carry = lax.fori_loop(0, n_c, ctx_body, (zero, zero, zero, zero), unroll=SCAN_UNROLL)
    lax.fori_loop(0, n_l, lat_body, carry, unroll=SCAN_UNROLL)

    def plane(d, ri):
        return jnp.concatenate([h_s[d, ri, b * pitch_l:b * pitch_l + n_l, :] for b in range(nb)], axis=0)

    hb = [jnp.concatenate([plane(d, 0), plane(d, 1)], axis=1).astype(_MXU_DTYPE) for d in range(2)]
    for g in range(2):
        y = _mm(xlb[g], toep_ref[g]) + _mm(hb[0], cs_ref[0, g]) + _mm(hb[1], cs_ref[1, g])
        y_ref[g] = y + xl[g] * dvec_ref[g]


def _s5(xl, xc, toep, bs, cs, a_pow, dvec, nb, to_cast):
    g, rows_l, w = xl.shape
    rows_c = xc.shape[1]
    sw = bs.shape[3]
    steps = g // 2
    slabs = [m.shape[0] // steps for m in to_cast]
    assert all(m.shape[0] == s * steps and s % (2 * SUBLANES) == 0 for m, s in zip(to_cast, slabs))
    cast_specs = [pl.BlockSpec((s, m.shape[1]), lambda k: (k, 0)) for m, s in zip(to_cast, slabs)]
    return pl.pallas_call(
        functools.partial(_s5_kernel, nb, len(to_cast)),
        grid=(steps,),
        in_specs=[pl.BlockSpec((2, rows_l, w), lambda k: (k, 0, 0)),
                  pl.BlockSpec((2, rows_c, w), lambda k: (k, 0, 0)),
                  pl.BlockSpec((2, w, w), lambda k: (k, 0, 0)),
                  pl.BlockSpec((2, 2, w, sw), lambda k: (0, k, 0, 0)),
                  pl.BlockSpec((2, 2, sw, w), lambda k: (0, k, 0, 0)),
                  pl.BlockSpec((2, 1, 2, sw // 2), lambda k: (0, k, 0, 0)),
                  pl.BlockSpec((2, 1, w), lambda k: (k, 0, 0))] + cast_specs,
        out_specs=[pl.BlockSpec((2, rows_l, w), lambda k: (k, 0, 0))] + cast_specs,
        out_shape=[jax.ShapeDtypeStruct((g, rows_l, w), jnp.float32)]
        + [jax.ShapeDtypeStruct(m.shape, _MXU_DTYPE) for m in to_cast],
        scratch_shapes=[pltpu.VMEM((2, 2, nb * _row_pitch(rows_c // nb), sw // 2), jnp.float32),
                        pltpu.VMEM((2, 2, nb * _row_pitch(rows_l // nb), sw // 2), jnp.float32),
                        pltpu.VMEM((2, 2, nb * _row_pitch(rows_l // nb), sw // 2), jnp.float32)],
        compiler_params=pltpu.CompilerParams(dimension_semantics=("arbitrary",),
                                             vmem_limit_bytes=VMEM_LIMIT_BYTES),
        name="s5",
    )(xl, xc, toep, bs, cs, a_pow, dvec, *to_cast)


def _na_row_plan():
    last_q = NA_KROWS - NA_QROWS
    plan = []
    for r0s, qrs in ((lambda lr: 0, lambda lr: lr),
                     (lambda lr: lr, lambda lr: NA_KR // 2 + lr),
                     (lambda lr: last_q - NA_KR // 2, lambda lr: last_q + lr)):
        plan.append([[kl - qrs(lr) + (NA_KR - 1) if r0s(lr) <= kl < r0s(lr) + NA_KR else None
                      for kl in range(NA_KROWS)] for lr in range(NA_QROWS)])
    return plan


def _na_bias_fill(col_ref, bias_s):
    plan = _na_row_plan()
    lane = lax.broadcasted_iota(jnp.int32, (1, 2 * GRID_W), 1)
    masked = jnp.full((GRID_W, 2 * GRID_W), MASK_VALUE, jnp.float32)
    for head in range(bias_s.shape[1]):
        for kind in range(len(plan)):
            for kl in range(NA_KROWS):
                for lp in range(NA_QROWS // 2):
                    a0, a1 = plan[kind][2 * lp][kl], plan[kind][2 * lp + 1][kl]
                    t0 = masked if a0 is None else col_ref[head, a0]
                    t1 = masked if a1 is None else col_ref[head, a1]
                    tile = t0 if a0 == a1 else jnp.where(lane < GRID_W, t0, t1)
                    bias_s[kind, head, kl * GRID_W:(kl + 1) * GRID_W, lp * 2 * GRID_W:(lp + 1) * 2 * GRID_W] = tile


def _na_bias_columns(rpb):
    kc = np.arange(GRID_W)[:, None]
    qc = np.arange(GRID_W)[None, :]
    c0 = np.clip(qc - NA_KC // 2, 0, GRID_W - NA_KC)
    col_ok = (kc >= c0) & (kc < c0 + NA_KC)
    sel_c = ((kc - qc + (NA_KC - 1))[:, :, None] == np.arange(2 * NA_KC - 1)) & col_ok[:, :, None]
    by_col = jnp.einsum('hab,kqb->hakq', rpb.astype(jnp.float32), jnp.asarray(sel_c, jnp.float32), precision=_HI)
    by_col = jnp.where(jnp.asarray(col_ok), by_col * LOG2_E, MASK_VALUE)
    return jnp.concatenate([by_col, by_col], axis=3)


def _row_max(x):
    while x.shape[0] % (2 * SUBLANES) == 0:
        half = x.shape[0] // 2
        x = jnp.maximum(x[:half], x[half:])
    return jnp.max(x, axis=0, keepdims=True)


def _na_kernel(rows, q_ref, k_ref, vt_ref, kc_ref, vct_ref, col_ref, o_ref, bias_ref):
    @pl.when((pl.program_id(0) == 0) & (pl.program_id(1) == 0))
    def _():
        _na_bias_fill(col_ref, bias_ref)

    n_rb = rows // NA_QROWS
    nk = NA_KROWS * GRID_W
    pair = 2 * NA_DH
    lane_first = lax.broadcasted_iota(jnp.int32, (1, pair), 1) < NA_DH
    row_first = lax.broadcasted_iota(jnp.int32, (pair, 1), 0) < NA_DH
    n_heads = q_ref.shape[2] // NA_DH
    n_blocks = q_ref.shape[1] // NA_BLOCK

    def block(j):
        rb = pl.program_id(1) * n_blocks + j
        kind = jnp.where(rb == 0, 0, jnp.where(rb == n_rb - 1, 2, 1))
        first_row = jnp.clip(rb * NA_QROWS - NA_KR // 2, 0, rows - NA_KROWS)
        return kind, pl.multiple_of(first_row * GRID_W, GRID_W), first_row // NA_QROWS

    geometry = [block(j) for j in range(n_blocks)]

    def scores(u):
        j, h = divmod(u, n_heads)
        kind, start, _ = geometry[j]
        cols = slice(h // 2 * pair, (h // 2 + 1) * pair)
        lane_mine = lane_first if h % 2 == 0 else jnp.logical_not(lane_first)
        qm = jnp.where(lane_mine, q_ref[0, j * NA_BLOCK:(j + 1) * NA_BLOCK, cols],
                       jnp.zeros((NA_BLOCK, pair), q_ref.dtype))
        s_w = _mm_nt(k_ref[0, pl.ds(start, nk), cols], qm) + bias_ref[kind, h]
        s_c = _mm_nt(kc_ref[0, :, cols], qm)
        return s_w, s_c

    def weights(s_w, s_c):
        mx = jnp.maximum(_row_max(s_w), _row_max(s_c))
        return jnp.exp2(s_w - mx).astype(vt_ref.dtype), jnp.exp2(s_c - mx).astype(vt_ref.dtype)

    def values(u, p_w, p_c):
        j, h = divmod(u, n_heads)
        blk0 = geometry[j][2]
        cols = slice(h // 2 * pair, (h // 2 + 1) * pair)
        row_mine = row_first if h % 2 == 0 else jnp.logical_not(row_first)
        vtw = jnp.concatenate([vt_ref[blk0 + t, cols, :] for t in range(nk // NA_BLOCK)], axis=1)
        vtc = jnp.concatenate([vct_ref[t, cols, :] for t in range(vct_ref.shape[0])], axis=1)
        o = (_mm(jnp.where(row_mine, vtw, jnp.ones_like(vtw)), p_w)
             + _mm(jnp.where(row_mine, vtc, jnp.ones_like(vtc)), p_c))
        num, den = (o[:NA_DH], o[NA_DH:]) if h % 2 == 0 else (o[NA_DH:], o[:NA_DH])
        return num / den

    n_units = n_blocks * n_heads
    s = {0: scores(0), 1: scores(1)}
    p = {0: weights(*s.pop(0))}
    outs = []
    for u in range(n_units):
        if u + 2 < n_units:
            s[u + 2] = scores(u + 2)
        if u + 1 < n_units:
            p[u + 1] = weights(*s.pop(u + 1))
        outs.append(values(u, *p.pop(u)))
        if len(outs) == n_heads:
            j = u // n_heads
            o_ref[0, j * NA_BLOCK:(j + 1) * NA_BLOCK, :] = jnp.concatenate(outs, axis=0).T.astype(o_ref.dtype)
            outs = []


def _na(qk, vt, qk_c, vt_c, bias_cols, na_w):
    b, s, _ = qk.shape
    l = qk_c.shape[1]
    rows = s // GRID_W
    tq = NA_STEP_BLOCKS * NA_BLOCK
    n_kinds = len(_na_row_plan())
    return pl.pallas_call(
        functools.partial(_na_kernel, rows),
        grid=(b, s // tq),
        in_specs=[pl.BlockSpec((1, tq, na_w), lambda bi, rb: (bi, rb, 0)),
                  pl.BlockSpec((1, s, na_w), lambda bi, rb: (bi, 0, 1)),
                  pl.BlockSpec((s // NA_BLOCK, na_w, NA_BLOCK), lambda bi, rb: (bi, 0, 0)),
                  pl.BlockSpec((1, l, na_w), lambda bi, rb: (bi, 0, 1)),
                  pl.BlockSpec((l // NA_BLOCK, na_w, NA_BLOCK), lambda bi, rb: (bi, 0, 0)),
                  _const_spec(bias_cols.shape)],
        out_specs=pl.BlockSpec((1, tq, na_w), lambda bi, rb: (bi, rb, 0)),
        out_shape=jax.ShapeDtypeStruct((b, s, na_w), _MXU_DTYPE),
        scratch_shapes=[pltpu.VMEM((n_kinds, na_w // NA_DH, NA_KROWS * GRID_W, NA_BLOCK), jnp.float32)],
        compiler_params=pltpu.CompilerParams(
            dimension_semantics=("arbitrary", "arbitrary"),
            vmem_limit_bytes=VMEM_LIMIT_BYTES),
        name="na",
    )(qk, qk, vt, qk_c, vt_c, bias_cols)


def _out_ffn_kernel(alpha, x1_ref, yna_ref, ys5_ref, mod_a_ref, mod_b_ref, lng_ref, lnb_ref, wglu_ref, bglu_ref,
                    wout_ref, wup_ref, wdn_ref, o_ref, p_s, y_s, h_s, x2_s):
    @pl.when(pl.program_id(0) == 0)
    def _():
        h_s[...] = jnp.zeros_like(h_s)
        x2_s[...] = jnp.zeros_like(x2_s)

    mod_a = mod_a_ref[0]
    na_w = yna_ref.shape[1]
    stage_a = {}

    def relayout():
        _chunks_to_tokens(ys5_ref, y_s)

    def gated():
        gl = jax.nn.gelu(jnp.concatenate([y_s[v] for v in range(y_s.shape[0])], axis=1))
        gate = jax.nn.sigmoid(_mm(gl.astype(p_s.dtype), wglu_ref[...]) + bglu_ref[...])
        stage_a["glu"] = (gl * gate).astype(p_s.dtype)

    def mixed():
        y = _mm(yna_ref[...], wout_ref[:na_w, :]) + _mm(stage_a["glu"], wout_ref[na_w:, :])
        x2 = _layer_norm(alpha * x1_ref[...] + mod_a[5:6] * y, lng_ref[1:2], lnb_ref[1:2])
        stage_a["x2"] = x2
        stage_a["h"] = (x2 * (1.0 + mod_a[7:8]) + mod_a[6:7]).astype(h_s.dtype)

    mod_b = mod_b_ref[0]
    _swiglu_hidden(h_s[...], wup_ref, p_s, between=(relayout, gated, mixed))
    n_blk = o_ref.shape[0] // NA_BLOCK
    fs = [_mm(p_s[t * NA_BLOCK:(t + 1) * NA_BLOCK, :], wdn_ref[...]) for t in range(n_blk)]
    for t in range(n_blk):
        rows = slice(t * NA_BLOCK, (t + 1) * NA_BLOCK)
        o_ref[rows, :] = _layer_norm(alpha * x2_s[rows, :] + (0.5 * mod_b[8:9]) * fs[t], lng_ref[2:3], lnb_ref[2:3])

    x2_s[...] = stage_a["x2"]
    h_s[...] = stage_a["h"]


def _out_ffn(x1, yna, ys5, mod, tiles_per_mod, lng, lnb, wglu, bglu, wout, wup, wdn, alpha, tm):
    t, d = x1.shape
    ff = wdn.shape[0]
    na_w = yna.shape[1]
    groups, _, cw = ys5.shape
    s5_w = groups * S5_CG
    n = t // tm

    def cur(i):
        return jnp.minimum(i, n - 1)

    def prev(i):
        return jnp.maximum(i - 1, 0)

    return pl.pallas_call(
        functools.partial(_out_ffn_kernel, alpha),
        grid=(n + 1,),
        in_specs=[pl.BlockSpec((tm, d), lambda i: (cur(i), 0)),
                  pl.BlockSpec((tm, na_w), lambda i: (cur(i), 0)),
                  pl.BlockSpec((groups, tm // CHUNK, cw), lambda i: (0, cur(i), 0)),
                  pl.BlockSpec((1, N_MOD, d), lambda i: (cur(i) // tiles_per_mod, 0, 0)),
                  pl.BlockSpec((1, N_MOD, d), lambda i: (prev(i) // tiles_per_mod, 0, 0)),
                  _const_spec(lng.shape), _const_spec(lnb.shape),
                  _const_spec(wglu.shape), _const_spec(bglu.shape), _const_spec(wout.shape),
                  _const_spec(wup.shape), _const_spec(wdn.shape)],
        out_specs=pl.BlockSpec((tm, d), lambda i: (prev(i), 0)),
        out_shape=jax.ShapeDtypeStruct((t, d), jnp.float32),
        scratch_shapes=[pltpu.VMEM((tm, ff), _MXU_DTYPE), pltpu.VMEM((s5_w // LANES, tm, LANES), jnp.float32),
                        pltpu.VMEM((tm, d), _MXU_DTYPE), pltpu.VMEM((tm, d), jnp.float32)],
        compiler_params=pltpu.CompilerParams(dimension_semantics=("arbitrary",),
                                             vmem_limit_bytes=VMEM_LIMIT_BYTES),
        name="out_ffn",
    )(x1, yna, ys5, mod, mod, lng, lnb, wglu, bglu, wout, wup, wdn)


def _pick_tile(n, target):
    t = min(n, target)
    while n % t:
        t //= 2
    return t


def kernel(x, c, ctx, c_ctx, w_ada, b_ada, ln_g, ln_b, ffn1_w_up, ffn1_w_down, w_in, na_rpb, s5_a_re, s5_a_im, s5_log_dt, s5_b_re, s5_b_im, s5_c_re, s5_c_im, s5_d, s5_w_glu, s5_b_glu, w_out, ffn2_w_up, ffn2_w_down):
    depth = w_ada.shape[0]
    assert depth == 1, "single-layer stack only"
    bsz, seq, d = x.shape
    l_ctx = ctx.shape[1]
    s5_w = s5_w_glu.shape[1]
    na_w = w_out.shape[1] - s5_w
    groups = s5_w // S5_CG
    rows = seq // GRID_W
    assert bsz % 8 == 0 and seq % (NA_STEP_BLOCKS * NA_BLOCK) == 0 and rows >= NA_KROWS
    assert l_ctx % NA_BLOCK == 0 and groups % 2 == 0 and 2 * s5_a_re.shape[3] == LANES
    assert na_w % (2 * NA_DH) == 0 and ffn1_w_down.shape[1] % MXU_TILE == 0
    alpha = (2.0 * depth) ** 0.25
    cd = _MXU_DTYPE

    n_rows = -(-(bsz + 1) // 8) * 8
    cc = jnp.zeros((n_rows, d), jnp.float32).at[:bsz].set(c).at[bsz].set(c_ctx)
    n_ada = w_ada.shape[2]
    mod = _ada(cc, w_ada[0], b_ada[0][None, :], _pick_tile(n_ada, 1152)).reshape(n_rows, N_MOD, d)

    lng, lnb = ln_g[0], ln_b[0]
    wup1, wdn1 = ffn1_w_up[0].astype(cd), ffn1_w_down[0].astype(cd)
    win = w_in[0].astype(cd)

    tm = _pick_tile(seq, 512)
    x1, qk, vt, xs = _ffn_in(x.reshape(bsz * seq, d), mod, seq // tm, lng, lnb, wup1, wdn1, win, alpha, na_w, tm)
    tmc = _pick_tile(bsz * l_ctx, 512)
    _, qk_c, vt_c, xs_c = _ffn_in(ctx.reshape(bsz * l_ctx, d), mod[bsz:bsz + 1], None, lng, lnb, wup1, wdn1, win,
                                  alpha, na_w, tmc)

    tables = _s5_tables(s5_a_re[0], s5_a_im[0], s5_log_dt[0], s5_b_re[0], s5_b_im[0], s5_c_re[0], s5_c_im[0],
                        s5_d[0])
    y_s5, wup2, wdn2, wout, wglu = _s5(xs, xs_c, *tables, bsz,
                                       (ffn2_w_up[0], ffn2_w_down[0], w_out[0], s5_w_glu[0]))

    y_na = _na(qk.reshape(bsz, seq, 2 * na_w), vt, qk_c.reshape(bsz, l_ctx, 2 * na_w), vt_c, _na_bias_columns(na_rpb[0]),
               na_w).reshape(bsz * seq, na_w)

    out = _out_ffn(x1, y_na, y_s5, mod, seq // tm, lng, lnb, wglu, s5_b_glu[0][None, :], wout, wup2, wdn2,
                   alpha, tm)
    return out.reshape(bsz, seq, d)
```

```python
import functools

import jax
import jax.numpy as jnp
import numpy as np
from jax import lax
from jax.experimental import pallas as pl
from jax.experimental.pallas import tpu as pltpu

GRID_W = 64
NA_DH = 64
NA_KR = 8
NA_KC = 16
S5_CG = 16
S5_P = 64
LN_EPS = 1e-6
N_MOD = 9

CHUNK = 16
NA_QROWS = 4
NA_KROWS = 12
NA_BLOCK = NA_QROWS * GRID_W
NA_STEP_BLOCKS = 4
MASK_VALUE = -1e30
LOG2_E = 1.4426950408889634
SCAN_UNROLL = 4

MXU_TILE = 256
LANES = 128
SUBLANES = 8
VMEM_LIMIT_BYTES = 56 * 1024 * 1024

_MXU_DTYPE = jnp.bfloat16
_HI = lax.Precision.HIGHEST


def _mm(a, b):
    return jnp.dot(a, b, preferred_element_type=jnp.float32)


def _mm_nt(a, b):
    return lax.dot_general(a, b, (((1,), (1,)), ((), ())), preferred_element_type=jnp.float32)


def _const_spec(shape):
    nd = len(shape)
    return pl.BlockSpec(shape, lambda *_: (0,) * nd, pipeline_mode=pl.Buffered(1))


def _layer_norm(r, g, b):
    mu = jnp.mean(r, axis=-1, keepdims=True)
    d = r - mu
    var = jnp.mean(d * d, axis=-1, keepdims=True)
    return d * lax.rsqrt(var + LN_EPS) * g + b


def _swiglu_hidden(h, wup_ref, p_s, between=()):
    ff = p_s.shape[1]
    n_chunks = ff // MXU_TILE
    slots = [(k + 1) * n_chunks // (len(between) + 1) for k in range(len(between))]
    for j in range(n_chunks):
        for slot, fn in zip(slots, between):
            if slot == j:
                fn()
        lo = j * MXU_TILE
        a = _mm(h, wup_ref[:, lo:lo + MXU_TILE])
        g = _mm(h, wup_ref[:, ff + lo:ff + lo + MXU_TILE])
        p_s[:, lo:lo + MXU_TILE] = (g * jax.nn.sigmoid(g) * a).astype(p_s.dtype)


def _swiglu(h, wup_ref, wdn_ref, p_s, between=()):
    _swiglu_hidden(h, wup_ref, p_s, between)
    return _mm(p_s[...], wdn_ref[...])


def _block_transpose(tiles):
    per = len(tiles)
    blk = lax.broadcasted_iota(jnp.int32, (1, LANES), 1) // S5_CG
    tiles = list(tiles)
    d = per // 2
    while d:
        upper = (blk & d) != 0
        for a in range(per):
            if a & d:
                continue
            lo, hi = tiles[a], tiles[a + d]
            tiles[a] = jnp.where(upper, pltpu.roll(hi, S5_CG * d, axis=1), lo)
            tiles[a + d] = jnp.where(upper, hi, pltpu.roll(lo, LANES - S5_CG * d, axis=1))
        d //= 2
    return tiles


def _tokens_to_chunks(u_s, xs_ref, row0, n_rows):
    n_ch = n_rows // CHUNK
    ch0 = row0 // CHUNK
    per = LANES // S5_CG
    for v in range(u_s.shape[0]):
        for w in range(CHUNK // per):
            by_group = _block_transpose(
                [u_s[v, pl.ds(row0 + per * w + jj, n_ch, stride=CHUNK), :] for jj in range(per)])
            for gg in range(per):
                xs_ref[per * v + gg, ch0:ch0 + n_ch, w * LANES:(w + 1) * LANES] = by_group[gg].astype(xs_ref.dtype)


def _chunks_to_tokens(ys_ref, y_s):
    n_ch = y_s.shape[1] // CHUNK
    per = LANES // S5_CG
    for v in range(y_s.shape[0]):
        for w in range(CHUNK // per):
            by_phase = _block_transpose([ys_ref[per * v + gg, :, w * LANES:(w + 1) * LANES] for gg in range(per)])
            for jj in range(per):
                y_s[v, pl.ds(per * w + jj, n_ch, stride=CHUNK), :] = by_phase[jj]


def _ada_kernel(c_ref, w_ref, b_ref, o_ref):
    cc = c_ref[...]
    s = cc * jax.nn.sigmoid(cc)
    w = w_ref[...]
    s_hi = s.astype(_MXU_DTYPE)
    s_lo = (s - s_hi.astype(jnp.float32)).astype(_MXU_DTYPE)
    w_hi = w.astype(_MXU_DTYPE)
    w_lo = (w - w_hi.astype(jnp.float32)).astype(_MXU_DTYPE)
    o_ref[...] = _mm(s_hi, w_hi) + (_mm(s_lo, w_hi) + _mm(s_hi, w_lo)) + b_ref[...]


def _ada(cc, w, b, bn):
    rows, d = cc.shape
    n = w.shape[1]
    return pl.pallas_call(
        _ada_kernel,
        grid=(n // bn,),
        in_specs=[pl.BlockSpec((rows, d), lambda j: (0, 0)),
                  pl.BlockSpec((d, bn), lambda j: (0, j)),
                  pl.BlockSpec((1, bn), lambda j: (0, j))],
        out_specs=pl.BlockSpec((rows, bn), lambda j: (0, j)),
        out_shape=jax.ShapeDtypeStruct((rows, n), jnp.float32),
        compiler_params=pltpu.CompilerParams(dimension_semantics=("arbitrary",),
                                             vmem_limit_bytes=VMEM_LIMIT_BYTES),
        name="ada",
    )(cc, w, b)


def _ffn_in_kernel(alpha, na_w, x_ref, mod_ref, lng_ref, lnb_ref, wup_ref, wdn_ref, win_ref,
                   x1_ref, qk_ref, vt_ref, xs_ref, p_s, u_s):
    mod = mod_ref[0]
    h = (x_ref[...] * (1.0 + mod[1:2]) + mod[0:1]).astype(p_s.dtype)
    _swiglu_hidden(h, wup_ref, p_s)
    n_blk = vt_ref.shape[0]
    ys = [_mm(p_s[t * NA_BLOCK:(t + 1) * NA_BLOCK, :], wdn_ref[...]) for t in range(n_blk)]
    for t in range(n_blk):
        rows = slice(t * NA_BLOCK, (t + 1) * NA_BLOCK)
        x1 = _layer_norm(alpha * x_ref[rows, :] + (0.5 * mod[2:3]) * ys[t], lng_ref[0:1], lnb_ref[0:1])
        x1_ref[rows, :] = x1
        h2 = (x1 * (1.0 + mod[4:5]) + mod[3:4]).astype(p_s.dtype)
        pu = _mm(h2, win_ref[:, 3 * na_w:])
        for v in range(u_s.shape[0]):
            u_s[v, rows, :] = pu[:, v * LANES:(v + 1) * LANES]
        _tokens_to_chunks(u_s, xs_ref, t * NA_BLOCK, NA_BLOCK)
        pr = _mm(h2, win_ref[:, :3 * na_w])
        qk_ref[rows, :na_w] = (pr[:, :na_w] * (NA_DH ** -0.5 * LOG2_E)).astype(qk_ref.dtype)
        qk_ref[rows, na_w:] = pr[:, na_w:2 * na_w].astype(qk_ref.dtype)
        vt_ref[t] = pr[:, 2 * na_w:].T.astype(vt_ref.dtype)


def _ffn_in(x2d, mod, tiles_per_mod, lng, lnb, wup, wdn, win, alpha, na_w, tm):
    t, d = x2d.shape
    ff = wdn.shape[0]
    ncol = win.shape[1]
    s5_w = ncol - 3 * na_w
    groups, cw = s5_w // S5_CG, CHUNK * S5_CG
    if tiles_per_mod is None:
        mod_map = lambda i: (0, 0, 0)
    else:
        mod_map = lambda i: (i // tiles_per_mod, 0, 0)
    return pl.pallas_call(
        functools.partial(_ffn_in_kernel, alpha, na_w),
        grid=(t // tm,),
        in_specs=[pl.BlockSpec((tm, d), lambda i: (i, 0)),
                  pl.BlockSpec((1, N_MOD, d), mod_map),
                  _const_spec(lng.shape), _const_spec(lnb.shape),
                  _const_spec(wup.shape), _const_spec(wdn.shape), _const_spec(win.shape)],
        out_specs=[pl.BlockSpec((tm, d), lambda i: (i, 0)),
                   pl.BlockSpec((tm, 2 * na_w), lambda i: (i, 0)),
                   pl.BlockSpec((tm // NA_BLOCK, na_w, NA_BLOCK), lambda i: (i, 0, 0)),
                   pl.BlockSpec((groups, tm // CHUNK, cw), lambda i: (0, i, 0))],
        out_shape=[jax.ShapeDtypeStruct((t, d), jnp.float32),
                   jax.ShapeDtypeStruct((t, 2 * na_w), _MXU_DTYPE),
                   jax.ShapeDtypeStruct((t // NA_BLOCK, na_w, NA_BLOCK), _MXU_DTYPE),
                   jax.ShapeDtypeStruct((groups, t // CHUNK, cw), jnp.float32)],
        scratch_shapes=[pltpu.VMEM((tm, ff), _MXU_DTYPE), pltpu.VMEM((s5_w // LANES, tm, LANES), jnp.float32)],
        compiler_params=pltpu.CompilerParams(dimension_semantics=("arbitrary",),
                                             vmem_limit_bytes=VMEM_LIMIT_BYTES),
        name="ffn_in",
    )(x2d, mod, lng, lnb, wup, wdn, win)


def _toeplitz_kernel(lhs_ref, rhs_ref, o_ref):
    width = o_ref.shape[2]
    lane = lax.broadcasted_iota(jnp.int32, (1, width), 1)
    for g in range(o_ref.shape[0]):
        k_f = jnp.dot(lhs_ref[0, g], rhs_ref[0, g], precision=_HI, preferred_element_type=jnp.float32)
        k_b = jnp.dot(lhs_ref[1, g], rhs_ref[1, g], precision=_HI, preferred_element_type=jnp.float32)
        for i in range(CHUNK):
            lo, hi = i * S5_CG, (i + 1) * S5_CG
            fwd = k_f if lo == 0 else jnp.where(lane >= lo, pltpu.roll(k_f, lo, axis=1), 0.0)
            bwd = k_b if hi == width else jnp.where(lane < hi, pltpu.roll(k_b, hi, axis=1), 0.0)
            o_ref[g, lo:hi, :] = (fwd + bwd).astype(o_ref.dtype)


def _toeplitz(lhs, rhs):
    _, g, cg, kdim = lhs.shape
    width = rhs.shape[3]
    gb = _pick_tile(g, 8)
    return pl.pallas_call(
        _toeplitz_kernel,
        grid=(g // gb,),
        in_specs=[pl.BlockSpec((2, gb, cg, kdim), lambda k: (0, k, 0, 0)),
                  pl.BlockSpec((2, gb, kdim, width), lambda k: (0, k, 0, 0))],
        out_specs=pl.BlockSpec((gb, width, width), lambda k: (k, 0, 0)),
        out_shape=jax.ShapeDtypeStruct((g, width, width), _MXU_DTYPE),
        compiler_params=pltpu.CompilerParams(dimension_semantics=("arbitrary",)),
        name="s5_toeplitz",
    )(lhs, rhs)


def _s5_tables(a_re, a_im, log_dt, b_re, b_im, c_re, c_im, d_skip):
    g, p = a_re.shape[1], a_re.shape[2]
    f32 = jnp.float32
    a_re, a_im = a_re.astype(f32), a_im.astype(f32)
    dt = jnp.exp(log_dt.astype(f32))[..., None]
    lr, li = a_re * dt, a_im * dt

    width = CHUNK * S5_CG
    asc = np.arange(CHUNK)
    desc = asc[::-1].copy()
    lane_m, lane_c = np.arange(width) // S5_CG, np.arange(width) % S5_CG

    def powers(expo, lr_b, li_b):
        e = jnp.asarray(expo, f32)
        mag = jnp.exp(e * lr_b)
        return mag * jnp.cos(e * li_b), mag * jnp.sin(e * li_b)


    ab_r, ab_i = jnp.exp(lr) * jnp.cos(li), jnp.exp(lr) * jnp.sin(li)
    den = a_re ** 2 + a_im ** 2
    nr, ni = ab_r - 1.0, ab_i
    sr = ((nr * a_re + ni * a_im) / den)[:, :, None, :]
    si = ((ni * a_re - nr * a_im) / den)[:, :, None, :]
    bt_re, bt_im = jnp.swapaxes(b_re, 2, 3).astype(f32), jnp.swapaxes(b_im, 2, 3).astype(f32)
    bt_r = sr * bt_re - si * bt_im
    bt_i = sr * bt_im + si * bt_re

    tile_c = jnp.asarray(lane_c[None, :] == np.arange(S5_CG)[:, None], f32)
    ct_r = jnp.einsum('dgcp,cl->dgpl', c_re.astype(f32), tile_c, precision=_HI)
    ct_i = jnp.einsum('dgcp,cl->dgpl', c_im.astype(f32), tile_c, precision=_HI)

    rep_m = jnp.asarray(lane_m[None, :] == np.arange(CHUNK)[:, None], f32)

    def by_lane(expo):
        pw = powers(np.stack(expo)[:, None, None, :], lr[..., None], li[..., None])
        return tuple(jnp.einsum('dgpm,ml->dgpl', v, rep_m, precision=_HI) for v in pw)

    q_r, q_i = by_lane([asc, desc])
    rhs = jnp.concatenate([ct_r * q_r - ct_i * q_i, ct_r * q_i + ct_i * q_r], axis=2)
    toep = _toeplitz(jnp.concatenate([bt_r, -bt_i], axis=3), rhs)

    v_r, v_i = by_lane([asc + 1, desc + 1])
    cs_r = ct_r * v_r - ct_i * v_i
    cs_i = -(ct_r * v_i + ct_i * v_r)

    w_r, w_i = (jnp.repeat(v, S5_CG, axis=2) for v in
                powers(np.stack([desc, asc])[:, None, :, None], lr[:, :, None, :], li[:, :, None, :]))
    rows_r, rows_i = jnp.tile(bt_r, (1, 1, CHUNK, 1)), jnp.tile(bt_i, (1, 1, CHUNK, 1))
    bs_r = w_r * rows_r - w_i * rows_i
    bs_i = w_r * rows_i + w_i * rows_r
    even = jnp.asarray((np.arange(g) % 2) == 0)[None, :, None, None]
    bs = jnp.concatenate([jnp.where(even, bs_r, 0.0), jnp.where(even, 0.0, bs_r),
                          jnp.where(even, bs_i, 0.0), jnp.where(even, 0.0, bs_i)], axis=3)
    cs = jnp.concatenate([jnp.where(even, cs_r, 0.0), jnp.where(even, 0.0, cs_r),
                          jnp.where(even, cs_i, 0.0), jnp.where(even, 0.0, cs_i)], axis=2)

    e_r, e_i = jnp.exp(CHUNK * lr), CHUNK * li
    half = 2 * p
    a_pow = jnp.stack([(e_r * jnp.cos(e_i)).reshape(2, g // 2, half),
                       (e_r * jnp.sin(e_i)).reshape(2, g // 2, half)], axis=2)
    dvec = jnp.dot(d_skip.astype(f32), tile_c, precision=_HI)[:, None, :]
    return toep, bs.astype(_MXU_DTYPE), cs.astype(_MXU_DTYPE), a_pow, dvec


def _row_pitch(n):
    return n + (4 - n) % 8


def _s5_kernel(nb, n_cast, xl_ref, xc_ref, toep_ref, bs_ref, cs_ref, apow_ref, dvec_ref, *refs):
    cast_in, (y_ref, *cast_out), (zc_s, zl_s, h_s) = refs[:n_cast], refs[n_cast:2 * n_cast + 1], refs[2 * n_cast + 1:]
    for src, dst in zip(cast_in, cast_out):
        dst[...] = src[...].astype(dst.dtype)
    rows_l = xl_ref.shape[1]
    rows_c = xc_ref.shape[1]
    n_l, n_c = rows_l // nb, rows_c // nb
    pitch_l, pitch_c = _row_pitch(n_l), _row_pitch(n_c)
    half = zl_s.shape[3]
    xl = [xl_ref[g] for g in range(2)]
    xlb = [v.astype(_MXU_DTYPE) for v in xl]
    xcb = [xc_ref[g].astype(_MXU_DTYPE) for g in range(2)]

    for d in range(2):
        zc = _mm(xcb[0], bs_ref[d, 0]) + _mm(xcb[1], bs_ref[d, 1])
        zl = _mm(xlb[0], bs_ref[d, 0]) + _mm(xlb[1], bs_ref[d, 1])
        for ri in range(2):
            for b in range(nb):
                zc_s[d, ri, b * pitch_c:b * pitch_c + n_c, :] = zc[b * n_c:(b + 1) * n_c, ri * half:(ri + 1) * half]
                zl_s[d, ri, b * pitch_l:b * pitch_l + n_l, :] = zl[b * n_l:(b + 1) * n_l, ri * half:(ri + 1) * half]

    a_r = [jnp.broadcast_to(apow_ref[d, 0, 0:1, :], (nb, half)) for d in range(2)]
    a_i = [jnp.broadcast_to(apow_ref[d, 0, 1:2, :], (nb, half)) for d in range(2)]

    def step(d, n, latent, h_r, h_i):
        if latent:
            h_s[d, 0, pl.ds(n, nb, stride=pitch_l), :] = h_r
            h_s[d, 1, pl.ds(n, nb, stride=pitch_l), :] = h_i
            z_r = zl_s[d, 0, pl.ds(n, nb, stride=pitch_l), :]
            z_i = zl_s[d, 1, pl.ds(n, nb, stride=pitch_l), :]
        else:
            z_r = zc_s[d, 0, pl.ds(n, nb, stride=pitch_c), :]
            z_i = zc_s[d, 1, pl.ds(n, nb, stride=pitch_c), :]
        return (a_r[d] * h_r - a_i[d] * h_i + z_r,
                a_r[d] * h_i + a_i[d] * h_r + z_i)

    def ctx_body(n, carry):
        f_r, f_i, b_r, b_i = carry
        f_r, f_i = step(0, n, False, f_r, f_i)
        b_r, b_i = step(1, n_c - 1 - n, False, b_r, b_i)
        return f_r, f_i, b_r, b_i

    def lat_body(n, carry):
        f_r, f_i, b_r, b_i = carry
        f_r, f_i = step(0, n, True, f_r, f_i)
        b_r, b_i = step(1, n_l - 1 - n, True, b_r, b_i)
        return f_r, f_i, b_r, b_i

    zero = jnp.zeros((nb, half), jnp.float32)
    carry = lax.fori_loop(0, n_c, ctx_body, (zero, zero, zero, zero), unroll=SCAN_UNROLL)
    lax.fori_loop(0, n_l, lat_body, carry, unroll=SCAN_UNROLL)

    def plane(d, ri):
        return jnp.concatenate([h_s[d, ri, b * pitch_l:b * pitch_l + n_l, :] for b in range(nb)], axis=0)

    hb = [jnp.concatenate([plane(d, 0), plane(d, 1)], axis=1).astype(_MXU_DTYPE) for d in range(2)]
    for g in range(2):
        y = _mm(xlb[g], toep_ref[g]) + _mm(hb[0], cs_ref[0, g]) + _mm(hb[1], cs_ref[1, g])
        y_ref[g] = y + xl[g] * dvec_ref[g]


def _s5(xl, xc, toep, bs, cs, a_pow, dvec, nb, to_cast):
    g, rows_l, w = xl.shape
    rows_c = xc.shape[1]
    sw = bs.shape[3]
    steps = g // 2
    slabs = [m.shape[0] // steps for m in to_cast]
    assert all(m.shape[0] == s * steps and s % (2 * SUBLANES) == 0 for m, s in zip(to_cast, slabs))
    cast_specs = [pl.BlockSpec((s, m.shape[1]), lambda k: (k, 0)) for m, s in zip(to_cast, slabs)]
    return pl.pallas_call(
        functools.partial(_s5_kernel, nb, len(to_cast)),
        grid=(steps,),
        in_specs=[pl.BlockSpec((2, rows_l, w), lambda k: (k, 0, 0)),
                  pl.BlockSpec((2, rows_c, w), lambda k: (k, 0, 0)),
                  pl.BlockSpec((2, w, w), lambda k: (k, 0, 0)),
                  pl.BlockSpec((2, 2, w, sw), lambda k: (0, k, 0, 0)),
                  pl.BlockSpec((2, 2, sw, w), lambda k: (0, k, 0, 0)),
                  pl.BlockSpec((2, 1, 2, sw // 2), lambda k: (0, k, 0, 0)),
                  pl.BlockSpec((2, 1, w), lambda k: (k, 0, 0))] + cast_specs,
        out_specs=[pl.BlockSpec((2, rows_l, w), lambda k: (k, 0, 0))] + cast_specs,
        out_shape=[jax.ShapeDtypeStruct((g, rows_l, w), jnp.float32)]
        + [jax.ShapeDtypeStruct(m.shape, _MXU_DTYPE) for m in to_cast],
        scratch_shapes=[pltpu.VMEM((2, 2, nb * _row_pitch(rows_c // nb), sw // 2), jnp.float32),
                        pltpu.VMEM((2, 2, nb * _row_pitch(rows_l // nb), sw // 2), jnp.float32),
                        pltpu.VMEM((2, 2, nb * _row_pitch(rows_l // nb), sw // 2), jnp.float32)],
        compiler_params=pltpu.CompilerParams(dimension_semantics=("arbitrary",),
                                             vmem_limit_bytes=VMEM_LIMIT_BYTES),
        name="s5",
    )(xl, xc, toep, bs, cs, a_pow, dvec, *to_cast)


def _na_row_plan():
    last_q = NA_KROWS - NA_QROWS
    plan = []
    for r0s, qrs in ((lambda lr: 0, lambda lr: lr),
                     (lambda lr: lr, lambda lr: NA_KR // 2 + lr),
                     (lambda lr: last_q - NA_KR // 2, lambda lr: last_q + lr)):
        plan.append([[kl - qrs(lr) + (NA_KR - 1) if r0s(lr) <= kl < r0s(lr) + NA_KR else None
                      for kl in range(NA_KROWS)] for lr in range(NA_QROWS)])
    return plan


def _na_bias_fill(col_ref, bias_s):
    plan = _na_row_plan()
    lane = lax.broadcasted_iota(jnp.int32, (1, 2 * GRID_W), 1)
    masked = jnp.full((GRID_W, 2 * GRID_W), MASK_VALUE, jnp.float32)
    for head in range(bias_s.shape[1]):
        for kind in range(len(plan)):
            for kl in range(NA_KROWS):
                for lp in range(NA_QROWS // 2):
                    a0, a1 = plan[kind][2 * lp][kl], plan[kind][2 * lp + 1][kl]
                    t0 = masked if a0 is None else col_ref[head, a0]
                    t1 = masked if a1 is None else col_ref[head, a1]
                    tile = t0 if a0 == a1 else jnp.where(lane < GRID_W, t0, t1)
                    bias_s[kind, head, kl * GRID_W:(kl + 1) * GRID_W, lp * 2 * GRID_W:(lp + 1) * 2 * GRID_W] = tile


def _na_bias_columns(rpb):
    kc = np.arange(GRID_W)[:, None]
    qc = np.arange(GRID_W)[None, :]
    c0 = np.clip(qc - NA_KC // 2, 0, GRID_W - NA_KC)
    col_ok = (kc >= c0) & (kc < c0 + NA_KC)
    sel_c = ((kc - qc + (NA_KC - 1))[:, :, None] == np.arange(2 * NA_KC - 1)) & col_ok[:, :, None]
    by_col = jnp.einsum('hab,kqb->hakq', rpb.astype(jnp.float32), jnp.asarray(sel_c, jnp.float32), precision=_HI)
    by_col = jnp.where(jnp.asarray(col_ok), by_col * LOG2_E, MASK_VALUE)
    return jnp.concatenate([by_col, by_col], axis=3)


def _row_max(x):
    while x.shape[0] % (2 * SUBLANES) == 0:
        half = x.shape[0] // 2
        x = jnp.maximum(x[:half], x[half:])
    return jnp.max(x, axis=0, keepdims=True)


def _na_kernel(rows, q_ref, k_ref, vt_ref, kc_ref, vct_ref, col_ref, o_ref, bias_ref):
    @pl.when((pl.program_id(0) == 0) & (pl.program_id(1) == 0))
    def _():
        _na_bias_fill(col_ref, bias_ref)

    n_rb = rows // NA_QROWS
    nk = NA_KROWS * GRID_W
    pair = 2 * NA_DH
    lane_first = lax.broadcasted_iota(jnp.int32, (1, pair), 1) < NA_DH
    row_first = lax.broadcasted_iota(jnp.int32, (pair, 1), 0) < NA_DH
    n_heads = q_ref.shape[2] // NA_DH
    n_blocks = q_ref.shape[1] // NA_BLOCK

    def block(j):
        rb = pl.program_id(1) * n_blocks + j
        kind = jnp.where(rb == 0, 0, jnp.where(rb == n_rb - 1, 2, 1))
        first_row = jnp.clip(rb * NA_QROWS - NA_KR // 2, 0, rows - NA_KROWS)
        return kind, pl.multiple_of(first_row * GRID_W, GRID_W), first_row // NA_QROWS

    geometry = [block(j) for j in range(n_blocks)]

    def scores(u):
        j, h = divmod(u, n_heads)
        kind, start, _ = geometry[j]
        cols = slice(h // 2 * pair, (h // 2 + 1) * pair)
        lane_mine = lane_first if h % 2 == 0 else jnp.logical_not(lane_first)
        qm = jnp.where(lane_mine, q_ref[0, j * NA_BLOCK:(j + 1) * NA_BLOCK, cols],
                       jnp.zeros((NA_BLOCK, pair), q_ref.dtype))
        s_w = _mm_nt(k_ref[0, pl.ds(start, nk), cols], qm) + bias_ref[kind, h]
        s_c = _mm_nt(kc_ref[0, :, cols], qm)
        return s_w, s_c

    def weights(s_w, s_c):
        mx = jnp.maximum(_row_max(s_w), _row_max(s_c))
        return jnp.exp2(s_w - mx).astype(vt_ref.dtype), jnp.exp2(s_c - mx).astype(vt_ref.dtype)

    def values(u, p_w, p_c):
        j, h = divmod(u, n_heads)
        blk0 = geometry[j][2]
        cols = slice(h // 2 * pair, (h // 2 + 1) * pair)
        row_mine = row_first if h % 2 == 0 else jnp.logical_not(row_first)
        vtw = jnp.concatenate([vt_ref[blk0 + t, cols, :] for t in range(nk // NA_BLOCK)], axis=1)
        vtc = jnp.concatenate([vct_ref[t, cols, :] for t in range(vct_ref.shape[0])], axis=1)
        o = (_mm(jnp.where(row_mine, vtw, jnp.ones_like(vtw)), p_w)
             + _mm(jnp.where(row_mine, vtc, jnp.ones_like(vtc)), p_c))
        num, den = (o[:NA_DH], o[NA_DH:]) if h % 2 == 0 else (o[NA_DH:], o[:NA_DH])
        return num / den

    n_units = n_blocks * n_heads
    s = {0: scores(0), 1: scores(1)}
    p = {0: weights(*s.pop(0))}
    outs = []
    for u in range(n_units):
        if u + 2 < n_units:
            s[u + 2] = scores(u + 2)
        if u + 1 < n_units:
            p[u + 1] = weights(*s.pop(u + 1))
        outs.append(values(u, *p.pop(u)))
        if len(outs) == n_heads:
            j = u // n_heads
            o_ref[0, j * NA_BLOCK:(j + 1) * NA_BLOCK, :] = jnp.concatenate(outs, axis=0).T.astype(o_ref.dtype)
            outs = []


def _na(qk, vt, qk_c, vt_c, bias_cols, na_w):
    b, s, _ = qk.shape
    l = qk_c.shape[1]
    rows = s // GRID_W
    tq = NA_STEP_BLOCKS * NA_BLOCK
    n_kinds = len(_na_row_plan())
    return pl.pallas_call(
        functools.partial(_na_kernel, rows),
        grid=(b, s // tq),
        in_specs=[pl.BlockSpec((1, tq, na_w), lambda bi, rb: (bi, rb, 0)),
                  pl.BlockSpec((1, s, na_w), lambda bi, rb: (bi, 0, 1)),
                  pl.BlockSpec((s // NA_BLOCK, na_w, NA_BLOCK), lambda bi, rb: (bi, 0, 0)),
                  pl.BlockSpec((1, l, na_w), lambda bi, rb: (bi, 0, 1)),
                  pl.BlockSpec((l // NA_BLOCK, na_w, NA_BLOCK), lambda bi, rb: (bi, 0, 0)),
                  _const_spec(bias_cols.shape)],
        out_specs=pl.BlockSpec((1, tq, na_w), lambda bi, rb: (bi, rb, 0)),
        out_shape=jax.ShapeDtypeStruct((b, s, na_w), _MXU_DTYPE),
        scratch_shapes=[pltpu.VMEM((n_kinds, na_w // NA_DH, NA_KROWS * GRID_W, NA_BLOCK), jnp.float32)],
        compiler_params=pltpu.CompilerParams(
            dimension_semantics=("arbitrary", "arbitrary"),
            vmem_limit_bytes=VMEM_LIMIT_BYTES),
        name="na",
    )(qk, qk, vt, qk_c, vt_c, bias_cols)


def _out_ffn_kernel(alpha, x1_ref, yna_ref, ys5_ref, mod_a_ref, mod_b_ref, lng_ref, lnb_ref, wglu_ref, bglu_ref,
                    wout_ref, wup_ref, wdn_ref, o_ref, p_s, y_s, h_s, x2_s):
    @pl.when(pl.program_id(0) == 0)
    def _():
        h_s[...] = jnp.zeros_like(h_s)
        x2_s[...] = jnp.zeros_like(x2_s)

    mod_a = mod_a_ref[0]
    na_w = yna_ref.shape[1]
    stage_a = {}

    def relayout():
        _chunks_to_tokens(ys5_ref, y_s)

    def gated():
        gl = jax.nn.gelu(jnp.concatenate([y_s[v] for v in range(y_s.shape[0])], axis=1))
        gate = jax.nn.sigmoid(_mm(gl.astype(p_s.dtype), wglu_ref[...]) + bglu_ref[...])
        stage_a["glu"] = (gl * gate).astype(p_s.dtype)

    def mixed():
        y = _mm(yna_ref[...], wout_ref[:na_w, :]) + _mm(stage_a["glu"], wout_ref[na_w:, :])
        x2 = _layer_norm(alpha * x1_ref[...] + mod_a[5:6] * y, lng_ref[1:2], lnb_ref[1:2])
        stage_a["x2"] = x2
        stage_a["h"] = (x2 * (1.0 + mod_a[7:8]) + mod_a[6:7]).astype(h_s.dtype)

    mod_b = mod_b_ref[0]
    _swiglu_hidden(h_s[...], wup_ref, p_s, between=(relayout, gated, mixed))
    n_blk = o_ref.shape[0] // NA_BLOCK
    fs = [_mm(p_s[t * NA_BLOCK:(t + 1) * NA_BLOCK, :], wdn_ref[...]) for t in range(n_blk)]
    for t in range(n_blk):
        rows = slice(t * NA_BLOCK, (t + 1) * NA_BLOCK)
        o_ref[rows, :] = _layer_norm(alpha * x2_s[rows, :] + (0.5 * mod_b[8:9]) * fs[t], lng_ref[2:3], lnb_ref[2:3])

    x2_s[...] = stage_a["x2"]
    h_s[...] = stage_a["h"]


def _out_ffn(x1, yna, ys5, mod, tiles_per_mod, lng, lnb, wglu, bglu, wout, wup, wdn, alpha, tm):
    t, d = x1.shape
    ff = wdn.shape[0]
    na_w = yna.shape[1]
    groups, _, cw = ys5.shape
    s5_w = groups * S5_CG
    n = t // tm

    def cur(i):
        return jnp.minimum(i, n - 1)

    def prev(i):
        return jnp.maximum(i - 1, 0)

    return pl.pallas_call(
        functools.partial(_out_ffn_kernel, alpha),
        grid=(n + 1,),
        in_specs=[pl.BlockSpec((tm, d), lambda i: (cur(i), 0)),
                  pl.BlockSpec((tm, na_w), lambda i: (cur(i), 0)),
                  pl.BlockSpec((groups, tm // CHUNK, cw), lambda i: (0, cur(i), 0)),
                  pl.BlockSpec((1, N_MOD, d), lambda i: (cur(i) // tiles_per_mod, 0, 0)),
                  pl.BlockSpec((1, N_MOD, d), lambda i: (prev(i) // tiles_per_mod, 0, 0)),
                  _const_spec(lng.shape), _const_spec(lnb.shape),
                  _const_spec(wglu.shape), _const_spec(bglu.shape), _const_spec(wout.shape),
                  _const_spec(wup.shape), _const_spec(wdn.shape)],
        out_specs=pl.BlockSpec((tm, d), lambda i: (prev(i), 0)),
        out_shape=jax.ShapeDtypeStruct((t, d), jnp.float32),
        scratch_shapes=[pltpu.VMEM((tm, ff), _MXU_DTYPE), pltpu.VMEM((s5_w // LANES, tm, LANES), jnp.float32),
                        pltpu.VMEM((tm, d), _MXU_DTYPE), pltpu.VMEM((tm, d), jnp.float32)],
        compiler_params=pltpu.CompilerParams(dimension_semantics=("arbitrary",),
                                             vmem_limit_bytes=VMEM_LIMIT_BYTES),
        name="out_ffn",
    )(x1, yna, ys5, mod, mod, lng, lnb, wglu, bglu, wout, wup, wdn)


def _pick_tile(n, target):
    t = min(n, target)
    while n % t:
        t //= 2
    return t


def kernel(x, c, ctx, c_ctx, w_ada, b_ada, ln_g, ln_b, ffn1_w_up, ffn1_w_down, w_in, na_rpb, s5_a_re, s5_a_im, s5_log_dt, s5_b_re, s5_b_im, s5_c_re, s5_c_im, s5_d, s5_w_glu, s5_b_glu, w_out, ffn2_w_up, ffn2_w_down):
    depth = w_ada.shape[0]
    assert depth == 1, "single-layer stack only"
    bsz, seq, d = x.shape
    l_ctx = ctx.shape[1]
    s5_w = s5_w_glu.shape[1]
    na_w = w_out.shape[1] - s5_w
    groups = s5_w // S5_CG
    rows = seq // GRID_W
    assert bsz % 8 == 0 and seq % (NA_STEP_BLOCKS * NA_BLOCK) == 0 and rows >= NA_KROWS
    assert l_ctx % NA_BLOCK == 0 and groups % 2 == 0 and s5_a_re.shape[3] == S5_P and 2 * S5_P == LANES
    assert na_w % (2 * NA_DH) == 0 and ffn1_w_down.shape[1] % MXU_TILE == 0
    alpha = (2.0 * depth) ** 0.25
    cd = _MXU_DTYPE

    n_rows = -(-(bsz + 1) // 8) * 8
    cc = jnp.zeros((n_rows, d), jnp.float32).at[:bsz].set(c).at[bsz].set(c_ctx)
    n_ada = w_ada.shape[2]
    mod = _ada(cc, w_ada[0], b_ada[0][None, :], _pick_tile(n_ada, 1152)).reshape(n_rows, N_MOD, d)

    lng, lnb = ln_g[0], ln_b[0]
    wup1, wdn1 = ffn1_w_up[0].astype(cd), ffn1_w_down[0].astype(cd)
    win = w_in[0].astype(cd)

    tm = _pick_tile(seq, 512)
    x1, qk, vt, xs = _ffn_in(x.reshape(bsz * seq, d), mod, seq // tm, lng, lnb, wup1, wdn1, win, alpha, na_w, tm)
    tmc = _pick_tile(bsz * l_ctx, 512)
    _, qk_c, vt_c, xs_c = _ffn_in(ctx.reshape(bsz * l_ctx, d), mod[bsz:bsz + 1], None, lng, lnb, wup1, wdn1, win,
                                  alpha, na_w, tmc)

    tables = _s5_tables(s5_a_re[0], s5_a_im[0], s5_log_dt[0], s5_b_re[0], s5_b_im[0], s5_c_re[0], s5_c_im[0],
                        s5_d[0])
    y_s5, wup2, wdn2, wout, wglu = _s5(xs, xs_c, *tables, bsz,
                                       (ffn2_w_up[0], ffn2_w_down[0], w_out[0], s5_w_glu[0]))

    y_na = _na(qk.reshape(bsz, seq, 2 * na_w), vt, qk_c.reshape(bsz, l_ctx, 2 * na_w), vt_c, _na_bias_columns(na_rpb[0]),
               na_w).reshape(bsz * seq, na_w)

    out = _out_ffn(x1, y_na, y_s5, mod, seq // tm, lng, lnb, wglu, s5_b_glu[0][None, :], wout, wup2, wdn2,
                   alpha, tm)
    return out.reshape(bsz, seq, d)
```

```python
import functools

import jax
import jax.numpy as jnp
import numpy as np
from jax import lax
from jax.experimental import pallas as pl
from jax.experimental.pallas import tpu as pltpu

GRID_W = 64
NA_DH = 64
NA_KR = 8
NA_KC = 16
S5_CG = 16
S5_P = 64
LN_EPS = 1e-6
N_MOD = 9

CHUNK = 16
NA_QROWS = 4
NA_KROWS = 12
NA_BLOCK = NA_QROWS * GRID_W
NA_STEP_BLOCKS = 4
MASK_VALUE = -1e30
LOG2_E = 1.4426950408889634
SCAN_UNROLL = 4

MXU_TILE = 256
LANES = 128
SUBLANES = 8
VMEM_LIMIT_BYTES = 56 * 1024 * 1024

_MXU_DTYPE = jnp.bfloat16
_HI = lax.Precision.HIGHEST


def _mm(a, b):
    return jnp.dot(a, b, preferred_element_type=jnp.float32)


def _mm_nt(a, b):
    return lax.dot_general(a, b, (((1,), (1,)), ((), ())), preferred_element_type=jnp.float32)


def _const_spec(shape):
    nd = len(shape)
    return pl.BlockSpec(shape, lambda *_: (0,) * nd, pipeline_mode=pl.Buffered(1))


def _layer_norm(r, g, b):
    mu = jnp.mean(r, axis=-1, keepdims=True)
    d = r - mu
    var = jnp.mean(d * d, axis=-1, keepdims=True)
    return d * lax.rsqrt(var + LN_EPS) * g + b


def _swiglu_hidden(h, wup_ref, p_s, between=()):
    ff = p_s.shape[1]
    n_chunks = ff // MXU_TILE
    slots = [(k + 1) * n_chunks // (len(between) + 1) for k in range(len(between))]
    for j in range(n_chunks):
        for slot, fn in zip(slots, between):
            if slot == j:
                fn()
        lo = j * MXU_TILE
        a = _mm(h, wup_ref[:, lo:lo + MXU_TILE])
        g = _mm(h, wup_ref[:, ff + lo:ff + lo + MXU_TILE])
        p_s[:, lo:lo + MXU_TILE] = (g * jax.nn.sigmoid(g) * a).astype(p_s.dtype)


def _swiglu(h, wup_ref, wdn_ref, p_s, between=()):
    _swiglu_hidden(h, wup_ref, p_s, between)
    return _mm(p_s[...], wdn_ref[...])


def _block_transpose(tiles):
    per = len(tiles)
    blk = lax.broadcasted_iota(jnp.int32, (1, LANES), 1) // S5_CG
    tiles = list(tiles)
    d = per // 2
    while d:
        upper = (blk & d) != 0
        for a in range(per):
            if a & d:
                continue
            lo, hi = tiles[a], tiles[a + d]
            tiles[a] = jnp.where(upper, pltpu.roll(hi, S5_CG * d, axis=1), lo)
            tiles[a + d] = jnp.where(upper, hi, pltpu.roll(lo, LANES - S5_CG * d, axis=1))
        d //= 2
    return tiles


def _tokens_to_chunks(u_s, xs_ref, row0, n_rows):
    n_ch = n_rows // CHUNK
    ch0 = row0 // CHUNK
    per = LANES // S5_CG
    for v in range(u_s.shape[0]):
        for w in range(CHUNK // per):
            by_group = _block_transpose(
                [u_s[v, pl.ds(row0 + per * w + jj, n_ch, stride=CHUNK), :] for jj in range(per)])
            for gg in range(per):
                xs_ref[per * v + gg, ch0:ch0 + n_ch, w * LANES:(w + 1) * LANES] = by_group[gg].astype(xs_ref.dtype)


def _chunks_to_tokens(ys_ref, y_s):
    n_ch = y_s.shape[1] // CHUNK
    per = LANES // S5_CG
    for v in range(y_s.shape[0]):
        for w in range(CHUNK // per):
            by_phase = _block_transpose([ys_ref[per * v + gg, :, w * LANES:(w + 1) * LANES] for gg in range(per)])
            for jj in range(per):
                y_s[v, pl.ds(per * w + jj, n_ch, stride=CHUNK), :] = by_phase[jj]


def _ada_kernel(c_ref, w_ref, b_ref, o_ref):
    cc = c_ref[...]
    s = cc * jax.nn.sigmoid(cc)
    w = w_ref[...]
    s_hi = s.astype(_MXU_DTYPE)
    s_lo = (s - s_hi.astype(jnp.float32)).astype(_MXU_DTYPE)
    w_hi = w.astype(_MXU_DTYPE)
    w_lo = (w - w_hi.astype(jnp.float32)).astype(_MXU_DTYPE)
    o_ref[...] = _mm(s_hi, w_hi) + (_mm(s_lo, w_hi) + _mm(s_hi, w_lo)) + b_ref[...]


def _ada(cc, w, b, bn):
    rows, d = cc.shape
    n = w.shape[1]
    return pl.pallas_call(
        _ada_kernel,
        grid=(n // bn,),
        in_specs=[pl.BlockSpec((rows, d), lambda j: (0, 0)),
                  pl.BlockSpec((d, bn), lambda j: (0, j)),
                  pl.BlockSpec((1, bn), lambda j: (0, j))],
        out_specs=pl.BlockSpec((rows, bn), lambda j: (0, j)),
        out_shape=jax.ShapeDtypeStruct((rows, n), jnp.float32),
        compiler_params=pltpu.CompilerParams(dimension_semantics=("arbitrary",),
                                             vmem_limit_bytes=VMEM_LIMIT_BYTES),
        name="ada",
    )(cc, w, b)


def _ffn_in_kernel(alpha, na_w, x_ref, mod_ref, lng_ref, lnb_ref, wup_ref, wdn_ref, win_ref,
                   x1_ref, qk_ref, vt_ref, xs_ref, p_s, u_s):
    mod = mod_ref[0]
    h = (x_ref[...] * (1.0 + mod[1:2]) + mod[0:1]).astype(p_s.dtype)
    _swiglu_hidden(h, wup_ref, p_s)
    n_blk = vt_ref.shape[0]
    ys = [_mm(p_s[t * NA_BLOCK:(t + 1) * NA_BLOCK, :], wdn_ref[...]) for t in range(n_blk)]
    for t in range(n_blk):
        rows = slice(t * NA_BLOCK, (t + 1) * NA_BLOCK)
        x1 = _layer_norm(alpha * x_ref[rows, :] + (0.5 * mod[2:3]) * ys[t], lng_ref[0:1], lnb_ref[0:1])
        x1_ref[rows, :] = x1
        h2 = (x1 * (1.0 + mod[4:5]) + mod[3:4]).astype(p_s.dtype)
        pu = _mm(h2, win_ref[:, 3 * na_w:])
        for v in range(u_s.shape[0]):
            u_s[v, rows, :] = pu[:, v * LANES:(v + 1) * LANES]
        _tokens_to_chunks(u_s, xs_ref, t * NA_BLOCK, NA_BLOCK)
        pr = _mm(h2, win_ref[:, :3 * na_w])
        qk_ref[rows, :na_w] = (pr[:, :na_w] * (NA_DH ** -0.5 * LOG2_E)).astype(qk_ref.dtype)
        qk_ref[rows, na_w:] = pr[:, na_w:2 * na_w].astype(qk_ref.dtype)
        vt_ref[t] = pr[:, 2 * na_w:].T.astype(vt_ref.dtype)


def _ffn_in(x2d, mod, tiles_per_mod, lng, lnb, wup, wdn, win, alpha, na_w, tm):
    t, d = x2d.shape
    ff = wdn.shape[0]
    ncol = win.shape[1]
    s5_w = ncol - 3 * na_w
    groups, cw = s5_w // S5_CG, CHUNK * S5_CG
    if tiles_per_mod is None:
        mod_map = lambda i: (0, 0, 0)
    else:
        mod_map = lambda i: (i // tiles_per_mod, 0, 0)
    return pl.pallas_call(
        functools.partial(_ffn_in_kernel, alpha, na_w),
        grid=(t // tm,),
        in_specs=[pl.BlockSpec((tm, d), lambda i: (i, 0)),
                  pl.BlockSpec((1, N_MOD, d), mod_map),
                  _const_spec(lng.shape), _const_spec(lnb.shape),
                  _const_spec(wup.shape), _const_spec(wdn.shape), _const_spec(win.shape)],
        out_specs=[pl.BlockSpec((tm, d), lambda i: (i, 0)),
                   pl.BlockSpec((tm, 2 * na_w), lambda i: (i, 0)),
                   pl.BlockSpec((tm // NA_BLOCK, na_w, NA_BLOCK), lambda i: (i, 0, 0)),
                   pl.BlockSpec((groups, tm // CHUNK, cw), lambda i: (0, i, 0))],
        out_shape=[jax.ShapeDtypeStruct((t, d), jnp.float32),
                   jax.ShapeDtypeStruct((t, 2 * na_w), _MXU_DTYPE),
                   jax.ShapeDtypeStruct((t // NA_BLOCK, na_w, NA_BLOCK), _MXU_DTYPE),
                   jax.ShapeDtypeStruct((groups, t // CHUNK, cw), jnp.float32)],
        scratch_shapes=[pltpu.VMEM((tm, ff), _MXU_DTYPE), pltpu.VMEM((s5_w // LANES, tm, LANES), jnp.float32)],
        compiler_params=pltpu.CompilerParams(dimension_semantics=("arbitrary",),
                                             vmem_limit_bytes=VMEM_LIMIT_BYTES),
        name="ffn_in",
    )(x2d, mod, lng, lnb, wup, wdn, win)


def _toeplitz_kernel(lhs_ref, rhs_ref, o_ref):
    width = o_ref.shape[2]
    lane = lax.broadcasted_iota(jnp.int32, (1, width), 1)
    for g in range(o_ref.shape[0]):
        k_f = jnp.dot(lhs_ref[0, g], rhs_ref[0, g], precision=_HI, preferred_element_type=jnp.float32)
        k_b = jnp.dot(lhs_ref[1, g], rhs_ref[1, g], precision=_HI, preferred_element_type=jnp.float32)
        for i in range(CHUNK):
            lo, hi = i * S5_CG, (i + 1) * S5_CG
            fwd = k_f if lo == 0 else jnp.where(lane >= lo, pltpu.roll(k_f, lo, axis=1), 0.0)
            bwd = k_b if hi == width else jnp.where(lane < hi, pltpu.roll(k_b, hi, axis=1), 0.0)
            o_ref[g, lo:hi, :] = (fwd + bwd).astype(o_ref.dtype)


def _toeplitz(lhs, rhs):
    _, g, cg, kdim = lhs.shape
    width = rhs.shape[3]
    gb = _pick_tile(g, 8)
    return pl.pallas_call(
        _toeplitz_kernel,
        grid=(g // gb,),
        in_specs=[pl.BlockSpec((2, gb, cg, kdim), lambda k: (0, k, 0, 0)),
                  pl.BlockSpec((2, gb, kdim, width), lambda k: (0, k, 0, 0))],
        out_specs=pl.BlockSpec((gb, width, width), lambda k: (k, 0, 0)),
        out_shape=jax.ShapeDtypeStruct((g, width, width), _MXU_DTYPE),
        compiler_params=pltpu.CompilerParams(dimension_semantics=("arbitrary",)),
        name="s5_toeplitz",
    )(lhs, rhs)


def _s5_tables(a_re, a_im, log_dt, b_re, b_im, c_re, c_im, d_skip):
    g, p = a_re.shape[1], a_re.shape[2]
    f32 = jnp.float32
    a_re, a_im = a_re.astype(f32), a_im.astype(f32)
    dt = jnp.exp(log_dt.astype(f32))[..., None]
    lr, li = a_re * dt, a_im * dt

    width = CHUNK * S5_CG
    asc = np.arange(CHUNK)
    desc = asc[::-1].copy()
    lane_m, lane_c = np.arange(width) // S5_CG, np.arange(width) % S5_CG

    def powers(expo, lr_b, li_b):
        e = jnp.asarray(expo, f32)
        mag = jnp.exp(e * lr_b)
        return mag * jnp.cos(e * li_b), mag * jnp.sin(e * li_b)


    ab_r, ab_i = jnp.exp(lr) * jnp.cos(li), jnp.exp(lr) * jnp.sin(li)
    den = a_re ** 2 + a_im ** 2
    nr, ni = ab_r - 1.0, ab_i
    sr = ((nr * a_re + ni * a_im) / den)[:, :, None, :]
    si = ((ni * a_re - nr * a_im) / den)[:, :, None, :]
    bt_re, bt_im = jnp.swapaxes(b_re, 2, 3).astype(f32), jnp.swapaxes(b_im, 2, 3).astype(f32)
    bt_r = sr * bt_re - si * bt_im
    bt_i = sr * bt_im + si * bt_re

    tile_c = jnp.asarray(lane_c[None, :] == np.arange(S5_CG)[:, None], f32)
    ct_r = jnp.einsum('dgcp,cl->dgpl', c_re.astype(f32), tile_c, precision=_HI)
    ct_i = jnp.einsum('dgcp,cl->dgpl', c_im.astype(f32), tile_c, precision=_HI)

    rep_m = jnp.asarray(lane_m[None, :] == np.arange(CHUNK)[:, None], f32)

    def by_lane(expo):
        pw = powers(np.stack(expo)[:, None, None, :], lr[..., None], li[..., None])
        return tuple(jnp.einsum('dgpm,ml->dgpl', v, rep_m, precision=_HI) for v in pw)

    q_r, q_i = by_lane([asc, desc])
    rhs = jnp.concatenate([ct_r * q_r - ct_i * q_i, ct_r * q_i + ct_i * q_r], axis=2)
    toep = _toeplitz(jnp.concatenate([bt_r, -bt_i], axis=3), rhs)

    v_r, v_i = by_lane([asc + 1, desc + 1])
    cs_r = ct_r * v_r - ct_i * v_i
    cs_i = -(ct_r * v_i + ct_i * v_r)

    w_r, w_i = (jnp.repeat(v, S5_CG, axis=2) for v in
                powers(np.stack([desc, asc])[:, None, :, None], lr[:, :, None, :], li[:, :, None, :]))
    rows_r, rows_i = jnp.tile(bt_r, (1, 1, CHUNK, 1)), jnp.tile(bt_i, (1, 1, CHUNK, 1))
    bs_r = w_r * rows_r - w_i * rows_i
    bs_i = w_r * rows_i + w_i * rows_r
    even = jnp.asarray((np.arange(g) % 2) == 0)[None, :, None, None]
    bs = jnp.concatenate([jnp.where(even, bs_r, 0.0), jnp.where(even, 0.0, bs_r),
                          jnp.where(even, bs_i, 0.0), jnp.where(even, 0.0, bs_i)], axis=3)
    cs = jnp.concatenate([jnp.where(even, cs_r, 0.0), jnp.where(even, 0.0, cs_r),
                          jnp.where(even, cs_i, 0.0), jnp.where(even, 0.0, cs_i)], axis=2)

    e_r, e_i = jnp.exp(CHUNK * lr), CHUNK * li
    half = 2 * p
    a_pow = jnp.stack([(e_r * jnp.cos(e_i)).reshape(2, g // 2, half),
                       (e_r * jnp.sin(e_i)).reshape(2, g // 2, half)], axis=2)
    dvec = jnp.dot(d_skip.astype(f32), tile_c, precision=_HI)[:, None, :]
    return toep, bs.astype(_MXU_DTYPE), cs.astype(_MXU_DTYPE), a_pow, dvec


def _row_pitch(n):
    return n + (4 - n) % 8


def _s5_kernel(nb, n_cast, xl_ref, xc_ref, toep_ref, bs_ref, cs_ref, apow_ref, dvec_ref, *refs):
    cast_in, (y_ref, *cast_out), (zc_s, zl_s, h_s) = refs[:n_cast], refs[n_cast:2 * n_cast + 1], refs[2 * n_cast + 1:]
    for src, dst in zip(cast_in, cast_out):
        dst[...] = src[...].astype(dst.dtype)
    rows_l = xl_ref.shape[1]
    rows_c = xc_ref.shape[1]
    n_l, n_c = rows_l // nb, rows_c // nb
    pitch_l, pitch_c = _row_pitch(n_l), _row_pitch(n_c)
    half = zl_s.shape[3]
    xl = [xl_ref[g] for g in range(2)]
    xlb = [v.astype(_MXU_DTYPE) for v in xl]
    xcb = [xc_ref[g].astype(_MXU_DTYPE) for g in range(2)]

    for d in range(2):
        zc = _mm(xcb[0], bs_ref[d, 0]) + _mm(xcb[1], bs_ref[d, 1])
        zl = _mm(xlb[0], bs_ref[d, 0]) + _mm(xlb[1], bs_ref[d, 1])
        for ri in range(2):
            for b in range(nb):
                zc_s[d, ri, b * pitch_c:b * pitch_c + n_c, :] = zc[b * n_c:(b + 1) * n_c, ri * half:(ri + 1) * half]
                zl_s[d, ri, b * pitch_l:b * pitch_l + n_l, :] = zl[b * n_l:(b + 1) * n_l, ri * half:(ri + 1) * half]

    a_r = [jnp.broadcast_to(apow_ref[d, 0, 0:1, :], (nb, half)) for d in range(2)]
    a_i = [jnp.broadcast_to(apow_ref[d, 0, 1:2, :], (nb, half)) for d in range(2)]

    def step(d, n, latent, h_r, h_i):
        if latent:
            h_s[d, 0, pl.ds(n, nb, stride=pitch_l), :] = h_r
            h_s[d, 1, pl.ds(n, nb, stride=pitch_l), :] = h_i
            z_r = zl_s[d, 0, pl.ds(n, nb, stride=pitch_l), :]
            z_i = zl_s[d, 1, pl.ds(n, nb, stride=pitch_l), :]
        else:
            z_r = zc_s[d, 0, pl.ds(n, nb, stride=pitch_c), :]
            z_i = zc_s[d, 1, pl.ds(n, nb, stride=pitch_c), :]
        return (a_r[d] * h_r - a_i[d] * h_i + z_r,
                a_r[d] * h_i + a_i[d] * h_r + z_i)

    def ctx_body(n, carry):
        f_r, f_i, b_r, b_i = carry
        f_r, f_i = step(0, n, False, f_r, f_i)
        b_r, b_i = step(1, n_c - 1 - n, False, b_r, b_i)
        return f_r, f_i, b_r, b_i

    def lat_body(n, carry):
        f_r, f_i, b_r, b_i = carry
        f_r, f_i = step(0, n, True, f_r, f_i)
        b_r, b_i = step(1, n_l - 1 - n, True, b_r, b_i)
        return f_r, f_i, b_r, b_i

    zero = jnp.zeros((nb, half), jnp.float32)
    carry = lax.fori_loop(0, n_c, ctx_body, (zero, zero, zero, zero), unroll=SCAN_UNROLL)
    lax.fori_loop(0, n_l, lat_body, carry, unroll=SCAN_UNROLL)

    def plane(d, ri):
        return jnp.concatenate([h_s[d, ri, b * pitch_l:b * pitch_l + n_l, :] for b in range(nb)], axis=0)

    hb = [jnp.concatenate([plane(d, 0), plane(d, 1)], axis=1).astype(_MXU_DTYPE) for d in range(2)]
    for g in range(2):
        y = _mm(xlb[g], toep_ref[g]) + _mm(hb[0], cs_ref[0, g]) + _mm(hb[1], cs_ref[1, g])
        y_ref[g] = y + xl[g] * dvec_ref[g]


def _s5(xl, xc, toep, bs, cs, a_pow, dvec, nb, to_cast):
    g, rows_l, w = xl.shape
    rows_c = xc.shape[1]
    sw = bs.shape[3]
    steps = g // 2
    slabs = [m.shape[0] // steps for m in to_cast]
    assert all(m.shape[0] == s * steps and s % (2 * SUBLANES) == 0 for m, s in zip(to_cast, slabs))
    cast_specs = [pl.BlockSpec((s, m.shape[1]), lambda k: (k, 0)) for m, s in zip(to_cast, slabs)]
    return pl.pallas_call(
        functools.partial(_s5_kernel, nb, len(to_cast)),
        grid=(steps,),
        in_specs=[pl.BlockSpec((2, rows_l, w), lambda k: (k, 0, 0)),
                  pl.BlockSpec((2, rows_c, w), lambda k: (k, 0, 0)),
                  pl.BlockSpec((2, w, w), lambda k: (k, 0, 0)),
                  pl.BlockSpec((2, 2, w, sw), lambda k: (0, k, 0, 0)),
                  pl.BlockSpec((2, 2, sw, w), lambda k: (0, k, 0, 0)),
                  pl.BlockSpec((2, 1, 2, sw // 2), lambda k: (0, k, 0, 0)),
                  pl.BlockSpec((2, 1, w), lambda k: (k, 0, 0))] + cast_specs,
        out_specs=[pl.BlockSpec((2, rows_l, w), lambda k: (k, 0, 0))] + cast_specs,
        out_shape=[jax.ShapeDtypeStruct((g, rows_l, w), jnp.float32)]
        + [jax.ShapeDtypeStruct(m.shape, _MXU_DTYPE) for m in to_cast],
        scratch_shapes=[pltpu.VMEM((2, 2, nb * _row_pitch(rows_c // nb), sw // 2), jnp.float32),
                        pltpu.VMEM((2, 2, nb * _row_pitch(rows_l // nb), sw // 2), jnp.float32),
                        pltpu.VMEM((2, 2, nb * _row_pitch(rows_l // nb), sw // 2), jnp.float32)],
        compiler_params=pltpu.CompilerParams(dimension_semantics=("arbitrary",),
                                             vmem_limit_bytes=VMEM_LIMIT_BYTES),
        name="s5",
    )(xl, xc, toep, bs, cs, a_pow, dvec, *to_cast)


def _na_row_plan():
    last_q = NA_KROWS - NA_QROWS
    plan = []
    for r0s, qrs in ((lambda lr: 0, lambda lr: lr),
                     (lambda lr: lr, lambda lr: NA_KR // 2 + lr),
                     (lambda lr: last_q - NA_KR // 2, lambda lr: last_q + lr)):
        plan.append([[kl - qrs(lr) + (NA_KR - 1) if r0s(lr) <= kl < r0s(lr) + NA_KR else None
                      for kl in range(NA_KROWS)] for lr in range(NA_QROWS)])
    return plan


def _na_bias_fill(col_ref, bias_s):
    plan = _na_row_plan()
    lane = lax.broadcasted_iota(jnp.int32, (1, 2 * GRID_W), 1)
    masked = jnp.full((GRID_W, 2 * GRID_W), MASK_VALUE, jnp.float32)
    for head in range(bias_s.shape[1]):
        for kind in range(len(plan)):
            for kl in range(NA_KROWS):
                for lp in range(NA_QROWS // 2):
                    a0, a1 = plan[kind][2 * lp][kl], plan[kind][2 * lp + 1][kl]
                    t0 = masked if a0 is None else col_ref[head, a0]
                    t1 = masked if a1 is None else col_ref[head, a1]
                    tile = t0 if a0 == a1 else jnp.where(lane < GRID_W, t0, t1)
                    bias_s[kind, head, kl * GRID_W:(kl + 1) * GRID_W, lp * 2 * GRID_W:(lp + 1) * 2 * GRID_W] = tile


def _na_bias_columns(rpb):
    kc = np.arange(GRID_W)[:, None]
    qc = np.arange(GRID_W)[None, :]
    c0 = np.clip(qc - NA_KC // 2, 0, GRID_W - NA_KC)
    col_ok = (kc >= c0) & (kc < c0 + NA_KC)
    sel_c = ((kc - qc + (NA_KC - 1))[:, :, None] == np.arange(2 * NA_KC - 1)) & col_ok[:, :, None]
    by_col = jnp.einsum('hab,kqb->hakq', rpb.astype(jnp.float32), jnp.asarray(sel_c, jnp.float32), precision=_HI)
    by_col = jnp.where(jnp.asarray(col_ok), by_col * LOG2_E, MASK_VALUE)
    return jnp.concatenate([by_col, by_col], axis=3)


def _row_max(x):
    while x.shape[0] % (2 * SUBLANES) == 0:
        half = x.shape[0] // 2
        x = jnp.maximum(x[:half], x[half:])
    return jnp.max(x, axis=0, keepdims=True)


def _na_kernel(rows, q_ref, k_ref, vt_ref, kc_ref, vct_ref, col_ref, o_ref, bias_ref):
    @pl.when((pl.program_id(0) == 0) & (pl.program_id(1) == 0))
    def _():
        _na_bias_fill(col_ref, bias_ref)

    n_rb = rows // NA_QROWS
    nk = NA_KROWS * GRID_W
    pair = 2 * NA_DH
    lane_first = lax.broadcasted_iota(jnp.int32, (1, pair), 1) < NA_DH
    row_first = lax.broadcasted_iota(jnp.int32, (pair, 1), 0) < NA_DH
    n_heads = q_ref.shape[2] // NA_DH
    n_blocks = q_ref.shape[1] // NA_BLOCK

    def block(j):
        rb = pl.program_id(1) * n_blocks + j
        kind = jnp.where(rb == 0, 0, jnp.where(rb == n_rb - 1, 2, 1))
        first_row = jnp.clip(rb * NA_QROWS - NA_KR // 2, 0, rows - NA_KROWS)
        return kind, pl.multiple_of(first_row * GRID_W, GRID_W), first_row // NA_QROWS

    geometry = [block(j) for j in range(n_blocks)]

    def scores(u):
        j, h = divmod(u, n_heads)
        kind, start, _ = geometry[j]
        cols = slice(h // 2 * pair, (h // 2 + 1) * pair)
        lane_mine = lane_first if h % 2 == 0 else jnp.logical_not(lane_first)
        qm = jnp.where(lane_mine, q_ref[0, j * NA_BLOCK:(j + 1) * NA_BLOCK, cols],
                       jnp.zeros((NA_BLOCK, pair), q_ref.dtype))
        s_w = _mm_nt(k_ref[0, pl.ds(start, nk), cols], qm) + bias_ref[kind, h]
        s_c = _mm_nt(kc_ref[0, :, cols], qm)
        return s_w, s_c

    def weights(s_w, s_c):
        mx = jnp.maximum(_row_max(s_w), _row_max(s_c))
        return jnp.exp2((s_w - mx).astype(vt_ref.dtype)), jnp.exp2((s_c - mx).astype(vt_ref.dtype))

    def values(u, p_w, p_c):
        j, h = divmod(u, n_heads)
        blk0 = geometry[j][2]
        cols = slice(h // 2 * pair, (h // 2 + 1) * pair)
        row_mine = row_first if h % 2 == 0 else jnp.logical_not(row_first)
        vtw = jnp.concatenate([vt_ref[blk0 + t, cols, :] for t in range(nk // NA_BLOCK)], axis=1)
        vtc = jnp.concatenate([vct_ref[t, cols, :] for t in range(vct_ref.shape[0])], axis=1)
        o = (_mm(jnp.where(row_mine, vtw, jnp.ones_like(vtw)), p_w)
             + _mm(jnp.where(row_mine, vtc, jnp.ones_like(vtc)), p_c))
        num, den = (o[:NA_DH], o[NA_DH:]) if h % 2 == 0 else (o[NA_DH:], o[:NA_DH])
        return num / den

    n_units = n_blocks * n_heads
    s = {0: scores(0), 1: scores(1)}
    p = {0: weights(*s.pop(0))}
    outs = []
    for u in range(n_units):
        if u + 2 < n_units:
            s[u + 2] = scores(u + 2)
        if u + 1 < n_units:
            p[u + 1] = weights(*s.pop(u + 1))
        outs.append(values(u, *p.pop(u)))
        if len(outs) == n_heads:
            j = u // n_heads
            o_ref[0, j * NA_BLOCK:(j + 1) * NA_BLOCK, :] = jnp.concatenate(outs, axis=0).T.astype(o_ref.dtype)
            outs = []


def _na(qk, vt, qk_c, vt_c, bias_cols, na_w):
    b, s, _ = qk.shape
    l = qk_c.shape[1]
    rows = s // GRID_W
    tq = NA_STEP_BLOCKS * NA_BLOCK
    n_kinds = len(_na_row_plan())
    return pl.pallas_call(
        functools.partial(_na_kernel, rows),
        grid=(b, s // tq),
        in_specs=[pl.BlockSpec((1, tq, na_w), lambda bi, rb: (bi, rb, 0)),
                  pl.BlockSpec((1, s, na_w), lambda bi, rb: (bi, 0, 1)),
                  pl.BlockSpec((s // NA_BLOCK, na_w, NA_BLOCK), lambda bi, rb: (bi, 0, 0)),
                  pl.BlockSpec((1, l, na_w), lambda bi, rb: (bi, 0, 1)),
                  pl.BlockSpec((l // NA_BLOCK, na_w, NA_BLOCK), lambda bi, rb: (bi, 0, 0)),
                  _const_spec(bias_cols.shape)],
        out_specs=pl.BlockSpec((1, tq, na_w), lambda bi, rb: (bi, rb, 0)),
        out_shape=jax.ShapeDtypeStruct((b, s, na_w), _MXU_DTYPE),
        scratch_shapes=[pltpu.VMEM((n_kinds, na_w // NA_DH, NA_KROWS * GRID_W, NA_BLOCK), jnp.float32)],
        compiler_params=pltpu.CompilerParams(
            dimension_semantics=("arbitrary", "arbitrary"),
            vmem_limit_bytes=VMEM_LIMIT_BYTES),
        name="na",
    )(qk, qk, vt, qk_c, vt_c, bias_cols)


def _out_ffn_kernel(alpha, x1_ref, yna_ref, ys5_ref, mod_a_ref, mod_b_ref, lng_ref, lnb_ref, wglu_ref, bglu_ref,
                    wout_ref, wup_ref, wdn_ref, o_ref, p_s, y_s, h_s, x2_s):
    @pl.when(pl.program_id(0) == 0)
    def _():
        h_s[...] = jnp.zeros_like(h_s)
        x2_s[...] = jnp.zeros_like(x2_s)

    mod_a = mod_a_ref[0]
    na_w = yna_ref.shape[1]
    stage_a = {}

    def relayout():
        _chunks_to_tokens(ys5_ref, y_s)

    def gated():
        gl = jax.nn.gelu(jnp.concatenate([y_s[v] for v in range(y_s.shape[0])], axis=1))
        gate = jax.nn.sigmoid(_mm(gl.astype(p_s.dtype), wglu_ref[...]) + bglu_ref[...])
        stage_a["glu"] = (gl * gate).astype(p_s.dtype)

    def mixed():
        y = _mm(yna_ref[...], wout_ref[:na_w, :]) + _mm(stage_a["glu"], wout_ref[na_w:, :])
        x2 = _layer_norm(alpha * x1_ref[...] + mod_a[5:6] * y, lng_ref[1:2], lnb_ref[1:2])
        stage_a["x2"] = x2
        stage_a["h"] = (x2 * (1.0 + mod_a[7:8]) + mod_a[6:7]).astype(h_s.dtype)

    mod_b = mod_b_ref[0]
    _swiglu_hidden(h_s[...], wup_ref, p_s, between=(relayout, gated, mixed))
    n_blk = o_ref.shape[0] // NA_BLOCK
    fs = [_mm(p_s[t * NA_BLOCK:(t + 1) * NA_BLOCK, :], wdn_ref[...]) for t in range(n_blk)]
    for t in range(n_blk):
        rows = slice(t * NA_BLOCK, (t + 1) * NA_BLOCK)
        o_ref[rows, :] = _layer_norm(alpha * x2_s[rows, :] + (0.5 * mod_b[8:9]) * fs[t], lng_ref[2:3], lnb_ref[2:3])

    x2_s[...] = stage_a["x2"]
    h_s[...] = stage_a["h"]


def _out_ffn(x1, yna, ys5, mod, tiles_per_mod, lng, lnb, wglu, bglu, wout, wup, wdn, alpha, tm):
    t, d = x1.shape
    ff = wdn.shape[0]
    na_w = yna.shape[1]
    groups, _, cw = ys5.shape
    s5_w = groups * S5_CG
    n = t // tm

    def cur(i):
        return jnp.minimum(i, n - 1)

    def prev(i):
        return jnp.maximum(i - 1, 0)

    return pl.pallas_call(
        functools.partial(_out_ffn_kernel, alpha),
        grid=(n + 1,),
        in_specs=[pl.BlockSpec((tm, d), lambda i: (cur(i), 0)),
                  pl.BlockSpec((tm, na_w), lambda i: (cur(i), 0)),
                  pl.BlockSpec((groups, tm // CHUNK, cw), lambda i: (0, cur(i), 0)),
                  pl.BlockSpec((1, N_MOD, d), lambda i: (cur(i) // tiles_per_mod, 0, 0)),
                  pl.BlockSpec((1, N_MOD, d), lambda i: (prev(i) // tiles_per_mod, 0, 0)),
                  _const_spec(lng.shape), _const_spec(lnb.shape),
                  _const_spec(wglu.shape), _const_spec(bglu.shape), _const_spec(wout.shape),
                  _const_spec(wup.shape), _const_spec(wdn.shape)],
        out_specs=pl.BlockSpec((tm, d), lambda i: (prev(i), 0)),
        out_shape=jax.ShapeDtypeStruct((t, d), jnp.float32),
        scratch_shapes=[pltpu.VMEM((tm, ff), _MXU_DTYPE), pltpu.VMEM((s5_w // LANES, tm, LANES), jnp.float32),
                        pltpu.VMEM((tm, d), _MXU_DTYPE), pltpu.VMEM((tm, d), jnp.float32)],
        compiler_params=pltpu.CompilerParams(dimension_semantics=("arbitrary",),
                                             vmem_limit_bytes=VMEM_LIMIT_BYTES),
        name="out_ffn",
    )(x1, yna, ys5, mod, mod, lng, lnb, wglu, bglu, wout, wup, wdn)


def _pick_tile(n, target):
    t = min(n, target)
    while n % t:
        t //= 2
    return t


def kernel(x, c, ctx, c_ctx, w_ada, b_ada, ln_g, ln_b, ffn1_w_up, ffn1_w_down, w_in, na_rpb, s5_a_re, s5_a_im, s5_log_dt, s5_b_re, s5_b_im, s5_c_re, s5_c_im, s5_d, s5_w_glu, s5_b_glu, w_out, ffn2_w_up, ffn2_w_down):
    depth = w_ada.shape[0]
    assert depth == 1, "single-layer stack only"
    bsz, seq, d = x.shape
    l_ctx = ctx.shape[1]
    s5_w = s5_w_glu.shape[1]
    na_w = w_out.shape[1] - s5_w
    groups = s5_w // S5_CG
    rows = seq // GRID_W
    assert bsz % 8 == 0 and seq % (NA_STEP_BLOCKS * NA_BLOCK) == 0 and rows >= NA_KROWS
    assert l_ctx % NA_BLOCK == 0 and groups % 2 == 0 and s5_a_re.shape[3] == S5_P and 2 * S5_P == LANES
    assert na_w % (2 * NA_DH) == 0 and ffn1_w_down.shape[1] % MXU_TILE == 0
    alpha = (2.0 * depth) ** 0.25
    cd = _MXU_DTYPE

    n_rows = -(-(bsz + 1) // 8) * 8
    cc = jnp.zeros((n_rows, d), jnp.float32).at[:bsz].set(c).at[bsz].set(c_ctx)
    n_ada = w_ada.shape[2]
    mod = _ada(cc, w_ada[0], b_ada[0][None, :], _pick_tile(n_ada, 1152)).reshape(n_rows, N_MOD, d)

    lng, lnb = ln_g[0], ln_b[0]
    wup1, wdn1 = ffn1_w_up[0].astype(cd), ffn1_w_down[0].astype(cd)
    win = w_in[0].astype(cd)

    tm = _pick_tile(seq, 512)
    x1, qk, vt, xs = _ffn_in(x.reshape(bsz * seq, d), mod, seq // tm, lng, lnb, wup1, wdn1, win, alpha, na_w, tm)
    tmc = _pick_tile(bsz * l_ctx, 512)
    _, qk_c, vt_c, xs_c = _ffn_in(ctx.reshape(bsz * l_ctx, d), mod[bsz:bsz + 1], None, lng, lnb, wup1, wdn1, win,
                                  alpha, na_w, tmc)

    tables = _s5_tables(s5_a_re[0], s5_a_im[0], s5_log_dt[0], s5_b_re[0], s5_b_im[0], s5_c_re[0], s5_c_im[0],
                        s5_d[0])
    y_s5, wup2, wdn2, wout, wglu = _s5(xs, xs_c, *tables, bsz,
                                       (ffn2_w_up[0], ffn2_w_down[0], w_out[0], s5_w_glu[0]))

    y_na = _na(qk.reshape(bsz, seq, 2 * na_w), vt, qk_c.reshape(bsz, l_ctx, 2 * na_w), vt_c, _na_bias_columns(na_rpb[0]),
               na_w).reshape(bsz * seq, na_w)

    out = _out_ffn(x1, y_na, y_s5, mod, seq // tm, lng, lnb, wglu, s5_b_glu[0][None, :], wout, wup2, wdn2,
                   alpha, tm)
    return out.reshape(bsz, seq, d)
```

```python
import functools

import jax
import jax.numpy as jnp
import numpy as np
from jax import lax
from jax.experimental import pallas as pl
from jax.experimental.pallas import tpu as pltpu

GRID_W = 64
NA_DH = 64
NA_KR = 8
NA_KC = 16
S5_CG = 16
S5_P = 64
LN_EPS = 1e-6
N_MOD = 9

CHUNK = 16
NA_QROWS = 4
NA_KROWS = 12
NA_BLOCK = NA_QROWS * GRID_W
NA_STEP_BLOCKS = 4
MASK_VALUE = -1e30
LOG2_E = 1.4426950408889634
SCAN_UNROLL = 4

MXU_TILE = 256
LANES = 128
SUBLANES = 8
VMEM_LIMIT_BYTES = 56 * 1024 * 1024

_MXU_DTYPE = jnp.bfloat16
_HI = lax.Precision.HIGHEST


def _mm(a, b):
    return jnp.dot(a, b, preferred_element_type=jnp.float32)


def _mm_nt(a, b):
    return lax.dot_general(a, b, (((1,), (1,)), ((), ())), preferred_element_type=jnp.float32)


def _const_spec(shape):
    nd = len(shape)
    return pl.BlockSpec(shape, lambda *_: (0,) * nd, pipeline_mode=pl.Buffered(1))


def _layer_norm(r, g, b):
    mu = jnp.mean(r, axis=-1, keepdims=True)
    d = r - mu
    var = jnp.mean(d * d, axis=-1, keepdims=True)
    return d * lax.rsqrt(var + LN_EPS) * g + b


def _swiglu_hidden(h, wup_ref, p_s, between=()):
    ff = p_s.shape[1]
    n_chunks = ff // MXU_TILE
    slots = [(k + 1) * n_chunks // (len(between) + 1) for k in range(len(between))]
    for j in range(n_chunks):
        for slot, fn in zip(slots, between):
            if slot == j:
                fn()
        lo = j * MXU_TILE
        a = _mm(h, wup_ref[:, lo:lo + MXU_TILE])
        g = _mm(h, wup_ref[:, ff + lo:ff + lo + MXU_TILE])
        p_s[:, lo:lo + MXU_TILE] = (g * jax.nn.sigmoid(g) * a).astype(p_s.dtype)


def _swiglu(h, wup_ref, wdn_ref, p_s, between=()):
    _swiglu_hidden(h, wup_ref, p_s, between)
    return _mm(p_s[...], wdn_ref[...])


def _block_transpose(tiles):
    per = len(tiles)
    blk = lax.broadcasted_iota(jnp.int32, (1, LANES), 1) // S5_CG
    tiles = list(tiles)
    d = per // 2
    while d:
        upper = (blk & d) != 0
        for a in range(per):
            if a & d:
                continue
            lo, hi = tiles[a], tiles[a + d]
            tiles[a] = jnp.where(upper, pltpu.roll(hi, S5_CG * d, axis=1), lo)
            tiles[a + d] = jnp.where(upper, hi, pltpu.roll(lo, LANES - S5_CG * d, axis=1))
        d //= 2
    return tiles


def _tokens_to_chunks(u_s, xs_ref, row0, n_rows):
    n_ch = n_rows // CHUNK
    ch0 = row0 // CHUNK
    per = LANES // S5_CG
    for v in range(u_s.shape[0]):
        for w in range(CHUNK // per):
            by_group = _block_transpose(
                [u_s[v, pl.ds(row0 + per * w + jj, n_ch, stride=CHUNK), :] for jj in range(per)])
            for gg in range(per):
                xs_ref[per * v + gg, ch0:ch0 + n_ch, w * LANES:(w + 1) * LANES] = by_group[gg].astype(xs_ref.dtype)


def _chunks_to_tokens(ys_ref, y_s):
    n_ch = y_s.shape[1] // CHUNK
    per = LANES // S5_CG
    for v in range(y_s.shape[0]):
        for w in range(CHUNK // per):
            by_phase = _block_transpose([ys_ref[per * v + gg, :, w * LANES:(w + 1) * LANES] for gg in range(per)])
            for jj in range(per):
                y_s[v, pl.ds(per * w + jj, n_ch, stride=CHUNK), :] = by_phase[jj]


def _ada_kernel(c_ref, w_ref, b_ref, o_ref):
    cc = c_ref[...]
    s = cc * jax.nn.sigmoid(cc)
    w = w_ref[...]
    s_hi = s.astype(_MXU_DTYPE)
    s_lo = (s - s_hi.astype(jnp.float32)).astype(_MXU_DTYPE)
    w_hi = w.astype(_MXU_DTYPE)
    w_lo = (w - w_hi.astype(jnp.float32)).astype(_MXU_DTYPE)
    o_ref[...] = _mm(s_hi, w_hi) + (_mm(s_lo, w_hi) + _mm(s_hi, w_lo)) + b_ref[...]


def _ada(cc, w, b, bn):
    rows, d = cc.shape
    n = w.shape[1]
    return pl.pallas_call(
        _ada_kernel,
        grid=(n // bn,),
        in_specs=[pl.BlockSpec((rows, d), lambda j: (0, 0)),
                  pl.BlockSpec((d, bn), lambda j: (0, j)),
                  pl.BlockSpec((1, bn), lambda j: (0, j))],
        out_specs=pl.BlockSpec((rows, bn), lambda j: (0, j)),
        out_shape=jax.ShapeDtypeStruct((rows, n), jnp.float32),
        compiler_params=pltpu.CompilerParams(dimension_semantics=("arbitrary",),
                                             vmem_limit_bytes=VMEM_LIMIT_BYTES),
        name="ada",
    )(cc, w, b)


def _ffn_in_kernel(alpha, na_w, x_ref, mod_ref, lng_ref, lnb_ref, wup_ref, wdn_ref, win_ref,
                   x1_ref, qk_ref, vt_ref, xs_ref, p_s, u_s):
    mod = mod_ref[0]
    h = (x_ref[...] * (1.0 + mod[1:2]) + mod[0:1]).astype(p_s.dtype)
    _swiglu_hidden(h, wup_ref, p_s)
    n_blk = vt_ref.shape[0]
    ys = [_mm(p_s[t * NA_BLOCK:(t + 1) * NA_BLOCK, :], wdn_ref[...]) for t in range(n_blk)]
    for t in range(n_blk):
        rows = slice(t * NA_BLOCK, (t + 1) * NA_BLOCK)
        x1 = _layer_norm(alpha * x_ref[rows, :] + (0.5 * mod[2:3]) * ys[t], lng_ref[0:1], lnb_ref[0:1])
        x1_ref[rows, :] = x1
        h2 = (x1 * (1.0 + mod[4:5]) + mod[3:4]).astype(p_s.dtype)
        pu = _mm(h2, win_ref[:, 3 * na_w:])
        for v in range(u_s.shape[0]):
            u_s[v, rows, :] = pu[:, v * LANES:(v + 1) * LANES]
        _tokens_to_chunks(u_s, xs_ref, t * NA_BLOCK, NA_BLOCK)
        pr = _mm(h2, win_ref[:, :3 * na_w])
        qk_ref[rows, :na_w] = (pr[:, :na_w] * (NA_DH ** -0.5 * LOG2_E)).astype(qk_ref.dtype)
        qk_ref[rows, na_w:] = pr[:, na_w:2 * na_w].astype(qk_ref.dtype)
        vt_ref[t] = pr[:, 2 * na_w:].T.astype(vt_ref.dtype)


def _ffn_in(x2d, mod, tiles_per_mod, lng, lnb, wup, wdn, win, alpha, na_w, tm):
    t, d = x2d.shape
    ff = wdn.shape[0]
    ncol = win.shape[1]
    s5_w = ncol - 3 * na_w
    groups, cw = s5_w // S5_CG, CHUNK * S5_CG
    if tiles_per_mod is None:
        mod_map = lambda i: (0, 0, 0)
    else:
        mod_map = lambda i: (i // tiles_per_mod, 0, 0)
    return pl.pallas_call(
        functools.partial(_ffn_in_kernel, alpha, na_w),
        grid=(t // tm,),
        in_specs=[pl.BlockSpec((tm, d), lambda i: (i, 0)),
                  pl.BlockSpec((1, N_MOD, d), mod_map),
                  _const_spec(lng.shape), _const_spec(lnb.shape),
                  _const_spec(wup.shape), _const_spec(wdn.shape), _const_spec(win.shape)],
        out_specs=[pl.BlockSpec((tm, d), lambda i: (i, 0)),
                   pl.BlockSpec((tm, 2 * na_w), lambda i: (i, 0)),
                   pl.BlockSpec((tm // NA_BLOCK, na_w, NA_BLOCK), lambda i: (i, 0, 0)),
                   pl.BlockSpec((groups, tm // CHUNK, cw), lambda i: (0, i, 0))],
        out_shape=[jax.ShapeDtypeStruct((t, d), jnp.float32),
                   jax.ShapeDtypeStruct((t, 2 * na_w), _MXU_DTYPE),
                   jax.ShapeDtypeStruct((t // NA_BLOCK, na_w, NA_BLOCK), _MXU_DTYPE),
                   jax.ShapeDtypeStruct((groups, t // CHUNK, cw), jnp.float32)],
        scratch_shapes=[pltpu.VMEM((tm, ff), _MXU_DTYPE), pltpu.VMEM((s5_w // LANES, tm, LANES), jnp.float32)],
        compiler_params=pltpu.CompilerParams(dimension_semantics=("arbitrary",),
                                             vmem_limit_bytes=VMEM_LIMIT_BYTES),
        name="ffn_in",
    )(x2d, mod, lng, lnb, wup, wdn, win)


def _toeplitz_kernel(lhs_ref, rhs_ref, o_ref):
    width = o_ref.shape[2]
    lane = lax.broadcasted_iota(jnp.int32, (1, width), 1)
    for g in range(o_ref.shape[0]):
        k_f = jnp.dot(lhs_ref[0, g], rhs_ref[0, g], precision=_HI, preferred_element_type=jnp.float32)
        k_b = jnp.dot(lhs_ref[1, g], rhs_ref[1, g], precision=_HI, preferred_element_type=jnp.float32)
        for i in range(CHUNK):
            lo, hi = i * S5_CG, (i + 1) * S5_CG
            fwd = k_f if lo == 0 else jnp.where(lane >= lo, pltpu.roll(k_f, lo, axis=1), 0.0)
            bwd = k_b if hi == width else jnp.where(lane < hi, pltpu.roll(k_b, hi, axis=1), 0.0)
            o_ref[g, lo:hi, :] = (fwd + bwd).astype(o_ref.dtype)


def _toeplitz(lhs, rhs):
    _, g, cg, kdim = lhs.shape
    width = rhs.shape[3]
    gb = _pick_tile(g, 8)
    return pl.pallas_call(
        _toeplitz_kernel,
        grid=(g // gb,),
        in_specs=[pl.BlockSpec((2, gb, cg, kdim), lambda k: (0, k, 0, 0)),
                  pl.BlockSpec((2, gb, kdim, width), lambda k: (0, k, 0, 0))],
        out_specs=pl.BlockSpec((gb, width, width), lambda k: (k, 0, 0)),
        out_shape=jax.ShapeDtypeStruct((g, width, width), _MXU_DTYPE),
        compiler_params=pltpu.CompilerParams(dimension_semantics=("arbitrary",)),
        name="s5_toeplitz",
    )(lhs, rhs)


def _s5_tables(a_re, a_im, log_dt, b_re, b_im, c_re, c_im, d_skip):
    g, p = a_re.shape[1], a_re.shape[2]
    f32 = jnp.float32
    a_re, a_im = a_re.astype(f32), a_im.astype(f32)
    dt = jnp.exp(log_dt.astype(f32))[..., None]
    lr, li = a_re * dt, a_im * dt

    width = CHUNK * S5_CG
    asc = np.arange(CHUNK)
    desc = asc[::-1].copy()
    lane_m, lane_c = np.arange(width) // S5_CG, np.arange(width) % S5_CG

    def powers(expo, lr_b, li_b):
        e = jnp.asarray(expo, f32)
        mag = jnp.exp(e * lr_b)
        return mag * jnp.cos(e * li_b), mag * jnp.sin(e * li_b)


    ab_r, ab_i = jnp.exp(lr) * jnp.cos(li), jnp.exp(lr) * jnp.sin(li)
    den = a_re ** 2 + a_im ** 2
    nr, ni = ab_r - 1.0, ab_i
    sr = ((nr * a_re + ni * a_im) / den)[:, :, None, :]
    si = ((ni * a_re - nr * a_im) / den)[:, :, None, :]
    bt_re, bt_im = jnp.swapaxes(b_re, 2, 3).astype(f32), jnp.swapaxes(b_im, 2, 3).astype(f32)
    bt_r = sr * bt_re - si * bt_im
    bt_i = sr * bt_im + si * bt_re

    expand = lax.Precision.HIGH
    tile_c = jnp.asarray(lane_c[None, :] == np.arange(S5_CG)[:, None], f32)
    ct_r = jnp.einsum('dgcp,cl->dgpl', c_re.astype(f32), tile_c, precision=expand)
    ct_i = jnp.einsum('dgcp,cl->dgpl', c_im.astype(f32), tile_c, precision=expand)

    rep_m = jnp.asarray(lane_m[None, :] == np.arange(CHUNK)[:, None], f32)

    def by_lane(expo):
        pw = powers(np.stack(expo)[:, None, None, :], lr[..., None], li[..., None])
        return tuple(jnp.einsum('dgpm,ml->dgpl', v, rep_m, precision=expand) for v in pw)

    q_r, q_i = by_lane([asc, desc])
    rhs = jnp.concatenate([ct_r * q_r - ct_i * q_i, ct_r * q_i + ct_i * q_r], axis=2)
    toep = _toeplitz(jnp.concatenate([bt_r, -bt_i], axis=3), rhs)

    v_r, v_i = by_lane([asc + 1, desc + 1])
    cs_r = ct_r * v_r - ct_i * v_i
    cs_i = -(ct_r * v_i + ct_i * v_r)

    w_r, w_i = (jnp.repeat(v, S5_CG, axis=2) for v in
                powers(np.stack([desc, asc])[:, None, :, None], lr[:, :, None, :], li[:, :, None, :]))
    rows_r, rows_i = jnp.tile(bt_r, (1, 1, CHUNK, 1)), jnp.tile(bt_i, (1, 1, CHUNK, 1))
    bs_r = w_r * rows_r - w_i * rows_i
    bs_i = w_r * rows_i + w_i * rows_r
    even = jnp.asarray((np.arange(g) % 2) == 0)[None, :, None, None]
    bs = jnp.concatenate([jnp.where(even, bs_r, 0.0), jnp.where(even, 0.0, bs_r),
                          jnp.where(even, bs_i, 0.0), jnp.where(even, 0.0, bs_i)], axis=3)
    cs = jnp.concatenate([jnp.where(even, cs_r, 0.0), jnp.where(even, 0.0, cs_r),
                          jnp.where(even, cs_i, 0.0), jnp.where(even, 0.0, cs_i)], axis=2)

    e_r, e_i = jnp.exp(CHUNK * lr), CHUNK * li
    half = 2 * p
    a_pow = jnp.stack([(e_r * jnp.cos(e_i)).reshape(2, g // 2, half),
                       (e_r * jnp.sin(e_i)).reshape(2, g // 2, half)], axis=2)
    dvec = jnp.dot(d_skip.astype(f32), tile_c, precision=_HI)[:, None, :]
    return toep, bs.astype(_MXU_DTYPE), cs.astype(_MXU_DTYPE), a_pow, dvec


def _row_pitch(n):
    return n + (4 - n) % 8


def _s5_kernel(nb, n_cast, xl_ref, xc_ref, toep_ref, bs_ref, cs_ref, apow_ref, dvec_ref, *refs):
    cast_in, (y_ref, *cast_out), (zc_s, zl_s, h_s) = refs[:n_cast], refs[n_cast:2 * n_cast + 1], refs[2 * n_cast + 1:]
    for src, dst in zip(cast_in, cast_out):
        dst[...] = src[...].astype(dst.dtype)
    rows_l = xl_ref.shape[1]
    rows_c = xc_ref.shape[1]
    n_l, n_c = rows_l // nb, rows_c // nb
    pitch_l, pitch_c = _row_pitch(n_l), _row_pitch(n_c)
    half = zl_s.shape[3]
    xl = [xl_ref[g] for g in range(2)]
    xlb = [v.astype(_MXU_DTYPE) for v in xl]
    xcb = [xc_ref[g].astype(_MXU_DTYPE) for g in range(2)]

    for d in range(2):
        zc = _mm(xcb[0], bs_ref[d, 0]) + _mm(xcb[1], bs_ref[d, 1])
        zl = _mm(xlb[0], bs_ref[d, 0]) + _mm(xlb[1], bs_ref[d, 1])
        for ri in range(2):
            for b in range(nb):
                zc_s[d, ri, b * pitch_c:b * pitch_c + n_c, :] = zc[b * n_c:(b + 1) * n_c, ri * half:(ri + 1) * half]
                zl_s[d, ri, b * pitch_l:b * pitch_l + n_l, :] = zl[b * n_l:(b + 1) * n_l, ri * half:(ri + 1) * half]

    a_r = [jnp.broadcast_to(apow_ref[d, 0, 0:1, :], (nb, half)) for d in range(2)]
    a_i = [jnp.broadcast_to(apow_ref[d, 0, 1:2, :], (nb, half)) for d in range(2)]

    def step(d, n, latent, h_r, h_i):
        if latent:
            h_s[d, 0, pl.ds(n, nb, stride=pitch_l), :] = h_r
            h_s[d, 1, pl.ds(n, nb, stride=pitch_l), :] = h_i
            z_r = zl_s[d, 0, pl.ds(n, nb, stride=pitch_l), :]
            z_i = zl_s[d, 1, pl.ds(n, nb, stride=pitch_l), :]
        else:
            z_r = zc_s[d, 0, pl.ds(n, nb, stride=pitch_c), :]
            z_i = zc_s[d, 1, pl.ds(n, nb, stride=pitch_c), :]
        return (a_r[d] * h_r - a_i[d] * h_i + z_r,
                a_r[d] * h_i + a_i[d] * h_r + z_i)

    def ctx_body(n, carry):
        f_r, f_i, b_r, b_i = carry
        f_r, f_i = step(0, n, False, f_r, f_i)
        b_r, b_i = step(1, n_c - 1 - n, False, b_r, b_i)
        return f_r, f_i, b_r, b_i

    def lat_body(n, carry):
        f_r, f_i, b_r, b_i = carry
        f_r, f_i = step(0, n, True, f_r, f_i)
        b_r, b_i = step(1, n_l - 1 - n, True, b_r, b_i)
        return f_r, f_i, b_r, b_i

    zero = jnp.zeros((nb, half), jnp.float32)
    carry = lax.fori_loop(0, n_c, ctx_body, (zero, zero, zero, zero), unroll=SCAN_UNROLL)
    lax.fori_loop(0, n_l, lat_body, carry, unroll=SCAN_UNROLL)

    def plane(d, ri):
        return jnp.concatenate([h_s[d, ri, b * pitch_l:b * pitch_l + n_l, :] for b in range(nb)], axis=0)

    hb = [jnp.concatenate([plane(d, 0), plane(d, 1)], axis=1).astype(_MXU_DTYPE) for d in range(2)]
    for g in range(2):
        y = _mm(xlb[g], toep_ref[g]) + _mm(hb[0], cs_ref[0, g]) + _mm(hb[1], cs_ref[1, g])
        y_ref[g] = y + xl[g] * dvec_ref[g]


def _s5(xl, xc, toep, bs, cs, a_pow, dvec, nb, to_cast):
    g, rows_l, w = xl.shape
    rows_c = xc.shape[1]
    sw = bs.shape[3]
    steps = g // 2
    slabs = [m.shape[0] // steps for m in to_cast]
    assert all(m.shape[0] == s * steps and s % (2 * SUBLANES) == 0 for m, s in zip(to_cast, slabs))
    cast_specs = [pl.BlockSpec((s, m.shape[1]), lambda k: (k, 0)) for m, s in zip(to_cast, slabs)]
    return pl.pallas_call(
        functools.partial(_s5_kernel, nb, len(to_cast)),
        grid=(steps,),
        in_specs=[pl.BlockSpec((2, rows_l, w), lambda k: (k, 0, 0)),
                  pl.BlockSpec((2, rows_c, w), lambda k: (k, 0, 0)),
                  pl.BlockSpec((2, w, w), lambda k: (k, 0, 0)),
                  pl.BlockSpec((2, 2, w, sw), lambda k: (0, k, 0, 0)),
                  pl.BlockSpec((2, 2, sw, w), lambda k: (0, k, 0, 0)),
                  pl.BlockSpec((2, 1, 2, sw // 2), lambda k: (0, k, 0, 0)),
                  pl.BlockSpec((2, 1, w), lambda k: (k, 0, 0))] + cast_specs,
        out_specs=[pl.BlockSpec((2, rows_l, w), lambda k: (k, 0, 0))] + cast_specs,
        out_shape=[jax.ShapeDtypeStruct((g, rows_l, w), jnp.float32)]
        + [jax.ShapeDtypeStruct(m.shape, _MXU_DTYPE) for m in to_cast],
        scratch_shapes=[pltpu.VMEM((2, 2, nb * _row_pitch(rows_c // nb), sw // 2), jnp.float32),
                        pltpu.VMEM((2, 2, nb * _row_pitch(rows_l // nb), sw // 2), jnp.float32),
                        pltpu.VMEM((2, 2, nb * _row_pitch(rows_l // nb), sw // 2), jnp.float32)],
        compiler_params=pltpu.CompilerParams(dimension_semantics=("arbitrary",),
                                             vmem_limit_bytes=VMEM_LIMIT_BYTES),
        name="s5",
    )(xl, xc, toep, bs, cs, a_pow, dvec, *to_cast)


def _na_row_plan():
    last_q = NA_KROWS - NA_QROWS
    plan = []
    for r0s, qrs in ((lambda lr: 0, lambda lr: lr),
                     (lambda lr: lr, lambda lr: NA_KR // 2 + lr),
                     (lambda lr: last_q - NA_KR // 2, lambda lr: last_q + lr)):
        plan.append([[kl - qrs(lr) + (NA_KR - 1) if r0s(lr) <= kl < r0s(lr) + NA_KR else None
                      for kl in range(NA_KROWS)] for lr in range(NA_QROWS)])
    return plan


def _na_bias_fill(col_ref, bias_s):
    plan = _na_row_plan()
    lane = lax.broadcasted_iota(jnp.int32, (1, 2 * GRID_W), 1)
    masked = jnp.full((GRID_W, 2 * GRID_W), MASK_VALUE, jnp.float32)
    for head in range(bias_s.shape[1]):
        for kind in range(len(plan)):
            for kl in range(NA_KROWS):
                for lp in range(NA_QROWS // 2):
                    a0, a1 = plan[kind][2 * lp][kl], plan[kind][2 * lp + 1][kl]
                    t0 = masked if a0 is None else col_ref[head, a0]
                    t1 = masked if a1 is None else col_ref[head, a1]
                    tile = t0 if a0 == a1 else jnp.where(lane < GRID_W, t0, t1)
                    bias_s[kind, head, kl * GRID_W:(kl + 1) * GRID_W, lp * 2 * GRID_W:(lp + 1) * 2 * GRID_W] = tile


def _na_bias_columns(rpb):
    kc = np.arange(GRID_W)[:, None]
    qc = np.arange(GRID_W)[None, :]
    c0 = np.clip(qc - NA_KC // 2, 0, GRID_W - NA_KC)
    col_ok = (kc >= c0) & (kc < c0 + NA_KC)
    sel_c = ((kc - qc + (NA_KC - 1))[:, :, None] == np.arange(2 * NA_KC - 1)) & col_ok[:, :, None]
    by_col = jnp.einsum('hab,kqb->hakq', rpb.astype(jnp.float32), jnp.asarray(sel_c, jnp.float32), precision=_HI)
    by_col = jnp.where(jnp.asarray(col_ok), by_col * LOG2_E, MASK_VALUE)
    return jnp.concatenate([by_col, by_col], axis=3)


def _row_max(x):
    while x.shape[0] % (2 * SUBLANES) == 0:
        half = x.shape[0] // 2
        x = jnp.maximum(x[:half], x[half:])
    return jnp.max(x, axis=0, keepdims=True)


def _na_kernel(rows, q_ref, k_ref, vt_ref, kc_ref, vct_ref, col_ref, o_ref, bias_ref):
    @pl.when((pl.program_id(0) == 0) & (pl.program_id(1) == 0))
    def _():
        _na_bias_fill(col_ref, bias_ref)

    n_rb = rows // NA_QROWS
    nk = NA_KROWS * GRID_W
    pair = 2 * NA_DH
    lane_first = lax.broadcasted_iota(jnp.int32, (1, pair), 1) < NA_DH
    row_first = lax.broadcasted_iota(jnp.int32, (pair, 1), 0) < NA_DH
    n_heads = q_ref.shape[2] // NA_DH
    n_blocks = q_ref.shape[1] // NA_BLOCK

    def block(j):
        rb = pl.program_id(1) * n_blocks + j
        kind = jnp.where(rb == 0, 0, jnp.where(rb == n_rb - 1, 2, 1))
        first_row = jnp.clip(rb * NA_QROWS - NA_KR // 2, 0, rows - NA_KROWS)
        return kind, pl.multiple_of(first_row * GRID_W, GRID_W), first_row // NA_QROWS

    geometry = [block(j) for j in range(n_blocks)]

    def scores(u):
        j, h = divmod(u, n_heads)
        kind, start, _ = geometry[j]
        cols = slice(h // 2 * pair, (h // 2 + 1) * pair)
        lane_mine = lane_first if h % 2 == 0 else jnp.logical_not(lane_first)
        qm = jnp.where(lane_mine, q_ref[0, j * NA_BLOCK:(j + 1) * NA_BLOCK, cols],
                       jnp.zeros((NA_BLOCK, pair), q_ref.dtype))
        s_w = _mm_nt(k_ref[0, pl.ds(start, nk), cols], qm) + bias_ref[kind, h]
        s_c = _mm_nt(kc_ref[0, :, cols], qm)
        return s_w, s_c

    def weights(s_w, s_c):
        mx = jnp.maximum(_row_max(s_w), _row_max(s_c))
        return jnp.exp2((s_w - mx).astype(vt_ref.dtype)), jnp.exp2((s_c - mx).astype(vt_ref.dtype))

    def values(u, p_w, p_c):
        j, h = divmod(u, n_heads)
        blk0 = geometry[j][2]
        cols = slice(h // 2 * pair, (h // 2 + 1) * pair)
        row_mine = row_first if h % 2 == 0 else jnp.logical_not(row_first)
        vtw = jnp.concatenate([vt_ref[blk0 + t, cols, :] for t in range(nk // NA_BLOCK)], axis=1)
        vtc = jnp.concatenate([vct_ref[t, cols, :] for t in range(vct_ref.shape[0])], axis=1)
        o = (_mm(jnp.where(row_mine, vtw, jnp.ones_like(vtw)), p_w)
             + _mm(jnp.where(row_mine, vtc, jnp.ones_like(vtc)), p_c))
        num, den = (o[:NA_DH], o[NA_DH:]) if h % 2 == 0 else (o[NA_DH:], o[:NA_DH])
        return num / den

    n_units = n_blocks * n_heads
    s = {0: scores(0), 1: scores(1)}
    p = {0: weights(*s.pop(0))}
    outs = []
    for u in range(n_units):
        if u + 2 < n_units:
            s[u + 2] = scores(u + 2)
        if u + 1 < n_units:
            p[u + 1] = weights(*s.pop(u + 1))
        outs.append(values(u, *p.pop(u)))
        if len(outs) == n_heads:
            j = u // n_heads
            o_ref[0, j * NA_BLOCK:(j + 1) * NA_BLOCK, :] = jnp.concatenate(outs, axis=0).T.astype(o_ref.dtype)
            outs = []


def _na(qk, vt, qk_c, vt_c, bias_cols, na_w):
    b, s, _ = qk.shape
    l = qk_c.shape[1]
    rows = s // GRID_W
    tq = NA_STEP_BLOCKS * NA_BLOCK
    n_kinds = len(_na_row_plan())
    return pl.pallas_call(
        functools.partial(_na_kernel, rows),
        grid=(b, s // tq),
        in_specs=[pl.BlockSpec((1, tq, na_w), lambda bi, rb: (bi, rb, 0)),
                  pl.BlockSpec((1, s, na_w), lambda bi, rb: (bi, 0, 1)),
                  pl.BlockSpec((s // NA_BLOCK, na_w, NA_BLOCK), lambda bi, rb: (bi, 0, 0)),
                  pl.BlockSpec((1, l, na_w), lambda bi, rb: (bi, 0, 1)),
                  pl.BlockSpec((l // NA_BLOCK, na_w, NA_BLOCK), lambda bi, rb: (bi, 0, 0)),
                  _const_spec(bias_cols.shape)],
        out_specs=pl.BlockSpec((1, tq, na_w), lambda bi, rb: (bi, rb, 0)),
        out_shape=jax.ShapeDtypeStruct((b, s, na_w), _MXU_DTYPE),
        scratch_shapes=[pltpu.VMEM((n_kinds, na_w // NA_DH, NA_KROWS * GRID_W, NA_BLOCK), jnp.float32)],
        compiler_params=pltpu.CompilerParams(
            dimension_semantics=("arbitrary", "arbitrary"),
            vmem_limit_bytes=VMEM_LIMIT_BYTES),
        name="na",
    )(qk, qk, vt, qk_c, vt_c, bias_cols)


def _out_ffn_kernel(alpha, x1_ref, yna_ref, ys5_ref, mod_a_ref, mod_b_ref, lng_ref, lnb_ref, wglu_ref, bglu_ref,
                    wout_ref, wup_ref, wdn_ref, o_ref, p_s, y_s, h_s, x2_s):
    @pl.when(pl.program_id(0) == 0)
    def _():
        h_s[...] = jnp.zeros_like(h_s)
        x2_s[...] = jnp.zeros_like(x2_s)

    mod_a = mod_a_ref[0]
    na_w = yna_ref.shape[1]
    stage_a = {}

    def relayout():
        _chunks_to_tokens(ys5_ref, y_s)

    def gated():
        gl = jax.nn.gelu(jnp.concatenate([y_s[v] for v in range(y_s.shape[0])], axis=1))
        gate = jax.nn.sigmoid(_mm(gl.astype(p_s.dtype), wglu_ref[...]) + bglu_ref[...])
        stage_a["glu"] = (gl * gate).astype(p_s.dtype)

    def mixed():
        y = _mm(yna_ref[...], wout_ref[:na_w, :]) + _mm(stage_a["glu"], wout_ref[na_w:, :])
        x2 = _layer_norm(alpha * x1_ref[...] + mod_a[5:6] * y, lng_ref[1:2], lnb_ref[1:2])
        stage_a["x2"] = x2
        stage_a["h"] = (x2 * (1.0 + mod_a[7:8]) + mod_a[6:7]).astype(h_s.dtype)

    mod_b = mod_b_ref[0]
    _swiglu_hidden(h_s[...], wup_ref, p_s, between=(relayout, gated, mixed))
    n_blk = o_ref.shape[0] // NA_BLOCK
    fs = [_mm(p_s[t * NA_BLOCK:(t + 1) * NA_BLOCK, :], wdn_ref[...]) for t in range(n_blk)]
    for t in range(n_blk):
        rows = slice(t * NA_BLOCK, (t + 1) * NA_BLOCK)
        o_ref[rows, :] = _layer_norm(alpha * x2_s[rows, :] + (0.5 * mod_b[8:9]) * fs[t], lng_ref[2:3], lnb_ref[2:3])

    x2_s[...] = stage_a["x2"]
    h_s[...] = stage_a["h"]


def _out_ffn(x1, yna, ys5, mod, tiles_per_mod, lng, lnb, wglu, bglu, wout, wup, wdn, alpha, tm):
    t, d = x1.shape
    ff = wdn.shape[0]
    na_w = yna.shape[1]
    groups, _, cw = ys5.shape
    s5_w = groups * S5_CG
    n = t // tm

    def cur(i):
        return jnp.minimum(i, n - 1)

    def prev(i):
        return jnp.maximum(i - 1, 0)

    return pl.pallas_call(
        functools.partial(_out_ffn_kernel, alpha),
        grid=(n + 1,),
        in_specs=[pl.BlockSpec((tm, d), lambda i: (cur(i), 0)),
                  pl.BlockSpec((tm, na_w), lambda i: (cur(i), 0)),
                  pl.BlockSpec((groups, tm // CHUNK, cw), lambda i: (0, cur(i), 0)),
                  pl.BlockSpec((1, N_MOD, d), lambda i: (cur(i) // tiles_per_mod, 0, 0)),
                  pl.BlockSpec((1, N_MOD, d), lambda i: (prev(i) // tiles_per_mod, 0, 0)),
                  _const_spec(lng.shape), _const_spec(lnb.shape),
                  _const_spec(wglu.shape), _const_spec(bglu.shape), _const_spec(wout.shape),
                  _const_spec(wup.shape), _const_spec(wdn.shape)],
        out_specs=pl.BlockSpec((tm, d), lambda i: (prev(i), 0)),
        out_shape=jax.ShapeDtypeStruct((t, d), jnp.float32),
        scratch_shapes=[pltpu.VMEM((tm, ff), _MXU_DTYPE), pltpu.VMEM((s5_w // LANES, tm, LANES), jnp.float32),
                        pltpu.VMEM((tm, d), _MXU_DTYPE), pltpu.VMEM((tm, d), jnp.float32)],
        compiler_params=pltpu.CompilerParams(dimension_semantics=("arbitrary",),
                                             vmem_limit_bytes=VMEM_LIMIT_BYTES),
        name="out_ffn",
    )(x1, yna, ys5, mod, mod, lng, lnb, wglu, bglu, wout, wup, wdn)


def _pick_tile(n, target):
    t = min(n, target)
    while n % t:
        t //= 2
    return t


def kernel(x, c, ctx, c_ctx, w_ada, b_ada, ln_g, ln_b, ffn1_w_up, ffn1_w_down, w_in, na_rpb, s5_a_re, s5_a_im, s5_log_dt, s5_b_re, s5_b_im, s5_c_re, s5_c_im, s5_d, s5_w_glu, s5_b_glu, w_out, ffn2_w_up, ffn2_w_down):
    depth = w_ada.shape[0]
    assert depth == 1, "single-layer stack only"
    bsz, seq, d = x.shape
    l_ctx = ctx.shape[1]
    s5_w = s5_w_glu.shape[1]
    na_w = w_out.shape[1] - s5_w
    groups = s5_w // S5_CG
    rows = seq // GRID_W
    assert bsz % 8 == 0 and seq % (NA_STEP_BLOCKS * NA_BLOCK) == 0 and rows >= NA_KROWS
    assert l_ctx % NA_BLOCK == 0 and groups % 2 == 0 and s5_a_re.shape[3] == S5_P and 2 * S5_P == LANES
    assert na_w % (2 * NA_DH) == 0 and ffn1_w_down.shape[1] % MXU_TILE == 0
    alpha = (2.0 * depth) ** 0.25
    cd = _MXU_DTYPE

    n_rows = -(-(bsz + 1) // 8) * 8
    cc = jnp.zeros((n_rows, d), jnp.float32).at[:bsz].set(c).at[bsz].set(c_ctx)
    n_ada = w_ada.shape[2]
    mod = _ada(cc, w_ada[0], b_ada[0][None, :], _pick_tile(n_ada, 1152)).reshape(n_rows, N_MOD, d)

    lng, lnb = ln_g[0], ln_b[0]
    wup1, wdn1 = ffn1_w_up[0].astype(cd), ffn1_w_down[0].astype(cd)
    win = w_in[0].astype(cd)

    tm = _pick_tile(seq, 512)
    x1, qk, vt, xs = _ffn_in(x.reshape(bsz * seq, d), mod, seq // tm, lng, lnb, wup1, wdn1, win, alpha, na_w, tm)
    tmc = _pick_tile(bsz * l_ctx, 512)
    _, qk_c, vt_c, xs_c = _ffn_in(ctx.reshape(bsz * l_ctx, d), mod[bsz:bsz + 1], None, lng, lnb, wup1, wdn1, win,
                                  alpha, na_w, tmc)

    tables = _s5_tables(s5_a_re[0], s5_a_im[0], s5_log_dt[0], s5_b_re[0], s5_b_im[0], s5_c_re[0], s5_c_im[0],
                        s5_d[0])
    y_s5, wup2, wdn2, wout, wglu = _s5(xs, xs_c, *tables, bsz,
                                       (ffn2_w_up[0], ffn2_w_down[0], w_out[0], s5_w_glu[0]))

    y_na = _na(qk.reshape(bsz, seq, 2 * na_w), vt, qk_c.reshape(bsz, l_ctx, 2 * na_w), vt_c, _na_bias_columns(na_rpb[0]),
               na_w).reshape(bsz * seq, na_w)

    out = _out_ffn(x1, y_na, y_s5, mod, seq // tm, lng, lnb, wglu, s5_b_glu[0][None, :], wout, wup2, wdn2,
                   alpha, tm)
    return out.reshape(bsz, seq, d)
```

```python
import functools

import jax
import jax.numpy as jnp
import numpy as np
from jax import lax
from jax.experimental import pallas as pl
from jax.experimental.pallas import tpu as pltpu

GRID_W = 64
NA_DH = 64
NA_KR = 8
NA_KC = 16
S5_CG = 16
S5_P = 64
LN_EPS = 1e-6
N_MOD = 9

CHUNK = 16
NA_QROWS = 4
NA_KROWS = 12
NA_BLOCK = NA_QROWS * GRID_W
NA_STEP_BLOCKS = 4
MASK_VALUE = -1e30
LOG2_E = 1.4426950408889634
SCAN_UNROLL = 4

MXU_TILE = 256
LANES = 128
SUBLANES = 8
VMEM_LIMIT_BYTES = 56 * 1024 * 1024

_MXU_DTYPE = jnp.bfloat16
_HI = lax.Precision.HIGHEST


def _mm(a, b):
    return jnp.dot(a, b, preferred_element_type=jnp.float32)


def _mm_split(a, b):
    a_hi = a.astype(_MXU_DTYPE)
    a_lo = (a - a_hi.astype(jnp.float32)).astype(_MXU_DTYPE)
    b_hi = b.astype(_MXU_DTYPE)
    b_lo = (b - b_hi.astype(jnp.float32)).astype(_MXU_DTYPE)
    return _mm(a_hi, b_hi) + (_mm(a_lo, b_hi) + _mm(a_hi, b_lo))


def _mm_nt(a, b):
    return lax.dot_general(a, b, (((1,), (1,)), ((), ())), preferred_element_type=jnp.float32)


def _const_spec(shape):
    nd = len(shape)
    return pl.BlockSpec(shape, lambda *_: (0,) * nd, pipeline_mode=pl.Buffered(1))


def _layer_norm(r, g, b):
    mu = jnp.mean(r, axis=-1, keepdims=True)
    d = r - mu
    var = jnp.mean(d * d, axis=-1, keepdims=True)
    return d * lax.rsqrt(var + LN_EPS) * g + b


def _swiglu_hidden(h, wup_ref, p_s, between=()):
    ff = p_s.shape[1]
    n_chunks = ff // MXU_TILE
    slots = [(k + 1) * n_chunks // (len(between) + 1) for k in range(len(between))]
    for j in range(n_chunks):
        for slot, fn in zip(slots, between):
            if slot == j:
                fn()
        lo = j * MXU_TILE
        a = _mm(h, wup_ref[:, lo:lo + MXU_TILE])
        g = _mm(h, wup_ref[:, ff + lo:ff + lo + MXU_TILE])
        p_s[:, lo:lo + MXU_TILE] = (g * jax.nn.sigmoid(g) * a).astype(p_s.dtype)


def _swiglu(h, wup_ref, wdn_ref, p_s, between=()):
    _swiglu_hidden(h, wup_ref, p_s, between)
    return _mm(p_s[...], wdn_ref[...])


def _block_transpose(tiles):
    per = len(tiles)
    blk = lax.broadcasted_iota(jnp.int32, (1, LANES), 1) // S5_CG
    tiles = list(tiles)
    d = per // 2
    while d:
        upper = (blk & d) != 0
        for a in range(per):
            if a & d:
                continue
            lo, hi = tiles[a], tiles[a + d]
            tiles[a] = jnp.where(upper, pltpu.roll(hi, S5_CG * d, axis=1), lo)
            tiles[a + d] = jnp.where(upper, hi, pltpu.roll(lo, LANES - S5_CG * d, axis=1))
        d //= 2
    return tiles


def _tokens_to_chunks(u_s, xs_ref, row0, n_rows):
    n_ch = n_rows // CHUNK
    ch0 = row0 // CHUNK
    per = LANES // S5_CG
    for v in range(u_s.shape[0]):
        for w in range(CHUNK // per):
            by_group = _block_transpose(
                [u_s[v, pl.ds(row0 + per * w + jj, n_ch, stride=CHUNK), :] for jj in range(per)])
            for gg in range(per):
                xs_ref[per * v + gg, ch0:ch0 + n_ch, w * LANES:(w + 1) * LANES] = by_group[gg].astype(xs_ref.dtype)


def _chunks_to_tokens(ys_ref, y_s):
    n_ch = y_s.shape[1] // CHUNK
    per = LANES // S5_CG
    for v in range(y_s.shape[0]):
        for w in range(CHUNK // per):
            by_phase = _block_transpose([ys_ref[per * v + gg, :, w * LANES:(w + 1) * LANES] for gg in range(per)])
            for jj in range(per):
                y_s[v, pl.ds(per * w + jj, n_ch, stride=CHUNK), :] = by_phase[jj]


def _ada_kernel(c_ref, w_ref, b_ref, o_ref):
    cc = c_ref[...]
    o_ref[...] = _mm_split(cc * jax.nn.sigmoid(cc), w_ref[...]) + b_ref[...]


def _ada(cc, w, b, bn):
    rows, d = cc.shape
    n = w.shape[1]
    return pl.pallas_call(
        _ada_kernel,
        grid=(n // bn,),
        in_specs=[pl.BlockSpec((rows, d), lambda j: (0, 0)),
                  pl.BlockSpec((d, bn), lambda j: (0, j)),
                  pl.BlockSpec((1, bn), lambda j: (0, j))],
        out_specs=pl.BlockSpec((rows, bn), lambda j: (0, j)),
        out_shape=jax.ShapeDtypeStruct((rows, n), jnp.float32),
        compiler_params=pltpu.CompilerParams(dimension_semantics=("arbitrary",),
                                             vmem_limit_bytes=VMEM_LIMIT_BYTES),
        name="ada",
    )(cc, w, b)


def _ffn_in_kernel(alpha, na_w, x_ref, mod_ref, lng_ref, lnb_ref, wup_ref, wdn_ref, win_ref,
                   x1_ref, qk_ref, vt_ref, xs_ref, p_s, u_s):
    mod = mod_ref[0]
    h = (x_ref[...] * (1.0 + mod[1:2]) + mod[0:1]).astype(p_s.dtype)
    _swiglu_hidden(h, wup_ref, p_s)
    n_blk = vt_ref.shape[0]
    ys = [_mm(p_s[t * NA_BLOCK:(t + 1) * NA_BLOCK, :], wdn_ref[...]) for t in range(n_blk)]
    for t in range(n_blk):
        rows = slice(t * NA_BLOCK, (t + 1) * NA_BLOCK)
        x1 = _layer_norm(alpha * x_ref[rows, :] + (0.5 * mod[2:3]) * ys[t], lng_ref[0:1], lnb_ref[0:1])
        x1_ref[rows, :] = x1
        h2 = (x1 * (1.0 + mod[4:5]) + mod[3:4]).astype(p_s.dtype)
        pu = _mm(h2, win_ref[:, 3 * na_w:])
        for v in range(u_s.shape[0]):
            u_s[v, rows, :] = pu[:, v * LANES:(v + 1) * LANES]
        _tokens_to_chunks(u_s, xs_ref, t * NA_BLOCK, NA_BLOCK)
        pr = _mm(h2, win_ref[:, :3 * na_w])
        qk_ref[rows, :na_w] = (pr[:, :na_w] * (NA_DH ** -0.5 * LOG2_E)).astype(qk_ref.dtype)
        qk_ref[rows, na_w:] = pr[:, na_w:2 * na_w].astype(qk_ref.dtype)
        vt_ref[t] = pr[:, 2 * na_w:].T.astype(vt_ref.dtype)


def _ffn_in(x2d, mod, tiles_per_mod, lng, lnb, wup, wdn, win, alpha, na_w, tm):
    t, d = x2d.shape
    ff = wdn.shape[0]
    ncol = win.shape[1]
    s5_w = ncol - 3 * na_w
    groups, cw = s5_w // S5_CG, CHUNK * S5_CG
    if tiles_per_mod is None:
        mod_map = lambda i: (0, 0, 0)
    else:
        mod_map = lambda i: (i // tiles_per_mod, 0, 0)
    return pl.pallas_call(
        functools.partial(_ffn_in_kernel, alpha, na_w),
        grid=(t // tm,),
        in_specs=[pl.BlockSpec((tm, d), lambda i: (i, 0)),
                  pl.BlockSpec((1, N_MOD, d), mod_map),
                  _const_spec(lng.shape), _const_spec(lnb.shape),
                  _const_spec(wup.shape), _const_spec(wdn.shape), _const_spec(win.shape)],
        out_specs=[pl.BlockSpec((tm, d), lambda i: (i, 0)),
                   pl.BlockSpec((tm, 2 * na_w), lambda i: (i, 0)),
                   pl.BlockSpec((tm // NA_BLOCK, na_w, NA_BLOCK), lambda i: (i, 0, 0)),
                   pl.BlockSpec((groups, tm // CHUNK, cw), lambda i: (0, i, 0))],
        out_shape=[jax.ShapeDtypeStruct((t, d), jnp.float32),
                   jax.ShapeDtypeStruct((t, 2 * na_w), _MXU_DTYPE),
                   jax.ShapeDtypeStruct((t // NA_BLOCK, na_w, NA_BLOCK), _MXU_DTYPE),
                   jax.ShapeDtypeStruct((groups, t // CHUNK, cw), jnp.float32)],
        scratch_shapes=[pltpu.VMEM((tm, ff), _MXU_DTYPE), pltpu.VMEM((s5_w // LANES, tm, LANES), jnp.float32)],
        compiler_params=pltpu.CompilerParams(dimension_semantics=("arbitrary",),
                                             vmem_limit_bytes=VMEM_LIMIT_BYTES),
        name="ffn_in",
    )(x2d, mod, lng, lnb, wup, wdn, win)


def _toeplitz_kernel(lhs_ref, rhs_ref, o_ref):
    width = o_ref.shape[2]
    lane = lax.broadcasted_iota(jnp.int32, (1, width), 1)
    for g in range(o_ref.shape[0]):
        k_f = _mm_split(lhs_ref[0, g], rhs_ref[0, g])
        k_b = _mm_split(lhs_ref[1, g], rhs_ref[1, g])
        for i in range(CHUNK):
            lo, hi = i * S5_CG, (i + 1) * S5_CG
            fwd = k_f if lo == 0 else jnp.where(lane >= lo, pltpu.roll(k_f, lo, axis=1), 0.0)
            bwd = k_b if hi == width else jnp.where(lane < hi, pltpu.roll(k_b, hi, axis=1), 0.0)
            o_ref[g, lo:hi, :] = (fwd + bwd).astype(o_ref.dtype)


def _toeplitz(lhs, rhs):
    _, g, cg, kdim = lhs.shape
    width = rhs.shape[3]
    gb = _pick_tile(g, 8)
    return pl.pallas_call(
        _toeplitz_kernel,
        grid=(g // gb,),
        in_specs=[pl.BlockSpec((2, gb, cg, kdim), lambda k: (0, k, 0, 0)),
                  pl.BlockSpec((2, gb, kdim, width), lambda k: (0, k, 0, 0))],
        out_specs=pl.BlockSpec((gb, width, width), lambda k: (k, 0, 0)),
        out_shape=jax.ShapeDtypeStruct((g, width, width), _MXU_DTYPE),
        compiler_params=pltpu.CompilerParams(dimension_semantics=("arbitrary",)),
        name="s5_toeplitz",
    )(lhs, rhs)


def _s5_tables(a_re, a_im, log_dt, b_re, b_im, c_re, c_im, d_skip):
    g, p = a_re.shape[1], a_re.shape[2]
    f32 = jnp.float32
    a_re, a_im = a_re.astype(f32), a_im.astype(f32)
    dt = jnp.exp(log_dt.astype(f32))[..., None]
    lr, li = a_re * dt, a_im * dt

    width = CHUNK * S5_CG
    asc = np.arange(CHUNK)
    desc = asc[::-1].copy()
    lane_m, lane_c = np.arange(width) // S5_CG, np.arange(width) % S5_CG

    def powers(expo, lr_b, li_b):
        e = jnp.asarray(expo, f32)
        mag = jnp.exp(e * lr_b)
        return mag * jnp.cos(e * li_b), mag * jnp.sin(e * li_b)


    ab_r, ab_i = jnp.exp(lr) * jnp.cos(li), jnp.exp(lr) * jnp.sin(li)
    den = a_re ** 2 + a_im ** 2
    nr, ni = ab_r - 1.0, ab_i
    sr = ((nr * a_re + ni * a_im) / den)[:, :, None, :]
    si = ((ni * a_re - nr * a_im) / den)[:, :, None, :]
    bt_re, bt_im = jnp.swapaxes(b_re, 2, 3).astype(f32), jnp.swapaxes(b_im, 2, 3).astype(f32)
    bt_r = sr * bt_re - si * bt_im
    bt_i = sr * bt_im + si * bt_re

    expand = lax.Precision.HIGH
    tile_c = jnp.asarray(lane_c[None, :] == np.arange(S5_CG)[:, None], f32)
    ct_r = jnp.einsum('dgcp,cl->dgpl', c_re.astype(f32), tile_c, precision=expand)
    ct_i = jnp.einsum('dgcp,cl->dgpl', c_im.astype(f32), tile_c, precision=expand)

    rep_m = jnp.asarray(lane_m[None, :] == np.arange(CHUNK)[:, None], f32)

    def by_lane(expo):
        pw = powers(np.stack(expo)[:, None, None, :], lr[..., None], li[..., None])
        return tuple(jnp.einsum('dgpm,ml->dgpl', v, rep_m, precision=expand) for v in pw)

    q_r, q_i = by_lane([asc, desc])
    rhs = jnp.concatenate([ct_r * q_r - ct_i * q_i, ct_r * q_i + ct_i * q_r], axis=2)
    toep = _toeplitz(jnp.concatenate([bt_r, -bt_i], axis=3), rhs)

    v_r, v_i = by_lane([asc + 1, desc + 1])
    cs_r = ct_r * v_r - ct_i * v_i
    cs_i = -(ct_r * v_i + ct_i * v_r)

    w_r, w_i = (jnp.repeat(v, S5_CG, axis=2) for v in
                powers(np.stack([desc, asc])[:, None, :, None], lr[:, :, None, :], li[:, :, None, :]))
    rows_r, rows_i = jnp.tile(bt_r, (1, 1, CHUNK, 1)), jnp.tile(bt_i, (1, 1, CHUNK, 1))
    bs_r = w_r * rows_r - w_i * rows_i
    bs_i = w_r * rows_i + w_i * rows_r
    even = jnp.asarray((np.arange(g) % 2) == 0)[None, :, None, None]
    bs = jnp.concatenate([jnp.where(even, bs_r, 0.0), jnp.where(even, 0.0, bs_r),
                          jnp.where(even, bs_i, 0.0), jnp.where(even, 0.0, bs_i)], axis=3)
    cs = jnp.concatenate([jnp.where(even, cs_r, 0.0), jnp.where(even, 0.0, cs_r),
                          jnp.where(even, cs_i, 0.0), jnp.where(even, 0.0, cs_i)], axis=2)

    e_r, e_i = jnp.exp(CHUNK * lr), CHUNK * li
    half = 2 * p
    a_pow = jnp.stack([(e_r * jnp.cos(e_i)).reshape(2, g // 2, half),
                       (e_r * jnp.sin(e_i)).reshape(2, g // 2, half)], axis=2)
    dvec = jnp.dot(d_skip.astype(f32), tile_c, precision=_HI)[:, None, :]
    return toep, bs.astype(_MXU_DTYPE), cs.astype(_MXU_DTYPE), a_pow, dvec


def _row_pitch(n):
    return n + (4 - n) % 8


def _s5_kernel(nb, n_cast, xl_ref, xc_ref, toep_ref, bs_ref, cs_ref, apow_ref, dvec_ref, *refs):
    cast_in, (y_ref, *cast_out), (zc_s, zl_s, h_s) = refs[:n_cast], refs[n_cast:2 * n_cast + 1], refs[2 * n_cast + 1:]
    for src, dst in zip(cast_in, cast_out):
        dst[...] = src[...].astype(dst.dtype)
    rows_l = xl_ref.shape[1]
    rows_c = xc_ref.shape[1]
    n_l, n_c = rows_l // nb, rows_c // nb
    pitch_l, pitch_c = _row_pitch(n_l), _row_pitch(n_c)
    half = zl_s.shape[3]
    xl = [xl_ref[g] for g in range(2)]
    xlb = [v.astype(_MXU_DTYPE) for v in xl]
    xcb = [xc_ref[g].astype(_MXU_DTYPE) for g in range(2)]

    for d in range(2):
        zc = _mm(xcb[0], bs_ref[d, 0]) + _mm(xcb[1], bs_ref[d, 1])
        zl = _mm(xlb[0], bs_ref[d, 0]) + _mm(xlb[1], bs_ref[d, 1])
        for ri in range(2):
            for b in range(nb):
                zc_s[d, ri, b * pitch_c:b * pitch_c + n_c, :] = zc[b * n_c:(b + 1) * n_c, ri * half:(ri + 1) * half]
                zl_s[d, ri, b * pitch_l:b * pitch_l + n_l, :] = zl[b * n_l:(b + 1) * n_l, ri * half:(ri + 1) * half]

    a_r = [jnp.broadcast_to(apow_ref[d, 0, 0:1, :], (nb, half)) for d in range(2)]
    a_i = [jnp.broadcast_to(apow_ref[d, 0, 1:2, :], (nb, half)) for d in range(2)]

    def step(d, n, latent, h_r, h_i):
        if latent:
            h_s[d, 0, pl.ds(n, nb, stride=pitch_l), :] = h_r
            h_s[d, 1, pl.ds(n, nb, stride=pitch_l), :] = h_i
            z_r = zl_s[d, 0, pl.ds(n, nb, stride=pitch_l), :]
            z_i = zl_s[d, 1, pl.ds(n, nb, stride=pitch_l), :]
        else:
            z_r = zc_s[d, 0, pl.ds(n, nb, stride=pitch_c), :]
            z_i = zc_s[d, 1, pl.ds(n, nb, stride=pitch_c), :]
        return (a_r[d] * h_r - a_i[d] * h_i + z_r,
                a_r[d] * h_i + a_i[d] * h_r + z_i)

    def ctx_body(n, carry):
        f_r, f_i, b_r, b_i = carry
        f_r, f_i = step(0, n, False, f_r, f_i)
        b_r, b_i = step(1, n_c - 1 - n, False, b_r, b_i)
        return f_r, f_i, b_r, b_i

    def lat_body(n, carry):
        f_r, f_i, b_r, b_i = carry
        f_r, f_i = step(0, n, True, f_r, f_i)
        b_r, b_i = step(1, n_l - 1 - n, True, b_r, b_i)
        return f_r, f_i, b_r, b_i

    zero = jnp.zeros((nb, half), jnp.float32)
    carry = lax.fori_loop(0, n_c, ctx_body, (zero, zero, zero, zero), unroll=SCAN_UNROLL)
    lax.fori_loop(0, n_l, lat_body, carry, unroll=SCAN_UNROLL)

    def plane(d, ri):
        return jnp.concatenate([h_s[d, ri, b * pitch_l:b * pitch_l + n_l, :] for b in range(nb)], axis=0)

    hb = [jnp.concatenate([plane(d, 0), plane(d, 1)], axis=1).astype(_MXU_DTYPE) for d in range(2)]
    for g in range(2):
        y = _mm(xlb[g], toep_ref[g]) + _mm(hb[0], cs_ref[0, g]) + _mm(hb[1], cs_ref[1, g])
        y_ref[g] = y + xl[g] * dvec_ref[g]


def _s5(xl, xc, toep, bs, cs, a_pow, dvec, nb, to_cast):
    g, rows_l, w = xl.shape
    rows_c = xc.shape[1]
    sw = bs.shape[3]
    steps = g // 2
    slabs = [m.shape[0] // steps for m in to_cast]
    assert all(m.shape[0] == s * steps and s % (2 * SUBLANES) == 0 for m, s in zip(to_cast, slabs))
    cast_specs = [pl.BlockSpec((s, m.shape[1]), lambda k: (k, 0)) for m, s in zip(to_cast, slabs)]
    return pl.pallas_call(
        functools.partial(_s5_kernel, nb, len(to_cast)),
        grid=(steps,),
        in_specs=[pl.BlockSpec((2, rows_l, w), lambda k: (k, 0, 0)),
                  pl.BlockSpec((2, rows_c, w), lambda k: (k, 0, 0)),
                  pl.BlockSpec((2, w, w), lambda k: (k, 0, 0)),
                  pl.BlockSpec((2, 2, w, sw), lambda k: (0, k, 0, 0)),
                  pl.BlockSpec((2, 2, sw, w), lambda k: (0, k, 0, 0)),
                  pl.BlockSpec((2, 1, 2, sw // 2), lambda k: (0, k, 0, 0)),
                  pl.BlockSpec((2, 1, w), lambda k: (k, 0, 0))] + cast_specs,
        out_specs=[pl.BlockSpec((2, rows_l, w), lambda k: (k, 0, 0))] + cast_specs,
        out_shape=[jax.ShapeDtypeStruct((g, rows_l, w), jnp.float32)]
        + [jax.ShapeDtypeStruct(m.shape, _MXU_DTYPE) for m in to_cast],
        scratch_shapes=[pltpu.VMEM((2, 2, nb * _row_pitch(rows_c // nb), sw // 2), jnp.float32),
                        pltpu.VMEM((2, 2, nb * _row_pitch(rows_l // nb), sw // 2), jnp.float32),
                        pltpu.VMEM((2, 2, nb * _row_pitch(rows_l // nb), sw // 2), jnp.float32)],
        compiler_params=pltpu.CompilerParams(dimension_semantics=("arbitrary",),
                                             vmem_limit_bytes=VMEM_LIMIT_BYTES),
        name="s5",
    )(xl, xc, toep, bs, cs, a_pow, dvec, *to_cast)


def _na_row_plan():
    last_q = NA_KROWS - NA_QROWS
    plan = []
    for r0s, qrs in ((lambda lr: 0, lambda lr: lr),
                     (lambda lr: lr, lambda lr: NA_KR // 2 + lr),
                     (lambda lr: last_q - NA_KR // 2, lambda lr: last_q + lr)):
        plan.append([[kl - qrs(lr) + (NA_KR - 1) if r0s(lr) <= kl < r0s(lr) + NA_KR else None
                      for kl in range(NA_KROWS)] for lr in range(NA_QROWS)])
    return plan


def _na_bias_fill(col_ref, bias_s):
    plan = _na_row_plan()
    lane = lax.broadcasted_iota(jnp.int32, (1, 2 * GRID_W), 1)
    masked = jnp.full((GRID_W, 2 * GRID_W), MASK_VALUE, jnp.float32)
    for head in range(bias_s.shape[1]):
        for kind in range(len(plan)):
            for kl in range(NA_KROWS):
                for lp in range(NA_QROWS // 2):
                    a0, a1 = plan[kind][2 * lp][kl], plan[kind][2 * lp + 1][kl]
                    t0 = masked if a0 is None else col_ref[head, a0]
                    t1 = masked if a1 is None else col_ref[head, a1]
                    tile = t0 if a0 == a1 else jnp.where(lane < GRID_W, t0, t1)
                    bias_s[kind, head, kl * GRID_W:(kl + 1) * GRID_W, lp * 2 * GRID_W:(lp + 1) * 2 * GRID_W] = tile


def _na_bias_columns(rpb):
    kc = np.arange(GRID_W)[:, None]
    qc = np.arange(GRID_W)[None, :]
    c0 = np.clip(qc - NA_KC // 2, 0, GRID_W - NA_KC)
    col_ok = (kc >= c0) & (kc < c0 + NA_KC)
    sel_c = ((kc - qc + (NA_KC - 1))[:, :, None] == np.arange(2 * NA_KC - 1)) & col_ok[:, :, None]
    by_col = jnp.einsum('hab,kqb->hakq', rpb.astype(jnp.float32), jnp.asarray(sel_c, jnp.float32), precision=_HI)
    by_col = jnp.where(jnp.asarray(col_ok), by_col * LOG2_E, MASK_VALUE)
    return jnp.concatenate([by_col, by_col], axis=3)


def _row_max(x):
    while x.shape[0] % (2 * SUBLANES) == 0:
        half = x.shape[0] // 2
        x = jnp.maximum(x[:half], x[half:])
    return jnp.max(x, axis=0, keepdims=True)


def _na_kernel(rows, q_ref, k_ref, vt_ref, kc_ref, vct_ref, col_ref, o_ref, bias_ref):
    @pl.when((pl.program_id(0) == 0) & (pl.program_id(1) == 0))
    def _():
        _na_bias_fill(col_ref, bias_ref)

    n_rb = rows // NA_QROWS
    nk = NA_KROWS * GRID_W
    pair = 2 * NA_DH
    lane_first = lax.broadcasted_iota(jnp.int32, (1, pair), 1) < NA_DH
    row_first = lax.broadcasted_iota(jnp.int32, (pair, 1), 0) < NA_DH
    n_heads = q_ref.shape[2] // NA_DH
    n_blocks = q_ref.shape[1] // NA_BLOCK

    def block(j):
        rb = pl.program_id(1) * n_blocks + j
        kind = jnp.where(rb == 0, 0, jnp.where(rb == n_rb - 1, 2, 1))
        first_row = jnp.clip(rb * NA_QROWS - NA_KR // 2, 0, rows - NA_KROWS)
        return kind, pl.multiple_of(first_row * GRID_W, GRID_W), first_row // NA_QROWS

    geometry = [block(j) for j in range(n_blocks)]

    def scores(u):
        j, h = divmod(u, n_heads)
        kind, start, _ = geometry[j]
        cols = slice(h // 2 * pair, (h // 2 + 1) * pair)
        lane_mine = lane_first if h % 2 == 0 else jnp.logical_not(lane_first)
        qm = jnp.where(lane_mine, q_ref[0, j * NA_BLOCK:(j + 1) * NA_BLOCK, cols],
                       jnp.zeros((NA_BLOCK, pair), q_ref.dtype))
        s_w = _mm_nt(k_ref[0, pl.ds(start, nk), cols], qm) + bias_ref[kind, h]
        s_c = _mm_nt(kc_ref[0, :, cols], qm)
        return s_w, s_c

    def weights(s_w, s_c):
        mx = jnp.maximum(_row_max(s_w), _row_max(s_c))
        return jnp.exp2((s_w - mx).astype(vt_ref.dtype)), jnp.exp2((s_c - mx).astype(vt_ref.dtype))

    def values(u, p_w, p_c):
        j, h = divmod(u, n_heads)
        blk0 = geometry[j][2]
        cols = slice(h // 2 * pair, (h // 2 + 1) * pair)
        row_mine = row_first if h % 2 == 0 else jnp.logical_not(row_first)
        vtw = jnp.concatenate([vt_ref[blk0 + t, cols, :] for t in range(nk // NA_BLOCK)], axis=1)
        vtc = jnp.concatenate([vct_ref[t, cols, :] for t in range(vct_ref.shape[0])], axis=1)
        o = (_mm(jnp.where(row_mine, vtw, jnp.ones_like(vtw)), p_w)
             + _mm(jnp.where(row_mine, vtc, jnp.ones_like(vtc)), p_c))
        num, den = (o[:NA_DH], o[NA_DH:]) if h % 2 == 0 else (o[NA_DH:], o[:NA_DH])
        return num / den

    n_units = n_blocks * n_heads
    s = {0: scores(0), 1: scores(1)}
    p = {0: weights(*s.pop(0))}
    outs = []
    for u in range(n_units):
        if u + 2 < n_units:
            s[u + 2] = scores(u + 2)
        if u + 1 < n_units:
            p[u + 1] = weights(*s.pop(u + 1))
        outs.append(values(u, *p.pop(u)))
        if len(outs) == n_heads:
            j = u // n_heads
            o_ref[0, j * NA_BLOCK:(j + 1) * NA_BLOCK, :] = jnp.concatenate(outs, axis=0).T.astype(o_ref.dtype)
            outs = []


def _na(qk, vt, qk_c, vt_c, bias_cols, na_w):
    b, s, _ = qk.shape
    l = qk_c.shape[1]
    rows = s // GRID_W
    tq = NA_STEP_BLOCKS * NA_BLOCK
    n_kinds = len(_na_row_plan())
    return pl.pallas_call(
        functools.partial(_na_kernel, rows),
        grid=(b, s // tq),
        in_specs=[pl.BlockSpec((1, tq, na_w), lambda bi, rb: (bi, rb, 0)),
                  pl.BlockSpec((1, s, na_w), lambda bi, rb: (bi, 0, 1)),
                  pl.BlockSpec((s // NA_BLOCK, na_w, NA_BLOCK), lambda bi, rb: (bi, 0, 0)),
                  pl.BlockSpec((1, l, na_w), lambda bi, rb: (bi, 0, 1)),
                  pl.BlockSpec((l // NA_BLOCK, na_w, NA_BLOCK), lambda bi, rb: (bi, 0, 0)),
                  _const_spec(bias_cols.shape)],
        out_specs=pl.BlockSpec((1, tq, na_w), lambda bi, rb: (bi, rb, 0)),
        out_shape=jax.ShapeDtypeStruct((b, s, na_w), _MXU_DTYPE),
        scratch_shapes=[pltpu.VMEM((n_kinds, na_w // NA_DH, NA_KROWS * GRID_W, NA_BLOCK), jnp.float32)],
        compiler_params=pltpu.CompilerParams(
            dimension_semantics=("arbitrary", "arbitrary"),
            vmem_limit_bytes=VMEM_LIMIT_BYTES),
        name="na",
    )(qk, qk, vt, qk_c, vt_c, bias_cols)


def _out_ffn_kernel(alpha, x1_ref, yna_ref, ys5_ref, mod_a_ref, mod_b_ref, lng_ref, lnb_ref, wglu_ref, bglu_ref,
                    wout_ref, wup_ref, wdn_ref, o_ref, p_s, y_s, h_s, x2_s):
    @pl.when(pl.program_id(0) == 0)
    def _():
        h_s[...] = jnp.zeros_like(h_s)
        x2_s[...] = jnp.zeros_like(x2_s)

    mod_a = mod_a_ref[0]
    na_w = yna_ref.shape[1]
    stage_a = {}

    def relayout():
        _chunks_to_tokens(ys5_ref, y_s)

    def gated():
        gl = jax.nn.gelu(jnp.concatenate([y_s[v] for v in range(y_s.shape[0])], axis=1))
        gate = jax.nn.sigmoid(_mm(gl.astype(p_s.dtype), wglu_ref[...]) + bglu_ref[...])
        stage_a["glu"] = (gl * gate).astype(p_s.dtype)

    def mixed():
        y = _mm(yna_ref[...], wout_ref[:na_w, :]) + _mm(stage_a["glu"], wout_ref[na_w:, :])
        x2 = _layer_norm(alpha * x1_ref[...] + mod_a[5:6] * y, lng_ref[1:2], lnb_ref[1:2])
        stage_a["x2"] = x2
        stage_a["h"] = (x2 * (1.0 + mod_a[7:8]) + mod_a[6:7]).astype(h_s.dtype)

    mod_b = mod_b_ref[0]
    _swiglu_hidden(h_s[...], wup_ref, p_s, between=(relayout, gated, mixed))
    n_blk = o_ref.shape[0] // NA_BLOCK
    fs = [_mm(p_s[t * NA_BLOCK:(t + 1) * NA_BLOCK, :], wdn_ref[...]) for t in range(n_blk)]
    for t in range(n_blk):
        rows = slice(t * NA_BLOCK, (t + 1) * NA_BLOCK)
        o_ref[rows, :] = _layer_norm(alpha * x2_s[rows, :] + (0.5 * mod_b[8:9]) * fs[t], lng_ref[2:3], lnb_ref[2:3])

    x2_s[...] = stage_a["x2"]
    h_s[...] = stage_a["h"]


def _out_ffn(x1, yna, ys5, mod, tiles_per_mod, lng, lnb, wglu, bglu, wout, wup, wdn, alpha, tm):
    t, d = x1.shape
    ff = wdn.shape[0]
    na_w = yna.shape[1]
    groups, _, cw = ys5.shape
    s5_w = groups * S5_CG
    n = t // tm

    def cur(i):
        return jnp.minimum(i, n - 1)

    def prev(i):
        return jnp.maximum(i - 1, 0)

    return pl.pallas_call(
        functools.partial(_out_ffn_kernel, alpha),
        grid=(n + 1,),
        in_specs=[pl.BlockSpec((tm, d), lambda i: (cur(i), 0)),
                  pl.BlockSpec((tm, na_w), lambda i: (cur(i), 0)),
                  pl.BlockSpec((groups, tm // CHUNK, cw), lambda i: (0, cur(i), 0)),
                  pl.BlockSpec((1, N_MOD, d), lambda i: (cur(i) // tiles_per_mod, 0, 0)),
                  pl.BlockSpec((1, N_MOD, d), lambda i: (prev(i) // tiles_per_mod, 0, 0)),
                  _const_spec(lng.shape), _const_spec(lnb.shape),
                  _const_spec(wglu.shape), _const_spec(bglu.shape), _const_spec(wout.shape),
                  _const_spec(wup.shape), _const_spec(wdn.shape)],
        out_specs=pl.BlockSpec((tm, d), lambda i: (prev(i), 0)),
        out_shape=jax.ShapeDtypeStruct((t, d), jnp.float32),
        scratch_shapes=[pltpu.VMEM((tm, ff), _MXU_DTYPE), pltpu.VMEM((s5_w // LANES, tm, LANES), jnp.float32),
                        pltpu.VMEM((tm, d), _MXU_DTYPE), pltpu.VMEM((tm, d), jnp.float32)],
        compiler_params=pltpu.CompilerParams(dimension_semantics=("arbitrary",),
                                             vmem_limit_bytes=VMEM_LIMIT_BYTES),
        name="out_ffn",
    )(x1, yna, ys5, mod, mod, lng, lnb, wglu, bglu, wout, wup, wdn)


def _pick_tile(n, target):
    t = min(n, target)
    while n % t:
        t //= 2
    return t


def kernel(x, c, ctx, c_ctx, w_ada, b_ada, ln_g, ln_b, ffn1_w_up, ffn1_w_down, w_in, na_rpb, s5_a_re, s5_a_im, s5_log_dt, s5_b_re, s5_b_im, s5_c_re, s5_c_im, s5_d, s5_w_glu, s5_b_glu, w_out, ffn2_w_up, ffn2_w_down):
    depth = w_ada.shape[0]
    assert depth == 1, "single-layer stack only"
    bsz, seq, d = x.shape
    l_ctx = ctx.shape[1]
    s5_w = s5_w_glu.shape[1]
    na_w = w_out.shape[1] - s5_w
    groups = s5_w // S5_CG
    rows = seq // GRID_W
    assert bsz % 8 == 0 and seq % (NA_STEP_BLOCKS * NA_BLOCK) == 0 and rows >= NA_KROWS
    assert l_ctx % NA_BLOCK == 0 and groups % 2 == 0 and s5_a_re.shape[3] == S5_P and 2 * S5_P == LANES
    assert na_w % (2 * NA_DH) == 0 and ffn1_w_down.shape[1] % MXU_TILE == 0
    alpha = (2.0 * depth) ** 0.25
    cd = _MXU_DTYPE

    n_rows = -(-(bsz + 1) // 8) * 8
    cc = jnp.zeros((n_rows, d), jnp.float32).at[:bsz].set(c).at[bsz].set(c_ctx)
    n_ada = w_ada.shape[2]
    mod = _ada(cc, w_ada[0], b_ada[0][None, :], _pick_tile(n_ada, 2304)).reshape(n_rows, N_MOD, d)

    lng, lnb = ln_g[0], ln_b[0]
    wup1, wdn1 = ffn1_w_up[0].astype(cd), ffn1_w_down[0].astype(cd)
    win = w_in[0].astype(cd)

    tm = _pick_tile(seq, 512)
    x1, qk, vt, xs = _ffn_in(x.reshape(bsz * seq, d), mod, seq // tm, lng, lnb, wup1, wdn1, win, alpha, na_w, tm)
    tmc = _pick_tile(bsz * l_ctx, 512)
    _, qk_c, vt_c, xs_c = _ffn_in(ctx.reshape(bsz * l_ctx, d), mod[bsz:bsz + 1], None, lng, lnb, wup1, wdn1, win,
                                  alpha, na_w, tmc)

    tables = _s5_tables(s5_a_re[0], s5_a_im[0], s5_log_dt[0], s5_b_re[0], s5_b_im[0], s5_c_re[0], s5_c_im[0],
                        s5_d[0])
    y_s5, wup2, wdn2, wout, wglu = _s5(xs, xs_c, *tables, bsz,
                                       (ffn2_w_up[0], ffn2_w_down[0], w_out[0], s5_w_glu[0]))

    y_na = _na(qk.reshape(bsz, seq, 2 * na_w), vt, qk_c.reshape(bsz, l_ctx, 2 * na_w), vt_c, _na_bias_columns(na_rpb[0]),
               na_w).reshape(bsz * seq, na_w)

    out = _out_ffn(x1, y_na, y_s5, mod, seq // tm, lng, lnb, wglu, s5_b_glu[0][None, :], wout, wup2, wdn2,
                   alpha, tm)
    return out.reshape(bsz, seq, d)
```

```python
import functools

import jax
import jax.numpy as jnp
import numpy as np
from jax import lax
from jax.experimental import pallas as pl
from jax.experimental.pallas import tpu as pltpu

GRID_W = 64
NA_DH = 64
NA_KR = 8
NA_KC = 16
S5_CG = 16
S5_P = 64
LN_EPS = 1e-6
N_MOD = 9

CHUNK = 16
NA_QROWS = 4
NA_KROWS = 12
NA_BLOCK = NA_QROWS * GRID_W
NA_STEP_BLOCKS = 4
MASK_VALUE = -1e30
LOG2_E = 1.4426950408889634
SCAN_UNROLL = 4

MXU_TILE = 256
LANES = 128
SUBLANES = 8
VMEM_LIMIT_BYTES = 56 * 1024 * 1024

_MXU_DTYPE = jnp.bfloat16
_HI = lax.Precision.HIGHEST


def _mm(a, b):
    return jnp.dot(a, b, preferred_element_type=jnp.float32)


def _mm_nt(a, b):
    return lax.dot_general(a, b, (((1,), (1,)), ((), ())), preferred_element_type=jnp.float32)


def _const_spec(shape):
    nd = len(shape)
    return pl.BlockSpec(shape, lambda *_: (0,) * nd, pipeline_mode=pl.Buffered(1))


def _layer_norm(r, g, b):
    mu = jnp.mean(r, axis=-1, keepdims=True)
    d = r - mu
    var = jnp.mean(d * d, axis=-1, keepdims=True)
    return d * lax.rsqrt(var + LN_EPS) * g + b


def _swiglu_hidden(h, wup_ref, p_s, between=()):
    ff = p_s.shape[1]
    n_chunks = ff // MXU_TILE
    slots = [(k + 1) * n_chunks // (len(between) + 1) for k in range(len(between))]
    for j in range(n_chunks):
        for slot, fn in zip(slots, between):
            if slot == j:
                fn()
        lo = j * MXU_TILE
        a = _mm(h, wup_ref[:, lo:lo + MXU_TILE])
        g = _mm(h, wup_ref[:, ff + lo:ff + lo + MXU_TILE])
        p_s[:, lo:lo + MXU_TILE] = (g * jax.nn.sigmoid(g) * a).astype(p_s.dtype)


def _swiglu(h, wup_ref, wdn_ref, p_s, between=()):
    _swiglu_hidden(h, wup_ref, p_s, between)
    return _mm(p_s[...], wdn_ref[...])


def _block_transpose(tiles):
    per = len(tiles)
    blk = lax.broadcasted_iota(jnp.int32, (1, LANES), 1) // S5_CG
    tiles = list(tiles)
    d = per // 2
    while d:
        upper = (blk & d) != 0
        for a in range(per):
            if a & d:
                continue
            lo, hi = tiles[a], tiles[a + d]
            tiles[a] = jnp.where(upper, pltpu.roll(hi, S5_CG * d, axis=1), lo)
            tiles[a + d] = jnp.where(upper, hi, pltpu.roll(lo, LANES - S5_CG * d, axis=1))
        d //= 2
    return tiles


def _tokens_to_chunks(u_s, xs_ref, row0, n_rows):
    n_ch = n_rows // CHUNK
    ch0 = row0 // CHUNK
    per = LANES // S5_CG
    for v in range(u_s.shape[0]):
        for w in range(CHUNK // per):
            by_group = _block_transpose(
                [u_s[v, pl.ds(row0 + per * w + jj, n_ch, stride=CHUNK), :] for jj in range(per)])
            for gg in range(per):
                xs_ref[per * v + gg, ch0:ch0 + n_ch, w * LANES:(w + 1) * LANES] = by_group[gg].astype(xs_ref.dtype)


def _chunks_to_tokens(ys_ref, y_s):
    n_ch = y_s.shape[1] // CHUNK
    per = LANES // S5_CG
    for v in range(y_s.shape[0]):
        for w in range(CHUNK // per):
            by_phase = _block_transpose([ys_ref[per * v + gg, :, w * LANES:(w + 1) * LANES] for gg in range(per)])
            for jj in range(per):
                y_s[v, pl.ds(per * w + jj, n_ch, stride=CHUNK), :] = by_phase[jj]


def _ada_kernel(c_ref, w_ref, b_ref, o_ref):
    cc = c_ref[...]
    s = cc * jax.nn.sigmoid(cc)
    w = w_ref[...]
    s_hi = s.astype(_MXU_DTYPE)
    s_lo = (s - s_hi.astype(jnp.float32)).astype(_MXU_DTYPE)
    w_hi = w.astype(_MXU_DTYPE)
    w_lo = (w - w_hi.astype(jnp.float32)).astype(_MXU_DTYPE)
    o_ref[...] = _mm(s_hi, w_hi) + (_mm(s_lo, w_hi) + _mm(s_hi, w_lo)) + b_ref[...]


def _ada(cc, w, b, bn):
    rows, d = cc.shape
    n = w.shape[1]
    return pl.pallas_call(
        _ada_kernel,
        grid=(n // bn,),
        in_specs=[pl.BlockSpec((rows, d), lambda j: (0, 0)),
                  pl.BlockSpec((d, bn), lambda j: (0, j)),
                  pl.BlockSpec((1, bn), lambda j: (0, j))],
        out_specs=pl.BlockSpec((rows, bn), lambda j: (0, j)),
        out_shape=jax.ShapeDtypeStruct((rows, n), jnp.float32),
        compiler_params=pltpu.CompilerParams(dimension_semantics=("arbitrary",),
                                             vmem_limit_bytes=VMEM_LIMIT_BYTES),
        name="ada",
    )(cc, w, b)


def _ffn_in_kernel(alpha, na_w, x_ref, mod_ref, lng_ref, lnb_ref, wup_ref, wdn_ref, win_ref,
                   x1_ref, qk_ref, vt_ref, xs_ref, p_s, u_s):
    mod = mod_ref[0]
    h = (x_ref[...] * (1.0 + mod[1:2]) + mod[0:1]).astype(p_s.dtype)
    _swiglu_hidden(h, wup_ref, p_s)
    n_blk = vt_ref.shape[0]
    ys = [_mm(p_s[t * NA_BLOCK:(t + 1) * NA_BLOCK, :], wdn_ref[...]) for t in range(n_blk)]
    for t in range(n_blk):
        rows = slice(t * NA_BLOCK, (t + 1) * NA_BLOCK)
        x1 = _layer_norm(alpha * x_ref[rows, :] + (0.5 * mod[2:3]) * ys[t], lng_ref[0:1], lnb_ref[0:1])
        x1_ref[rows, :] = x1
        h2 = (x1 * (1.0 + mod[4:5]) + mod[3:4]).astype(p_s.dtype)
        pu = _mm(h2, win_ref[:, 3 * na_w:])
        for v in range(u_s.shape[0]):
            u_s[v, rows, :] = pu[:, v * LANES:(v + 1) * LANES]
        _tokens_to_chunks(u_s, xs_ref, t * NA_BLOCK, NA_BLOCK)
        pr = _mm(h2, win_ref[:, :3 * na_w])
        qk_ref[rows, :na_w] = (pr[:, :na_w] * (NA_DH ** -0.5 * LOG2_E)).astype(qk_ref.dtype)
        qk_ref[rows, na_w:] = pr[:, na_w:2 * na_w].astype(qk_ref.dtype)
        vt_ref[t] = pr[:, 2 * na_w:].T.astype(vt_ref.dtype)


def _ffn_in(x2d, mod, tiles_per_mod, lng, lnb, wup, wdn, win, alpha, na_w, tm):
    t, d = x2d.shape
    ff = wdn.shape[0]
    ncol = win.shape[1]
    s5_w = ncol - 3 * na_w
    groups, cw = s5_w // S5_CG, CHUNK * S5_CG
    if tiles_per_mod is None:
        mod_map = lambda i: (0, 0, 0)
    else:
        mod_map = lambda i: (i // tiles_per_mod, 0, 0)
    return pl.pallas_call(
        functools.partial(_ffn_in_kernel, alpha, na_w),
        grid=(t // tm,),
        in_specs=[pl.BlockSpec((tm, d), lambda i: (i, 0)),
                  pl.BlockSpec((1, N_MOD, d), mod_map),
                  _const_spec(lng.shape), _const_spec(lnb.shape),
                  _const_spec(wup.shape), _const_spec(wdn.shape), _const_spec(win.shape)],
        out_specs=[pl.BlockSpec((tm, d), lambda i: (i, 0)),
                   pl.BlockSpec((tm, 2 * na_w), lambda i: (i, 0)),
                   pl.BlockSpec((tm // NA_BLOCK, na_w, NA_BLOCK), lambda i: (i, 0, 0)),
                   pl.BlockSpec((groups, tm // CHUNK, cw), lambda i: (0, i, 0))],
        out_shape=[jax.ShapeDtypeStruct((t, d), jnp.float32),
                   jax.ShapeDtypeStruct((t, 2 * na_w), _MXU_DTYPE),
                   jax.ShapeDtypeStruct((t // NA_BLOCK, na_w, NA_BLOCK), _MXU_DTYPE),
                   jax.ShapeDtypeStruct((groups, t // CHUNK, cw), jnp.float32)],
        scratch_shapes=[pltpu.VMEM((tm, ff), _MXU_DTYPE), pltpu.VMEM((s5_w // LANES, tm, LANES), jnp.float32)],
        compiler_params=pltpu.CompilerParams(dimension_semantics=("arbitrary",),
                                             vmem_limit_bytes=VMEM_LIMIT_BYTES),
        name="ffn_in",
    )(x2d, mod, lng, lnb, wup, wdn, win)


def _toeplitz_kernel(lhs_ref, rhs_ref, o_ref):
    width = o_ref.shape[2]
    lane = lax.broadcasted_iota(jnp.int32, (1, width), 1)
    for g in range(o_ref.shape[0]):
        k_f = jnp.dot(lhs_ref[0, g], rhs_ref[0, g], precision=_HI, preferred_element_type=jnp.float32)
        k_b = jnp.dot(lhs_ref[1, g], rhs_ref[1, g], precision=_HI, preferred_element_type=jnp.float32)
        for i in range(CHUNK):
            lo, hi = i * S5_CG, (i + 1) * S5_CG
            fwd = k_f if lo == 0 else jnp.where(lane >= lo, pltpu.roll(k_f, lo, axis=1), 0.0)
            bwd = k_b if hi == width else jnp.where(lane < hi, pltpu.roll(k_b, hi, axis=1), 0.0)
            o_ref[g, lo:hi, :] = (fwd + bwd).astype(o_ref.dtype)


def _toeplitz(lhs, rhs):
    _, g, cg, kdim = lhs.shape
    width = rhs.shape[3]
    gb = _pick_tile(g, 8)
    return pl.pallas_call(
        _toeplitz_kernel,
        grid=(g // gb,),
        in_specs=[pl.BlockSpec((2, gb, cg, kdim), lambda k: (0, k, 0, 0)),
                  pl.BlockSpec((2, gb, kdim, width), lambda k: (0, k, 0, 0))],
        out_specs=pl.BlockSpec((gb, width, width), lambda k: (k, 0, 0)),
        out_shape=jax.ShapeDtypeStruct((g, width, width), _MXU_DTYPE),
        compiler_params=pltpu.CompilerParams(dimension_semantics=("arbitrary",)),
        name="s5_toeplitz",
    )(lhs, rhs)


def _s5_tables(a_re, a_im, log_dt, b_re, b_im, c_re, c_im, d_skip):
    g, p = a_re.shape[1], a_re.shape[2]
    f32 = jnp.float32
    a_re, a_im = a_re.astype(f32), a_im.astype(f32)
    dt = jnp.exp(log_dt.astype(f32))[..., None]
    lr, li = a_re * dt, a_im * dt

    width = CHUNK * S5_CG
    asc = np.arange(CHUNK)
    desc = asc[::-1].copy()
    lane_m, lane_c = np.arange(width) // S5_CG, np.arange(width) % S5_CG

    def powers(expo, lr_b, li_b):
        e = jnp.asarray(expo, f32)
        mag = jnp.exp(e * lr_b)
        return mag * jnp.cos(e * li_b), mag * jnp.sin(e * li_b)


    ab_r, ab_i = jnp.exp(lr) * jnp.cos(li), jnp.exp(lr) * jnp.sin(li)
    den = a_re ** 2 + a_im ** 2
    nr, ni = ab_r - 1.0, ab_i
    sr = ((nr * a_re + ni * a_im) / den)[:, :, None, :]
    si = ((ni * a_re - nr * a_im) / den)[:, :, None, :]
    bt_re, bt_im = jnp.swapaxes(b_re, 2, 3).astype(f32), jnp.swapaxes(b_im, 2, 3).astype(f32)
    bt_r = sr * bt_re - si * bt_im
    bt_i = sr * bt_im + si * bt_re

    expand = lax.Precision.HIGH
    tile_c = jnp.asarray(lane_c[None, :] == np.arange(S5_CG)[:, None], f32)
    ct_r = jnp.einsum('dgcp,cl->dgpl', c_re.astype(f32), tile_c, precision=expand)
    ct_i = jnp.einsum('dgcp,cl->dgpl', c_im.astype(f32), tile_c, precision=expand)

    rep_m = jnp.asarray(lane_m[None, :] == np.arange(CHUNK)[:, None], f32)

    def by_lane(expo):
        pw = powers(np.stack(expo)[:, None, None, :], lr[..., None], li[..., None])
        return tuple(jnp.einsum('dgpm,ml->dgpl', v, rep_m, precision=expand) for v in pw)

    q_r, q_i = by_lane([asc, desc])
    rhs = jnp.concatenate([ct_r * q_r - ct_i * q_i, ct_r * q_i + ct_i * q_r], axis=2)
    toep = _toeplitz(jnp.concatenate([bt_r, -bt_i], axis=3), rhs)

    v_r, v_i = by_lane([asc + 1, desc + 1])
    cs_r = ct_r * v_r - ct_i * v_i
    cs_i = -(ct_r * v_i + ct_i * v_r)

    w_r, w_i = (jnp.repeat(v, S5_CG, axis=2) for v in
                powers(np.stack([desc, asc])[:, None, :, None], lr[:, :, None, :], li[:, :, None, :]))
    rows_r, rows_i = jnp.tile(bt_r, (1, 1, CHUNK, 1)), jnp.tile(bt_i, (1, 1, CHUNK, 1))
    bs_r = w_r * rows_r - w_i * rows_i
    bs_i = w_r * rows_i + w_i * rows_r
    even = jnp.asarray((np.arange(g) % 2) == 0)[None, :, None, None]
    bs = jnp.concatenate([jnp.where(even, bs_r, 0.0), jnp.where(even, 0.0, bs_r),
                          jnp.where(even, bs_i, 0.0), jnp.where(even, 0.0, bs_i)], axis=3)
    cs = jnp.concatenate([jnp.where(even, cs_r, 0.0), jnp.where(even, 0.0, cs_r),
                          jnp.where(even, cs_i, 0.0), jnp.where(even, 0.0, cs_i)], axis=2)

    e_r, e_i = jnp.exp(CHUNK * lr), CHUNK * li
    half = 2 * p
    a_pow = jnp.stack([(e_r * jnp.cos(e_i)).reshape(2, g // 2, half),
                       (e_r * jnp.sin(e_i)).reshape(2, g // 2, half)], axis=2)
    dvec = jnp.dot(d_skip.astype(f32), tile_c, precision=_HI)[:, None, :]
    return toep, bs.astype(_MXU_DTYPE), cs.astype(_MXU_DTYPE), a_pow, dvec


def _row_pitch(n):
    return n + (4 - n) % 8


def _s5_kernel(nb, n_cast, xl_ref, xc_ref, toep_ref, bs_ref, cs_ref, apow_ref, dvec_ref, *refs):
    cast_in, (y_ref, *cast_out), (zc_s, zl_s, h_s) = refs[:n_cast], refs[n_cast:2 * n_cast + 1], refs[2 * n_cast + 1:]
    for src, dst in zip(cast_in, cast_out):
        dst[...] = src[...].astype(dst.dtype)
    rows_l = xl_ref.shape[1]
    rows_c = xc_ref.shape[1]
    n_l, n_c = rows_l // nb, rows_c // nb
    pitch_l, pitch_c = _row_pitch(n_l), _row_pitch(n_c)
    half = zl_s.shape[3]
    xl = [xl_ref[g] for g in range(2)]
    xlb = [v.astype(_MXU_DTYPE) for v in xl]
    xcb = [xc_ref[g].astype(_MXU_DTYPE) for g in range(2)]

    for d in range(2):
        zc = _mm(xcb[0], bs_ref[d, 0]) + _mm(xcb[1], bs_ref[d, 1])
        zl = _mm(xlb[0], bs_ref[d, 0]) + _mm(xlb[1], bs_ref[d, 1])
        for ri in range(2):
            for b in range(nb):
                zc_s[d, ri, b * pitch_c:b * pitch_c + n_c, :] = zc[b * n_c:(b + 1) * n_c, ri * half:(ri + 1) * half]
                zl_s[d, ri, b * pitch_l:b * pitch_l + n_l, :] = zl[b * n_l:(b + 1) * n_l, ri * half:(ri + 1) * half]

    a_r = [jnp.broadcast_to(apow_ref[d, 0, 0:1, :], (nb, half)) for d in range(2)]
    a_i = [jnp.broadcast_to(apow_ref[d, 0, 1:2, :], (nb, half)) for d in range(2)]

    def step(d, n, latent, h_r, h_i):
        if latent:
            h_s[d, 0, pl.ds(n, nb, stride=pitch_l), :] = h_r
            h_s[d, 1, pl.ds(n, nb, stride=pitch_l), :] = h_i
            z_r = zl_s[d, 0, pl.ds(n, nb, stride=pitch_l), :]
            z_i = zl_s[d, 1, pl.ds(n, nb, stride=pitch_l), :]
        else:
            z_r = zc_s[d, 0, pl.ds(n, nb, stride=pitch_c), :]
            z_i = zc_s[d, 1, pl.ds(n, nb, stride=pitch_c), :]
        return (a_r[d] * h_r - a_i[d] * h_i + z_r,
                a_r[d] * h_i + a_i[d] * h_r + z_i)

    def ctx_body(n, carry):
        f_r, f_i, b_r, b_i = carry
        f_r, f_i = step(0, n, False, f_r, f_i)
        b_r, b_i = step(1, n_c - 1 - n, False, b_r, b_i)
        return f_r, f_i, b_r, b_i

    def lat_body(n, carry):
        f_r, f_i, b_r, b_i = carry
        f_r, f_i = step(0, n, True, f_r, f_i)
        b_r, b_i = step(1, n_l - 1 - n, True, b_r, b_i)
        return f_r, f_i, b_r, b_i

    zero = jnp.zeros((nb, half), jnp.float32)
    carry = lax.fori_loop(0, n_c, ctx_body, (zero, zero, zero, zero), unroll=SCAN_UNROLL)
    lax.fori_loop(0, n_l, lat_body, carry, unroll=SCAN_UNROLL)

    def plane(d, ri):
        return jnp.concatenate([h_s[d, ri, b * pitch_l:b * pitch_l + n_l, :] for b in range(nb)], axis=0)

    hb = [jnp.concatenate([plane(d, 0), plane(d, 1)], axis=1).astype(_MXU_DTYPE) for d in range(2)]
    for g in range(2):
        y = _mm(xlb[g], toep_ref[g]) + _mm(hb[0], cs_ref[0, g]) + _mm(hb[1], cs_ref[1, g])
        y_ref[g] = y + xl[g] * dvec_ref[g]


def _s5(xl, xc, toep, bs, cs, a_pow, dvec, nb, to_cast):
    g, rows_l, w = xl.shape
    rows_c = xc.shape[1]
    sw = bs.shape[3]
    steps = g // 2
    slabs = [m.shape[0] // steps for m in to_cast]
    assert all(m.shape[0] == s * steps and s % (2 * SUBLANES) == 0 for m, s in zip(to_cast, slabs))
    cast_specs = [pl.BlockSpec((s, m.shape[1]), lambda k: (k, 0)) for m, s in zip(to_cast, slabs)]
    return pl.pallas_call(
        functools.partial(_s5_kernel, nb, len(to_cast)),
        grid=(steps,),
        in_specs=[pl.BlockSpec((2, rows_l, w), lambda k: (k, 0, 0)),
                  pl.BlockSpec((2, rows_c, w), lambda k: (k, 0, 0)),
                  pl.BlockSpec((2, w, w), lambda k: (k, 0, 0)),
                  pl.BlockSpec((2, 2, w, sw), lambda k: (0, k, 0, 0)),
                  pl.BlockSpec((2, 2, sw, w), lambda k: (0, k, 0, 0)),
                  pl.BlockSpec((2, 1, 2, sw // 2), lambda k: (0, k, 0, 0)),
                  pl.BlockSpec((2, 1, w), lambda k: (k, 0, 0))] + cast_specs,
        out_specs=[pl.BlockSpec((2, rows_l, w), lambda k: (k, 0, 0))] + cast_specs,
        out_shape=[jax.ShapeDtypeStruct((g, rows_l, w), jnp.float32)]
        + [jax.ShapeDtypeStruct(m.shape, _MXU_DTYPE) for m in to_cast],
        scratch_shapes=[pltpu.VMEM((2, 2, nb * _row_pitch(rows_c // nb), sw // 2), jnp.float32),
                        pltpu.VMEM((2, 2, nb * _row_pitch(rows_l // nb), sw // 2), jnp.float32),
                        pltpu.VMEM((2, 2, nb * _row_pitch(rows_l // nb), sw // 2), jnp.float32)],
        compiler_params=pltpu.CompilerParams(dimension_semantics=("arbitrary",),
                                             vmem_limit_bytes=VMEM_LIMIT_BYTES),
        name="s5",
    )(xl, xc, toep, bs, cs, a_pow, dvec, *to_cast)


def _na_row_plan():
    last_q = NA_KROWS - NA_QROWS
    plan = []
    for r0s, qrs in ((lambda lr: 0, lambda lr: lr),
                     (lambda lr: lr, lambda lr: NA_KR // 2 + lr),
                     (lambda lr: last_q - NA_KR // 2, lambda lr: last_q + lr)):
        plan.append([[kl - qrs(lr) + (NA_KR - 1) if r0s(lr) <= kl < r0s(lr) + NA_KR else None
                      for kl in range(NA_KROWS)] for lr in range(NA_QROWS)])
    return plan


def _na_bias_fill(col_ref, bias_s):
    plan = _na_row_plan()
    lane = lax.broadcasted_iota(jnp.int32, (1, 2 * GRID_W), 1)
    masked = jnp.full((GRID_W, 2 * GRID_W), MASK_VALUE, jnp.float32)
    for head in range(bias_s.shape[1]):
        for kind in range(len(plan)):
            for kl in range(NA_KROWS):
                for lp in range(NA_QROWS // 2):
                    a0, a1 = plan[kind][2 * lp][kl], plan[kind][2 * lp + 1][kl]
                    t0 = masked if a0 is None else col_ref[head, a0]
                    t1 = masked if a1 is None else col_ref[head, a1]
                    tile = t0 if a0 == a1 else jnp.where(lane < GRID_W, t0, t1)
                    bias_s[kind, head, kl * GRID_W:(kl + 1) * GRID_W, lp * 2 * GRID_W:(lp + 1) * 2 * GRID_W] = tile


def _na_bias_columns(rpb):
    kc = np.arange(GRID_W)[:, None]
    qc = np.arange(GRID_W)[None, :]
    c0 = np.clip(qc - NA_KC // 2, 0, GRID_W - NA_KC)
    col_ok = (kc >= c0) & (kc < c0 + NA_KC)
    sel_c = ((kc - qc + (NA_KC - 1))[:, :, None] == np.arange(2 * NA_KC - 1)) & col_ok[:, :, None]
    by_col = jnp.einsum('hab,kqb->hakq', rpb.astype(jnp.float32), jnp.asarray(sel_c, jnp.float32), precision=_HI)
    by_col = jnp.where(jnp.asarray(col_ok), by_col * LOG2_E, MASK_VALUE)
    return jnp.concatenate([by_col, by_col], axis=3)


def _row_max(x):
    while x.shape[0] % (2 * SUBLANES) == 0:
        half = x.shape[0] // 2
        x = jnp.maximum(x[:half], x[half:])
    return jnp.max(x, axis=0, keepdims=True)


def _na_kernel(rows, q_ref, k_ref, vt_ref, kc_ref, vct_ref, col_ref, o_ref, bias_ref):
    @pl.when((pl.program_id(0) == 0) & (pl.program_id(1) == 0))
    def _():
        _na_bias_fill(col_ref, bias_ref)

    n_rb = rows // NA_QROWS
    nk = NA_KROWS * GRID_W
    pair = 2 * NA_DH
    lane_first = lax.broadcasted_iota(jnp.int32, (1, pair), 1) < NA_DH
    row_first = lax.broadcasted_iota(jnp.int32, (pair, 1), 0) < NA_DH
    n_heads = q_ref.shape[2] // NA_DH
    n_blocks = q_ref.shape[1] // NA_BLOCK

    def block(j):
        rb = pl.program_id(1) * n_blocks + j
        kind = jnp.where(rb == 0, 0, jnp.where(rb == n_rb - 1, 2, 1))
        first_row = jnp.clip(rb * NA_QROWS - NA_KR // 2, 0, rows - NA_KROWS)
        return kind, pl.multiple_of(first_row * GRID_W, GRID_W), first_row // NA_QROWS

    geometry = [block(j) for j in range(n_blocks)]

    def scores(u):
        j, h = divmod(u, n_heads)
        kind, start, _ = geometry[j]
        cols = slice(h // 2 * pair, (h // 2 + 1) * pair)
        lane_mine = lane_first if h % 2 == 0 else jnp.logical_not(lane_first)
        qm = jnp.where(lane_mine, q_ref[0, j * NA_BLOCK:(j + 1) * NA_BLOCK, cols],
                       jnp.zeros((NA_BLOCK, pair), q_ref.dtype))
        tiles = [_mm_nt(k_ref[0, pl.ds(start + t * NA_BLOCK, NA_BLOCK), cols], qm)
                 + bias_ref[kind, h, t * NA_BLOCK:(t + 1) * NA_BLOCK, :] for t in range(nk // NA_BLOCK)]
        tiles += [_mm_nt(kc_ref[0, t * NA_BLOCK:(t + 1) * NA_BLOCK, cols], qm) for t in range(vct_ref.shape[0])]
        return tiles

    def weights(*tiles):
        mx = functools.reduce(jnp.maximum, [_row_max(t) for t in tiles])
        return [jnp.exp2((t - mx).astype(vt_ref.dtype)) for t in tiles]

    def values(u, *p_tiles):
        j, h = divmod(u, n_heads)
        blk0 = geometry[j][2]
        cols = slice(h // 2 * pair, (h // 2 + 1) * pair)
        row_mine = row_first if h % 2 == 0 else jnp.logical_not(row_first)
        n_win = nk // NA_BLOCK
        vts = [vt_ref[blk0 + t, cols, :] for t in range(n_win)] + [vct_ref[t, cols, :] for t in range(vct_ref.shape[0])]
        o = functools.reduce(lambda a, b: a + b,
                             [_mm(jnp.where(row_mine, v, jnp.ones_like(v)), p) for v, p in zip(vts, p_tiles)])
        num, den = (o[:NA_DH], o[NA_DH:]) if h % 2 == 0 else (o[NA_DH:], o[:NA_DH])
        return num / den

    n_units = n_blocks * n_heads
    s = {0: scores(0), 1: scores(1)}
    p = {0: weights(*s.pop(0))}
    outs = []
    for u in range(n_units):
        if u + 2 < n_units:
            s[u + 2] = scores(u + 2)
        if u + 1 < n_units:
            p[u + 1] = weights(*s.pop(u + 1))
        outs.append(values(u, *p.pop(u)))
        if len(outs) == n_heads:
            j = u // n_heads
            o_ref[0, j * NA_BLOCK:(j + 1) * NA_BLOCK, :] = jnp.concatenate(outs, axis=0).T.astype(o_ref.dtype)
            outs = []


def _na(qk, vt, qk_c, vt_c, bias_cols, na_w):
    b, s, _ = qk.shape
    l = qk_c.shape[1]
    rows = s // GRID_W
    tq = NA_STEP_BLOCKS * NA_BLOCK
    n_kinds = len(_na_row_plan())
    return pl.pallas_call(
        functools.partial(_na_kernel, rows),
        grid=(b, s // tq),
        in_specs=[pl.BlockSpec((1, tq, na_w), lambda bi, rb: (bi, rb, 0)),
                  pl.BlockSpec((1, s, na_w), lambda bi, rb: (bi, 0, 1)),
                  pl.BlockSpec((s // NA_BLOCK, na_w, NA_BLOCK), lambda bi, rb: (bi, 0, 0)),
                  pl.BlockSpec((1, l, na_w), lambda bi, rb: (bi, 0, 1)),
                  pl.BlockSpec((l // NA_BLOCK, na_w, NA_BLOCK), lambda bi, rb: (bi, 0, 0)),
                  _const_spec(bias_cols.shape)],
        out_specs=pl.BlockSpec((1, tq, na_w), lambda bi, rb: (bi, rb, 0)),
        out_shape=jax.ShapeDtypeStruct((b, s, na_w), _MXU_DTYPE),
        scratch_shapes=[pltpu.VMEM((n_kinds, na_w // NA_DH, NA_KROWS * GRID_W, NA_BLOCK), jnp.float32)],
        compiler_params=pltpu.CompilerParams(
            dimension_semantics=("arbitrary", "arbitrary"),
            vmem_limit_bytes=VMEM_LIMIT_BYTES),
        name="na",
    )(qk, qk, vt, qk_c, vt_c, bias_cols)


def _out_ffn_kernel(alpha, x1_ref, yna_ref, ys5_ref, mod_a_ref, mod_b_ref, lng_ref, lnb_ref, wglu_ref, bglu_ref,
                    wout_ref, wup_ref, wdn_ref, o_ref, p_s, y_s, h_s, x2_s):
    @pl.when(pl.program_id(0) == 0)
    def _():
        h_s[...] = jnp.zeros_like(h_s)
        x2_s[...] = jnp.zeros_like(x2_s)

    mod_a = mod_a_ref[0]
    na_w = yna_ref.shape[1]
    stage_a = {}

    def relayout():
        _chunks_to_tokens(ys5_ref, y_s)

    def gated():
        gl = jax.nn.gelu(jnp.concatenate([y_s[v] for v in range(y_s.shape[0])], axis=1))
        gate = jax.nn.sigmoid(_mm(gl.astype(p_s.dtype), wglu_ref[...]) + bglu_ref[...])
        stage_a["glu"] = (gl * gate).astype(p_s.dtype)

    def mixed():
        y = _mm(yna_ref[...], wout_ref[:na_w, :]) + _mm(stage_a["glu"], wout_ref[na_w:, :])
        x2 = _layer_norm(alpha * x1_ref[...] + mod_a[5:6] * y, lng_ref[1:2], lnb_ref[1:2])
        stage_a["x2"] = x2
        stage_a["h"] = (x2 * (1.0 + mod_a[7:8]) + mod_a[6:7]).astype(h_s.dtype)

    mod_b = mod_b_ref[0]
    _swiglu_hidden(h_s[...], wup_ref, p_s, between=(relayout, gated, mixed))
    n_blk = o_ref.shape[0] // NA_BLOCK
    fs = [_mm(p_s[t * NA_BLOCK:(t + 1) * NA_BLOCK, :], wdn_ref[...]) for t in range(n_blk)]
    for t in range(n_blk):
        rows = slice(t * NA_BLOCK, (t + 1) * NA_BLOCK)
        o_ref[rows, :] = _layer_norm(alpha * x2_s[rows, :] + (0.5 * mod_b[8:9]) * fs[t], lng_ref[2:3], lnb_ref[2:3])

    x2_s[...] = stage_a["x2"]
    h_s[...] = stage_a["h"]


def _out_ffn(x1, yna, ys5, mod, tiles_per_mod, lng, lnb, wglu, bglu, wout, wup, wdn, alpha, tm):
    t, d = x1.shape
    ff = wdn.shape[0]
    na_w = yna.shape[1]
    groups, _, cw = ys5.shape
    s5_w = groups * S5_CG
    n = t // tm

    def cur(i):
        return jnp.minimum(i, n - 1)

    def prev(i):
        return jnp.maximum(i - 1, 0)

    return pl.pallas_call(
        functools.partial(_out_ffn_kernel, alpha),
        grid=(n + 1,),
        in_specs=[pl.BlockSpec((tm, d), lambda i: (cur(i), 0)),
                  pl.BlockSpec((tm, na_w), lambda i: (cur(i), 0)),
                  pl.BlockSpec((groups, tm // CHUNK, cw), lambda i: (0, cur(i), 0)),
                  pl.BlockSpec((1, N_MOD, d), lambda i: (cur(i) // tiles_per_mod, 0, 0)),
                  pl.BlockSpec((1, N_MOD, d), lambda i: (prev(i) // tiles_per_mod, 0, 0)),
                  _const_spec(lng.shape), _const_spec(lnb.shape),
                  _const_spec(wglu.shape), _const_spec(bglu.shape), _const_spec(wout.shape),
                  _const_spec(wup.shape), _const_spec(wdn.shape)],
        out_specs=pl.BlockSpec((tm, d), lambda i: (prev(i), 0)),
        out_shape=jax.ShapeDtypeStruct((t, d), jnp.float32),
        scratch_shapes=[pltpu.VMEM((tm, ff), _MXU_DTYPE), pltpu.VMEM((s5_w // LANES, tm, LANES), jnp.float32),
                        pltpu.VMEM((tm, d), _MXU_DTYPE), pltpu.VMEM((tm, d), jnp.float32)],
        compiler_params=pltpu.CompilerParams(dimension_semantics=("arbitrary",),
                                             vmem_limit_bytes=VMEM_LIMIT_BYTES),
        name="out_ffn",
    )(x1, yna, ys5, mod, mod, lng, lnb, wglu, bglu, wout, wup, wdn)


def _pick_tile(n, target):
    t = min(n, target)
    while n % t:
        t //= 2
    return t


def kernel(x, c, ctx, c_ctx, w_ada, b_ada, ln_g, ln_b, ffn1_w_up, ffn1_w_down, w_in, na_rpb, s5_a_re, s5_a_im, s5_log_dt, s5_b_re, s5_b_im, s5_c_re, s5_c_im, s5_d, s5_w_glu, s5_b_glu, w_out, ffn2_w_up, ffn2_w_down):
    depth = w_ada.shape[0]
    assert depth == 1, "single-layer stack only"
    bsz, seq, d = x.shape
    l_ctx = ctx.shape[1]
    s5_w = s5_w_glu.shape[1]
    na_w = w_out.shape[1] - s5_w
    groups = s5_w // S5_CG
    rows = seq // GRID_W
    assert bsz % 8 == 0 and seq % (NA_STEP_BLOCKS * NA_BLOCK) == 0 and rows >= NA_KROWS
    assert l_ctx % NA_BLOCK == 0 and groups % 2 == 0 and s5_a_re.shape[3] == S5_P and 2 * S5_P == LANES
    assert na_w % (2 * NA_DH) == 0 and ffn1_w_down.shape[1] % MXU_TILE == 0
    alpha = (2.0 * depth) ** 0.25
    cd = _MXU_DTYPE

    n_rows = -(-(bsz + 1) // 8) * 8
    cc = jnp.zeros((n_rows, d), jnp.float32).at[:bsz].set(c).at[bsz].set(c_ctx)
    n_ada = w_ada.shape[2]
    mod = _ada(cc, w_ada[0], b_ada[0][None, :], _pick_tile(n_ada, 1152)).reshape(n_rows, N_MOD, d)

    lng, lnb = ln_g[0], ln_b[0]
    wup1, wdn1 = ffn1_w_up[0].astype(cd), ffn1_w_down[0].astype(cd)
    win = w_in[0].astype(cd)

    tm = _pick_tile(seq, 512)
    x1, qk, vt, xs = _ffn_in(x.reshape(bsz * seq, d), mod, seq // tm, lng, lnb, wup1, wdn1, win, alpha, na_w, tm)
    tmc = _pick_tile(bsz * l_ctx, 512)
    _, qk_c, vt_c, xs_c = _ffn_in(ctx.reshape(bsz * l_ctx, d), mod[bsz:bsz + 1], None, lng, lnb, wup1, wdn1, win,
                                  alpha, na_w, tmc)

    tables = _s5_tables(s5_a_re[0], s5_a_im[0], s5_log_dt[0], s5_b_re[0], s5_b_im[0], s5_c_re[0], s5_c_im[0],
                        s5_d[0])
    y_s5, wup2, wdn2, wout, wglu = _s5(xs, xs_c, *tables, bsz,
                                       (ffn2_w_up[0], ffn2_w_down[0], w_out[0], s5_w_glu[0]))

    y_na = _na(qk.reshape(bsz, seq, 2 * na_w), vt, qk_c.reshape(bsz, l_ctx, 2 * na_w), vt_c, _na_bias_columns(na_rpb[0]),
               na_w).reshape(bsz * seq, na_w)

    out = _out_ffn(x1, y_na, y_s5, mod, seq // tm, lng, lnb, wglu, s5_b_glu[0][None, :], wout, wup2, wdn2,
                   alpha, tm)
    return out.reshape(bsz, seq, d)
```
